```python
import math
import jax, jax.numpy as jnp
from jax import lax
import numpy as np

D_MODEL = 1024
BATCH = 8
SEQ = 4096
DEPTH = 1

DSWA_CONFIGS = ((128, 1), (512, 4), (2048, 16))
N_GROUPS = len(DSWA_CONFIGS)
DSWA_HEADS = 4
DSWA_HEAD_DIM = 128
DSWA_WIDTH = N_GROUPS * DSWA_HEADS * DSWA_HEAD_DIM
DSWA_OUT = DSWA_HEADS * DSWA_HEAD_DIM
DSWA_BLK = 128
D_RNN = 1024
LRU_BLOCKS = 16
LRU_BW = D_RNN // LRU_BLOCKS
CONV_W = 4
LRU_C = 8.0
MEM_LEN = 256
MEM_HEADS = 4
MEM_HEAD_DIM = 128
MEM_WIDTH = MEM_HEADS * MEM_HEAD_DIM
N_BRANCHES = 3
IN_SPLITS = (DSWA_WIDTH, DSWA_WIDTH, DSWA_WIDTH, D_RNN, D_RNN, MEM_WIDTH, N_BRANCHES * D_MODEL)
D_IN = sum(IN_SPLITS)
PEER_HEADS = 8
PEER_KEY_DIM = 256
PEER_HALF = PEER_KEY_DIM // 2
N_KEYS = 128
N_EXPERTS = N_KEYS * N_KEYS
PEER_TOPK = 16
PEER_CHUNK = 128
ALPHA = (2.0 * DEPTH) ** 0.25
BETA = (8.0 * DEPTH) ** -0.25
LN_EPS = 1e-5
NEG_INF = -1e30

kernel_name = "hybrid_dswa_rglru_mem_peer_deepnorm"


def _layernorm(h, g, b):
    h = h.astype(jnp.float32)
    mu = jnp.mean(h, axis=-1, keepdims=True)
    var = jnp.mean(jnp.square(h - mu), axis=-1, keepdims=True)
    return (h - mu) * lax.rsqrt(var + LN_EPS) * g.astype(jnp.float32) + b.astype(jnp.float32)


def _dilated_window_attention(q, k, v, window, dilation):
    B, S, H, hd = q.shape
    steps = window // dilation
    assert steps <= DSWA_BLK
    span = dilation * DSWA_BLK
    sp = -(-S // span) * span
    m_len = sp // dilation
    nb = m_len // DSWA_BLK

    def to_sub(t):
        t = jnp.pad(t.astype(jnp.float32), ((0, 0), (0, sp - S), (0, 0), (0, 0)))
        t = t.reshape(B, m_len, dilation, H, hd).transpose(0, 2, 1, 3, 4)
        return t.reshape(B, dilation, nb, DSWA_BLK, H, hd)

    def with_prev(t):
        prev = jnp.pad(t[:, :, :-1], ((0, 0), (0, 0), (1, 0), (0, 0), (0, 0), (0, 0)))
        return jnp.concatenate([prev, t], axis=3)

    qb = to_sub(q)
    kw = with_prev(to_sub(k))
    vw = with_prev(to_sub(v))
    s = jnp.einsum('brnqhd,brnkhd->brnhqk', qb, kw) * (1.0 / math.sqrt(hd))
    qi = jnp.arange(DSWA_BLK)[:, None]
    kj = jnp.arange(2 * DSWA_BLK)[None, :]
    dist = DSWA_BLK + qi - kj
    band = (dist >= 0) & (dist <= steps)
    has_prev = (jnp.arange(nb) > 0)[:, None, None] | (kj >= DSWA_BLK)[None]
    valid = band[None] & has_prev
    s = jnp.where(valid[None, None, :, None], s, NEG_INF)
    lse = jax.nn.logsumexp(s, axis=-1)
    p = jnp.exp(s - lse[..., None])
    o = jnp.einsum('brnhqk,brnkhd->brnqhd', p, vw)
    o = o.reshape(B, dilation, m_len, H, hd).transpose(0, 2, 1, 3, 4).reshape(B, sp, H, hd)[:, :S]
    lse = lse.transpose(0, 1, 2, 4, 3).reshape(B, dilation, m_len, H)
    lse = lse.transpose(0, 2, 1, 3).reshape(B, sp, H)[:, :S]
    return o, lse


def _linear_recurrence(a, b):
    def comb(left, right):
        al, bl = left
        ar, br = right
        return al * ar, ar * bl + br
    _, h = lax.associative_scan(comb, (a, b), axis=1)
    return h


def _rglru_branch(xr, yg, conv_w, conv_b, wa, ba, wx, bx, lam):
    B, S, _ = xr.shape
    xc = lax.conv_general_dilated(
        xr.astype(jnp.float32), conv_w.astype(jnp.float32)[:, None, :],
        window_strides=(1,), padding=[(CONV_W - 1, 0)],
        dimension_numbers=('NWC', 'WIO', 'NWC'), feature_group_count=D_RNN,
    ) + conv_b.astype(jnp.float32)
    xh = xc.reshape(B, S, LRU_BLOCKS, LRU_BW)
    r = jax.nn.sigmoid(jnp.einsum('bsni,nij->bsnj', xh, wa.astype(jnp.float32)).reshape(B, S, D_RNN) + ba)
    i = jax.nn.sigmoid(jnp.einsum('bsni,nij->bsnj', xh, wx.astype(jnp.float32)).reshape(B, S, D_RNN) + bx)
    log_a = -LRU_C * r * jax.nn.softplus(-lam.astype(jnp.float32))
    a = jnp.exp(log_a)
    mult = jnp.sqrt(-jnp.expm1(2.0 * log_a))
    h = _linear_recurrence(a, mult * i * xc)
    return h * jax.nn.gelu(yg.astype(jnp.float32))


def _memory_attention(mq, mem, w_mem_kv):
    B, S, _ = mq.shape
    q = mq.astype(jnp.float32).reshape(B, S, MEM_HEADS, MEM_HEAD_DIM)
    kv = jnp.einsum('bmd,de->bme', mem.astype(jnp.float32), w_mem_kv.astype(jnp.float32))
    k, v = jnp.split(kv.reshape(B, MEM_LEN, 2, MEM_HEADS, MEM_HEAD_DIM), 2, axis=2)
    k, v = k[:, :, 0], v[:, :, 0]
    s = jnp.einsum('bshd,bmhd->bhsm', q, k) * (1.0 / math.sqrt(MEM_HEAD_DIM))
    p = jax.nn.softmax(s, axis=-1)
    return jnp.einsum('bhsm,bmhd->bshd', p, v).reshape(B, S, MEM_WIDTH)


def _peer(x, wq, keys, u, v):
    B, S, D = x.shape
    q = jnp.einsum('bsd,de->bse', x, wq.astype(jnp.float32)).reshape(B, S, PEER_HEADS, 2, PEER_HALF)
    sc = jnp.einsum('bshpc,hpnc->bshpn', q, keys.astype(jnp.float32))
    s_top, i_top = lax.top_k(sc, PEER_TOPK)
    cand = s_top[..., 0, :, None] + s_top[..., 1, None, :]
    c_s, c_i = lax.top_k(cand.reshape(B, S, PEER_HEADS, PEER_TOPK * PEER_TOPK), PEER_TOPK)
    ia = jnp.take_along_axis(i_top[..., 0, :], c_i // PEER_TOPK, axis=-1)
    ib = jnp.take_along_axis(i_top[..., 1, :], c_i % PEER_TOPK, axis=-1)
    ids = ia * N_KEYS + ib
    g = jax.nn.softmax(c_s, axis=-1)
    n_chunks = (B * S) // PEER_CHUNK
    xs = x.reshape(n_chunks, PEER_CHUNK, D)
    ids = ids.reshape(n_chunks, PEER_CHUNK, PEER_HEADS, PEER_TOPK)
    g = g.reshape(n_chunks, PEER_CHUNK, PEER_HEADS, PEER_TOPK)

    def block(args):
        xc, idc, gc = args
        act = jax.nn.gelu(jnp.einsum('thkd,td->thk', u[idc].astype(jnp.float32), xc), approximate=False)
        return jnp.einsum('thk,thkd->td', gc * act, v[idc].astype(jnp.float32))

    return lax.map(block, (xs, ids, g)).reshape(B, S, D)


def _hybrid_layer(x, mem, w_in, b_gate, conv_w, conv_b, lru_wa, lru_ba, lru_wx, lru_bx, lru_lambda,
                  w_mem_kv, w_br_attn, w_br_lru, w_br_mem, w_out, ln1_g, ln1_b,
                  peer_wq, peer_keys, peer_u, peer_v, ln2_g, ln2_b):
    B, S, D = x.shape
    z = jnp.einsum('bsd,de->bse', x, w_in.astype(jnp.float32))
    offs = [0]
    for w in IN_SPLITS:
        offs.append(offs[-1] + w)
    q, k, v, xr, yg, mq, gl = [z[..., offs[j]:offs[j + 1]] for j in range(len(IN_SPLITS))]

    hs = (B, S, N_GROUPS, DSWA_HEADS, DSWA_HEAD_DIM)
    q, k, v = q.reshape(hs), k.reshape(hs), v.reshape(hs)
    outs, lses = [], []
    for gi, (win, dil) in enumerate(DSWA_CONFIGS):
        o_g, l_g = _dilated_window_attention(q[:, :, gi], k[:, :, gi], v[:, :, gi], win, dil)
        outs.append(o_g)
        lses.append(l_g)
    wgt = jax.nn.softmax(jnp.stack(lses, axis=0), axis=0)
    attn = jnp.sum(wgt[..., None] * jnp.stack(outs, axis=0), axis=0).reshape(B, S, DSWA_OUT)

    rec = _rglru_branch(xr, yg, conv_w, conv_b, lru_wa, lru_ba, lru_wx, lru_bx, lru_lambda)

    memo = _memory_attention(mq, mem, w_mem_kv)

    gates = jax.nn.sigmoid(gl.reshape(B, S, N_BRANCHES, D) + b_gate.astype(jnp.float32))
    merged = (gates[:, :, 0] * (attn @ w_br_attn.astype(jnp.float32))
              + gates[:, :, 1] * (rec @ w_br_lru.astype(jnp.float32))
              + gates[:, :, 2] * (memo @ w_br_mem.astype(jnp.float32)))
    mix = merged @ w_out.astype(jnp.float32)
    x1 = _layernorm(ALPHA * x + mix, ln1_g, ln1_b)

    ffn = _peer(x1, peer_wq, peer_keys, peer_u, peer_v)
    return _layernorm(ALPHA * x1 + ffn, ln2_g, ln2_b)


def setup_inputs(seed: int = 0) -> dict:
    key = jax.random.key(seed)
    ks = jax.random.split(key, 26)
    f32 = jnp.float32
    L, D = DEPTH, D_MODEL

    def nrm(k, shape, scale):
        return jax.random.normal(k, shape, f32) * scale

    a0 = jax.random.uniform(ks[10], (L, D_RNN), f32, 0.9, 0.999)
    s0 = a0 ** (1.0 / LRU_C)
    lam = jnp.log(s0) - jnp.log1p(-s0)
    return {
        'x': nrm(ks[0], (BATCH, SEQ, D), 1.0),
        'mem': nrm(ks[1], (BATCH, MEM_LEN, D), 1.0),
        'w_in': nrm(ks[2], (L, D, D_IN), D ** -0.5),
        'b_gate': nrm(ks[3], (L, N_BRANCHES, D), 0.1),
        'conv_w': nrm(ks[4], (L, CONV_W, D_RNN), CONV_W ** -0.5),
        'conv_b': nrm(ks[5], (L, D_RNN), 0.01),
        'lru_wa': nrm(ks[6], (L, LRU_BLOCKS, LRU_BW, LRU_BW), LRU_BW ** -0.5),
        'lru_ba': nrm(ks[7], (L, D_RNN), 0.01),
        'lru_wx': nrm(ks[8], (L, LRU_BLOCKS, LRU_BW, LRU_BW), LRU_BW ** -0.5),
        'lru_bx': nrm(ks[9], (L, D_RNN), 0.01),
        'lru_lambda': lam,
        'w_mem_kv': nrm(ks[11], (L, D, 2 * MEM_WIDTH), D ** -0.5),
        'w_br_attn': nrm(ks[12], (L, DSWA_OUT, D), BETA * DSWA_OUT ** -0.5),
        'w_br_lru': nrm(ks[13], (L, D_RNN, D), BETA * D_RNN ** -0.5),
        'w_br_mem': nrm(ks[14], (L, MEM_WIDTH, D), BETA * MEM_WIDTH ** -0.5),
        'w_out': nrm(ks[15], (L, D, D), BETA * D ** -0.5),
        'ln1_g': 1.0 + nrm(ks[16], (L, D), 0.02),
        'ln1_b': nrm(ks[17], (L, D), 0.02),
        'peer_wq': nrm(ks[18], (L, D, PEER_HEADS * PEER_KEY_DIM), D ** -0.5),
        'peer_keys': nrm(ks[19], (L, PEER_HEADS, 2, N_KEYS, PEER_HALF), PEER_HALF ** -0.5),
        'peer_u': nrm(ks[20], (L, N_EXPERTS, D), D ** -0.5),
        'peer_v': nrm(ks[21], (L, N_EXPERTS, D), BETA * PEER_HEADS ** -0.5),
        'ln2_g': 1.0 + nrm(ks[22], (L, D), 0.02),
        'ln2_b': nrm(ks[23], (L, D), 0.02),
    }


def reference(x, mem, w_in, b_gate, conv_w, conv_b, lru_wa, lru_ba, lru_wx, lru_bx, lru_lambda,
              w_mem_kv, w_br_attn, w_br_lru, w_br_mem, w_out, ln1_g, ln1_b,
              peer_wq, peer_keys, peer_u, peer_v, ln2_g, ln2_b):
    h = x.astype(jnp.float32)
    for l in range(DEPTH):
        h = _hybrid_layer(h, mem, w_in[l], b_gate[l], conv_w[l], conv_b[l], lru_wa[l], lru_ba[l],
                          lru_wx[l], lru_bx[l], lru_lambda[l], w_mem_kv[l], w_br_attn[l], w_br_lru[l],
                          w_br_mem[l], w_out[l], ln1_g[l], ln1_b[l], peer_wq[l], peer_keys[l],
                          peer_u[l], peer_v[l], ln2_g[l], ln2_b[l])
    return h.astype(x.dtype)
```

```python
import functools
import math

import jax
import jax.numpy as jnp
from jax import lax
from jax.experimental import pallas as pl
from jax.experimental.pallas import tpu as pltpu

F32 = jnp.float32
BF16 = jnp.bfloat16

D_MODEL = 1024
N_GROUPS = 3
DSWA_DILATIONS = (1, 4, 16)
DSWA_HEADS = 4
HEAD_DIM = 128
DSWA_BLK = 128
GROUP_W = DSWA_HEADS * HEAD_DIM
DSWA_WIDTH = N_GROUPS * GROUP_W
D_RNN = 1024
LRU_BLOCKS = 16
LRU_BW = D_RNN // LRU_BLOCKS
CONV_W = 4
LRU_C = 8.0
MEM_HEADS = 4
MEM_WIDTH = MEM_HEADS * HEAD_DIM
PEER_HEADS = 8
PEER_KEY_DIM = 256
N_KEYS = 128
N_EXPERTS = N_KEYS * N_KEYS
PEER_TOPK = 16
ALPHA = 2.0 ** 0.25
LN_EPS = 1e-5
NEG_INF = -1e30

MXU_TILE = 256
VMEM_LIMIT = 56 * 1024 * 1024


def _cparams(sem):
    return pltpu.CompilerParams(dimension_semantics=sem, vmem_limit_bytes=VMEM_LIMIT)


def _dot(a, b):
    return jnp.dot(a, b, preferred_element_type=F32)


def _dot_nt(a, b):
    return lax.dot_general(a, b, (((1,), (1,)), ((), ())), preferred_element_type=F32)


def _gelu_erf(x):
    return 0.5 * x * (1.0 + lax.erf(x * (1.0 / math.sqrt(2.0))))


def _layernorm(h, g, b):
    mu = jnp.mean(h, axis=-1, keepdims=True)
    c = h - mu
    var = jnp.mean(c * c, axis=-1, keepdims=True)
    return c * lax.rsqrt(var + LN_EPS) * g + b


def _matmul_kernel(x_ref, w_ref, o_ref):
    o_ref[...] = _dot(x_ref[...], w_ref[...]).astype(o_ref.dtype)


def _matmul(x, w, col_off, n_cols, out_dtype, tm=1024, tn=1024):
    m, k = x.shape
    tn = min(tn, n_cols)
    tm = min(tm, m)
    assert m % tm == 0 and n_cols % tn == 0 and col_off % tn == 0
    off = col_off // tn
    return pl.pallas_call(
        _matmul_kernel,
        grid=(m // tm, n_cols // tn),
        in_specs=[pl.BlockSpec((tm, k), lambda i, j: (i, 0)),
                  pl.BlockSpec((k, tn), lambda i, j: (0, off + j))],
        out_specs=pl.BlockSpec((tm, tn), lambda i, j: (i, j)),
        out_shape=jax.ShapeDtypeStruct((m, n_cols), out_dtype),
        compiler_params=_cparams(("parallel", "arbitrary")),
        name="matmul",
    )(x, w)


def _dswa_kernel(q_ref, kc_ref, kp_ref, vc_ref, vp_ref, o_ref, lse_ref, *, nq):
    row = lax.broadcasted_iota(jnp.int32, (DSWA_BLK, DSWA_BLK), 0)
    col = lax.broadcasted_iota(jnp.int32, (DSWA_BLK, DSWA_BLK), 1)
    mask_cur = col <= row
    mask_band = col >= row
    mask_first = col >= row + jnp.where(pl.program_id(2) == 0, DSWA_BLK, 0)
    scale = 1.0 / math.sqrt(HEAD_DIM)
    for j in range(nq):
        rows = slice(j * DSWA_BLK, (j + 1) * DSWA_BLK)
        q = q_ref[0, rows, :]
        kcur = kc_ref[0, rows, :]
        vcur = vc_ref[0, rows, :]
        if j == 0:
            kprev = kp_ref[0]
            vprev = vp_ref[0]
            mask_prev = mask_first
        else:
            prows = slice((j - 1) * DSWA_BLK, j * DSWA_BLK)
            kprev = kc_ref[0, prows, :]
            vprev = vc_ref[0, prows, :]
            mask_prev = mask_band
        for h in range(DSWA_HEADS):
            hs = slice(h * HEAD_DIM, (h + 1) * HEAD_DIM)
            qh = q[:, hs]
            s_p = jnp.where(mask_prev, _dot_nt(qh, kprev[:, hs]) * scale, NEG_INF)
            s_c = jnp.where(mask_cur, _dot_nt(qh, kcur[:, hs]) * scale, NEG_INF)
            m = jnp.maximum(jnp.max(s_p, axis=-1, keepdims=True),
                            jnp.max(s_c, axis=-1, keepdims=True))
            p_p = jnp.exp(s_p - m)
            p_c = jnp.exp(s_c - m)
            l = jnp.sum(p_p, axis=-1, keepdims=True) + jnp.sum(p_c, axis=-1, keepdims=True)
            o = _dot(p_p.astype(BF16), vprev[:, hs]) + _dot(p_c.astype(BF16), vcur[:, hs])
            o_ref[0, rows, hs] = (o / l).astype(o_ref.dtype)
            lse_ref[0, rows, hs] = jnp.broadcast_to(m + jnp.log(l), (DSWA_BLK, HEAD_DIM))


def _dswa_group(qkv, group, dilation):
    b, s, w = qkv.shape
    m_len = s // dilation
    assert s % (dilation * DSWA_BLK) == 0
    nb = m_len // DSWA_BLK
    nq = min(8, nb)
    n_steps = nb // nq
    nblk = w // GROUP_W
    view = qkv.reshape(b, m_len, dilation * w)
    qc, kc, vc = group, N_GROUPS + group, 2 * N_GROUPS + group

    def cur(c):
        return pl.BlockSpec((1, nq * DSWA_BLK, GROUP_W), lambda bi, r, n: (bi, n, r * nblk + c))

    def prev(c):
        return pl.BlockSpec((1, DSWA_BLK, GROUP_W),
                            lambda bi, r, n: (bi, jnp.maximum(n * nq - 1, 0), r * nblk + c))

    out_spec = pl.BlockSpec((1, nq * DSWA_BLK, GROUP_W), lambda bi, r, n: (bi, n, r))
    o, lse = pl.pallas_call(
        functools.partial(_dswa_kernel, nq=nq),
        grid=(b, dilation, n_steps),
        in_specs=[cur(qc), cur(kc), prev(kc), cur(vc), prev(vc)],
        out_specs=[out_spec, out_spec],
        out_shape=[jax.ShapeDtypeStruct((b, m_len, dilation * GROUP_W), BF16),
                   jax.ShapeDtypeStruct((b, m_len, dilation * GROUP_W), F32)],
        compiler_params=_cparams(("parallel", "parallel", "arbitrary")),
        name=f"dswa_g{group}",
    )(view, view, view, view, view)
    return o.reshape(b, s, GROUP_W), lse.reshape(b, s, GROUP_W)


def _shift_rows(x, k, fill):
    rolled = pltpu.roll(x, k, 0)
    row = lax.broadcasted_iota(jnp.int32, x.shape, 0)
    return jnp.where(row >= k, rolled, fill)


def _rglru_kernel(xr_ref, halo_ref, yg_ref, cw_ref, cb_ref, wa_ref, ba_ref, wx_ref, bx_ref,
                  lam_ref, o_ref, h_ref, *, ts):
    si = pl.program_id(1)

    @pl.when(si == 0)
    def _():
        h_ref[...] = jnp.zeros_like(h_ref)

    x = xr_ref[0]
    halo = jnp.where(si > 0, halo_ref[0], 0.0)
    xfull = jnp.concatenate([halo, x], axis=0)
    xc = cb_ref[...] + cw_ref[CONV_W - 1:CONV_W, :] * x
    for k in range(1, CONV_W):
        xc = xc + cw_ref[CONV_W - 1 - k:CONV_W - k, :] * pltpu.roll(xfull, k, 0)[8:]

    xb = xc.astype(BF16)
    nblk = D_RNN // MXU_TILE
    ra = jnp.concatenate(
        [_dot(xb[:, c * MXU_TILE:(c + 1) * MXU_TILE], wa_ref[c]) for c in range(nblk)], axis=1)
    rx = jnp.concatenate(
        [_dot(xb[:, c * MXU_TILE:(c + 1) * MXU_TILE], wx_ref[c]) for c in range(nblk)], axis=1)
    r = jax.nn.sigmoid(ra + ba_ref[...])
    gate_i = jax.nn.sigmoid(rx + bx_ref[...])
    neg_lam = -lam_ref[...]
    softplus = jnp.maximum(neg_lam, 0.0) + jnp.log1p(jnp.exp(-jnp.abs(neg_lam)))
    log_a = (-LRU_C) * r * softplus
    a = jnp.exp(log_a)
    bterm = jnp.sqrt(1.0 - a * a) * gate_i * xc

    step = 1
    while step < ts:
        a_s = _shift_rows(a, step, 1.0)
        b_s = _shift_rows(bterm, step, 0.0)
        bterm = a * b_s + bterm
        a = a * a_s
        step *= 2
    h = a * h_ref[0:1, :] + bterm
    h_ref[0:1, :] = h[ts - 1:ts, :]
    o_ref[0] = (h * jax.nn.gelu(yg_ref[0], approximate=True)).astype(o_ref.dtype)


def _rglru(xr_yg, conv_w, conv_b, wa_bd, ba, wx_bd, bx, lam, ts=256):
    b, s, _ = xr_yg.shape
    assert s % ts == 0
    full = lambda shape: pl.BlockSpec(shape, lambda bi, si: (0,) * len(shape))
    return pl.pallas_call(
        functools.partial(_rglru_kernel, ts=ts),
        grid=(b, s // ts),
        in_specs=[pl.BlockSpec((1, ts, D_RNN), lambda bi, si: (bi, si, 0)),
                  pl.BlockSpec((1, 8, D_RNN),
                               lambda bi, si: (bi, jnp.maximum(si * (ts // 8) - 1, 0), 0)),
                  pl.BlockSpec((1, ts, D_RNN), lambda bi, si: (bi, si, 1)),
                  full((CONV_W, D_RNN)), full((1, D_RNN)),
                  full(wa_bd.shape), full((1, D_RNN)),
                  full(wx_bd.shape), full((1, D_RNN)), full((1, D_RNN))],
        out_specs=pl.BlockSpec((1, ts, D_RNN), lambda bi, si: (bi, si, 0)),
        out_shape=jax.ShapeDtypeStruct((b, s, D_RNN), BF16),
        scratch_shapes=[pltpu.VMEM((8, D_RNN), F32)],
        compiler_params=_cparams(("parallel", "arbitrary")),
        name="rglru",
    )(xr_yg, xr_yg, xr_yg, conv_w, conv_b, wa_bd, ba, wx_bd, bx, lam)


def _memattn_kernel(q_ref, kv_ref, o_ref):
    scale = 1.0 / math.sqrt(HEAD_DIM)
    for h in range(MEM_HEADS):
        hs = slice(h * HEAD_DIM, (h + 1) * HEAD_DIM)
        vs = slice(MEM_WIDTH + h * HEAD_DIM, MEM_WIDTH + (h + 1) * HEAD_DIM)
        s = _dot_nt(q_ref[0, :, hs], kv_ref[0, :, hs]) * scale
        m = jnp.max(s, axis=-1, keepdims=True)
        p = jnp.exp(s - m)
        l = jnp.sum(p, axis=-1, keepdims=True)
        o_ref[0, :, hs] = (_dot(p.astype(BF16), kv_ref[0, :, vs]) / l).astype(o_ref.dtype)


def _memattn(mq, kv, ts=1024):
    b, s, _ = mq.shape
    mem_len = kv.shape[1]
    return pl.pallas_call(
        _memattn_kernel,
        grid=(b, s // ts),
        in_specs=[pl.BlockSpec((1, ts, MEM_WIDTH), lambda bi, si: (bi, si, 0)),
                  pl.BlockSpec((1, mem_len, 2 * MEM_WIDTH), lambda bi, si: (bi, 0, 0))],
        out_specs=pl.BlockSpec((1, ts, MEM_WIDTH), lambda bi, si: (bi, si, 0)),
        out_shape=jax.ShapeDtypeStruct((b, s, MEM_WIDTH), BF16),
        compiler_params=_cparams(("parallel", "arbitrary")),
        name="memattn",
    )(mq, kv)


def _mix_kernel(o0_ref, o1_ref, o2_ref, l0_ref, l1_ref, l2_ref, rec_ref, memo_ref, gl_ref, x_ref,
                bg_ref, wa_ref, wl_ref, wm_ref, wo_ref, g_ref, b_ref, x1_ref, x1b_ref):
    l0, l1, l2 = l0_ref[...], l1_ref[...], l2_ref[...]
    mx = jnp.maximum(jnp.maximum(l0, l1), l2)
    e0, e1, e2 = jnp.exp(l0 - mx), jnp.exp(l1 - mx), jnp.exp(l2 - mx)
    attn = (e0 * o0_ref[...].astype(F32) + e1 * o1_ref[...].astype(F32)
            + e2 * o2_ref[...].astype(F32)) / (e0 + e1 + e2)
    d = D_MODEL
    gate = lambda j: jax.nn.sigmoid(gl_ref[:, j * d:(j + 1) * d] + bg_ref[j:j + 1, :])
    merged = (gate(0) * _dot(attn.astype(BF16), wa_ref[...])
              + gate(1) * _dot(rec_ref[...], wl_ref[...])
              + gate(2) * _dot(memo_ref[...], wm_ref[...]))
    mix = _dot(merged.astype(BF16), wo_ref[...])
    x1 = _layernorm(ALPHA * x_ref[...] + mix, g_ref[...], b_ref[...])
    x1_ref[...] = x1
    x1b_ref[...] = x1.astype(BF16)


def _mix(os, lses, rec, memo, gl, x, b_gate, wa, wl, wm, wo, g, bta, tt=512):
    t = x.shape[0]
    rows = lambda w: pl.BlockSpec((tt, w), lambda i: (i, 0))
    full = lambda a: pl.BlockSpec(a.shape, lambda i: (0,) * a.ndim)
    return pl.pallas_call(
        _mix_kernel,
        grid=(t // tt,),
        in_specs=[rows(GROUP_W)] * 6 + [rows(D_RNN), rows(MEM_WIDTH), rows(3 * D_MODEL), rows(D_MODEL),
                                        full(b_gate), full(wa), full(wl), full(wm), full(wo),
                                        full(g), full(bta)],
        out_specs=[rows(D_MODEL), rows(D_MODEL)],
        out_shape=[jax.ShapeDtypeStruct((t, D_MODEL), F32), jax.ShapeDtypeStruct((t, D_MODEL), BF16)],
        compiler_params=_cparams(("parallel",)),
        name="mix_ln1",
    )(*os, *lses, rec, memo, gl, x, b_gate, wa, wl, wm, wo, g, bta)


def _peer_scores_kernel(x_ref, wq_ref, kbd_ref, o_ref):
    q = _dot(x_ref[...], wq_ref[...]).astype(BF16)
    for h in range(PEER_HEADS):
        hs = slice(h * PEER_KEY_DIM, (h + 1) * PEER_KEY_DIM)
        o_ref[:, hs] = _dot(q[:, hs], kbd_ref[h])


def _peer_scores(x1b, wq, kbd, tt=512):
    t = x1b.shape[0]
    w = PEER_HEADS * PEER_KEY_DIM
    return pl.pallas_call(
        _peer_scores_kernel,
        grid=(t // tt,),
        in_specs=[pl.BlockSpec((tt, D_MODEL), lambda i: (i, 0)),
                  pl.BlockSpec(wq.shape, lambda i: (0, 0)),
                  pl.BlockSpec(kbd.shape, lambda i: (0, 0, 0))],
        out_specs=pl.BlockSpec((tt, w), lambda i: (i, 0)),
        out_shape=jax.ShapeDtypeStruct((t, w), F32),
        compiler_params=_cparams(("parallel",)),
        name="peer_scores",
    )(x1b, wq, kbd)


def _peer_mix_kernel(x1b_ref, x1_ref, u_ref, v_ref, sb_ref, sa_ref, ga_ref, tau_ref, mb_ref,
                     g_ref, b_ref, o_ref, acc_ref, eb_ref, taub_ref, *, ib):
    e = pl.program_id(1)
    tt = x1b_ref.shape[0]

    @pl.when(e == 0)
    def _():
        acc_ref[...] = jnp.zeros_like(acc_ref)
        for h in range(PEER_HEADS):
            hs = slice(h * N_KEYS, (h + 1) * N_KEYS)
            eb_ref[:, hs] = jnp.exp(sb_ref[:, hs] - mb_ref[:, h:h + 1])
            taub_ref[:, hs] = jnp.broadcast_to(tau_ref[:, h:h + 1], (tt, N_KEYS))

    hmat = _dot_nt(x1b_ref[...], u_ref[...])
    sa = sa_ref[0]
    ga = ga_ref[0]
    ys = []
    for ii in range(ib):
        w = jnp.zeros((tt, N_KEYS), F32)
        for h in range(PEER_HEADS):
            hs = slice(h * N_KEYS, (h + 1) * N_KEYS)
            c = ii * PEER_HEADS + h
            sel = (sb_ref[:, hs] + sa[:, c:c + 1]) >= taub_ref[:, hs]
            w = w + jnp.where(sel, eb_ref[:, hs] * ga[:, c:c + 1], 0.0)
        hblk = hmat[:, ii * N_KEYS:(ii + 1) * N_KEYS]
        ys.append((w * _gelu_erf(hblk)).astype(BF16))
    y = jnp.concatenate(ys, axis=1)
    acc_ref[...] += _dot(y, v_ref[...])

    @pl.when(e == pl.num_programs(1) - 1)
    def _():
        o_ref[...] = _layernorm(ALPHA * x1_ref[...] + acc_ref[...], g_ref[...], b_ref[...])


def _peer_mix(x1b, x1, u, v, sb, sa2, ga2, tau, mb, g, bta, tt=512, te=1024):
    t = x1.shape[0]
    ib = te // N_KEYS
    ne = N_EXPERTS // te
    return pl.pallas_call(
        functools.partial(_peer_mix_kernel, ib=ib),
        grid=(t // tt, ne),
        in_specs=[pl.BlockSpec((tt, D_MODEL), lambda i, e: (i, 0)),
                  pl.BlockSpec((tt, D_MODEL), lambda i, e: (i, 0)),
                  pl.BlockSpec((te, D_MODEL), lambda i, e: (e, 0)),
                  pl.BlockSpec((te, D_MODEL), lambda i, e: (e, 0)),
                  pl.BlockSpec((tt, PEER_HEADS * N_KEYS), lambda i, e: (i, 0)),
                  pl.BlockSpec((1, tt, ib * PEER_HEADS), lambda i, e: (e, i, 0)),
                  pl.BlockSpec((1, tt, ib * PEER_HEADS), lambda i, e: (e, i, 0)),
                  pl.BlockSpec((tt, PEER_HEADS), lambda i, e: (i, 0)),
                  pl.BlockSpec((tt, PEER_HEADS), lambda i, e: (i, 0)),
                  pl.BlockSpec((1, D_MODEL), lambda i, e: (0, 0)),
                  pl.BlockSpec((1, D_MODEL), lambda i, e: (0, 0))],
        out_specs=pl.BlockSpec((tt, D_MODEL), lambda i, e: (i, 0)),
        out_shape=jax.ShapeDtypeStruct((t, D_MODEL), F32),
        scratch_shapes=[pltpu.VMEM((tt, D_MODEL), F32),
                        pltpu.VMEM((tt, PEER_HEADS * N_KEYS), F32),
                        pltpu.VMEM((tt, PEER_HEADS * N_KEYS), F32)],
        compiler_params=_cparams(("parallel", "arbitrary")),
        name="peer_mix_ln2",
    )(x1b, x1, u, v, sb, sa2, ga2, tau, mb, g, bta)


def _block_diag(w, per_tile):
    n, k, _ = w.shape
    w = w.reshape(n // per_tile, per_tile, k, k)
    eye = jnp.eye(per_tile, dtype=w.dtype)
    out = jnp.einsum('tpij,pq->tpiqj', w, eye)
    return out.reshape(n // per_tile, per_tile * k, per_tile * k)


def _layer(x, mem, w_in, b_gate, conv_w, conv_b, lru_wa, lru_ba, lru_wx, lru_bx, lru_lambda,
           w_mem_kv, w_br_attn, w_br_lru, w_br_mem, w_out, ln1_g, ln1_b,
           peer_wq, peer_keys, peer_u, peer_v, ln2_g, ln2_b):
    b, s, d = x.shape
    t = b * s
    xf = x.reshape(t, d)
    xb = xf.astype(BF16)
    w_in_b = w_in.astype(BF16)
    row = lambda a: a.reshape(1, -1).astype(F32)

    off_rnn = 3 * DSWA_WIDTH
    off_mq = off_rnn + 2 * D_RNN
    off_gl = off_mq + MEM_WIDTH
    qkv = _matmul(xb, w_in_b, 0, 3 * DSWA_WIDTH, BF16, tn=512).reshape(b, s, 3 * DSWA_WIDTH)
    xr_yg = _matmul(xb, w_in_b, off_rnn, 2 * D_RNN, F32, tn=512).reshape(b, s, 2 * D_RNN)
    mq = _matmul(xb, w_in_b, off_mq, MEM_WIDTH, BF16, tn=512).reshape(b, s, MEM_WIDTH)
    gl = _matmul(xb, w_in_b, off_gl, 3 * D_MODEL, F32, tn=1024)

    os, lses = [], []
    for gi, dil in enumerate(DSWA_DILATIONS):
        o_g, lse_g = _dswa_group(qkv, gi, dil)
        os.append(o_g.reshape(t, GROUP_W))
        lses.append(lse_g.reshape(t, GROUP_W))

    per_tile = MXU_TILE // LRU_BW
    wa_bd = _block_diag(lru_wa, per_tile).astype(BF16)
    wx_bd = _block_diag(lru_wx, per_tile).astype(BF16)
    rec = _rglru(xr_yg, conv_w.astype(F32), row(conv_b), wa_bd, row(lru_ba), wx_bd, row(lru_bx),
                 row(lru_lambda)).reshape(t, D_RNN)

    mem_len = mem.shape[1]
    kv = _matmul(mem.reshape(b * mem_len, d).astype(BF16), w_mem_kv.astype(BF16), 0,
                 2 * MEM_WIDTH, BF16, tn=1024).reshape(b, mem_len, 2 * MEM_WIDTH)
    memo = _memattn(mq, kv).reshape(t, MEM_WIDTH)

    x1, x1b = _mix(os, lses, rec, memo, gl, xf, b_gate.astype(F32), w_br_attn.astype(BF16),
                   w_br_lru.astype(BF16), w_br_mem.astype(BF16), w_out.astype(BF16),
                   row(ln1_g), row(ln1_b))

    kt = jnp.swapaxes(peer_keys, -1, -2).reshape(PEER_HEADS * 2, PEER_KEY_DIM // 2, N_KEYS)
    kbd = _block_diag(kt, 2).astype(BF16)
    scores = _peer_scores(x1b, peer_wq.astype(BF16), kbd)

    sc = scores.reshape(t, PEER_HEADS, 2, N_KEYS)
    s_top, _ = lax.top_k(sc, PEER_TOPK)
    cand = s_top[:, :, 0, :, None] + s_top[:, :, 1, None, :]
    c_s, _ = lax.top_k(cand.reshape(t, PEER_HEADS, PEER_TOPK * PEER_TOPK), PEER_TOPK)
    tau = c_s[..., PEER_TOPK - 1]
    ma = s_top[:, :, 0, 0]
    mb = s_top[:, :, 1, 0]
    zsum = jnp.sum(jnp.exp(c_s - (ma + mb)[..., None]), axis=-1)
    sa = sc[:, :, 0, :]
    sb = sc[:, :, 1, :].reshape(t, PEER_HEADS * N_KEYS)
    ga = jnp.exp(sa - ma[..., None]) / zsum[..., None]

    te = 1024
    ib = te // N_KEYS

    def by_block(a):
        a = a.transpose(2, 0, 1).reshape(N_KEYS // ib, ib, t, PEER_HEADS)
        return a.transpose(0, 2, 1, 3).reshape(N_KEYS // ib, t, ib * PEER_HEADS)

    out = _peer_mix(x1b, x1, peer_u.astype(BF16), peer_v.astype(BF16), sb, by_block(sa), by_block(ga),
                    tau, mb, row(ln2_g), row(ln2_b), te=te)
    return out.reshape(b, s, d)


def kernel(x, mem, w_in, b_gate, conv_w, conv_b, lru_wa, lru_ba, lru_wx, lru_bx, lru_lambda, w_mem_kv, w_br_attn, w_br_lru, w_br_mem, w_out, ln1_g, ln1_b, peer_wq, peer_keys, peer_u, peer_v, ln2_g, ln2_b):
    h = x.astype(F32)
    depth = w_in.shape[0]
    for l in range(depth):
        h = _layer(h, mem, w_in[l], b_gate[l], conv_w[l], conv_b[l], lru_wa[l], lru_ba[l],
                   lru_wx[l], lru_bx[l], lru_lambda[l], w_mem_kv[l], w_br_attn[l], w_br_lru[l],
                   w_br_mem[l], w_out[l], ln1_g[l], ln1_b[l], peer_wq[l], peer_keys[l],
                   peer_u[l], peer_v[l], ln2_g[l], ln2_b[l])
    return h.astype(x.dtype)
```

```python
import functools
import math

import jax
import jax.numpy as jnp
from jax import lax
from jax.experimental import pallas as pl
from jax.experimental.pallas import tpu as pltpu

F32 = jnp.float32
BF16 = jnp.bfloat16

D_MODEL = 1024
N_GROUPS = 3
DSWA_DILATIONS = (1, 4, 16)
DSWA_HEADS = 4
HEAD_DIM = 128
DSWA_BLK = 128
GROUP_W = DSWA_HEADS * HEAD_DIM
DSWA_WIDTH = N_GROUPS * GROUP_W
D_RNN = 1024
LRU_BLOCKS = 16
LRU_BW = D_RNN // LRU_BLOCKS
CONV_W = 4
LRU_C = 8.0
MEM_HEADS = 4
MEM_WIDTH = MEM_HEADS * HEAD_DIM
PEER_HEADS = 8
PEER_KEY_DIM = 256
N_KEYS = 128
N_EXPERTS = N_KEYS * N_KEYS
PEER_TOPK = 16
ALPHA = 2.0 ** 0.25
LN_EPS = 1e-5
NEG_INF = -1e30

MXU_TILE = 256
VMEM_LIMIT = 56 * 1024 * 1024


def _cparams(sem):
    return pltpu.CompilerParams(dimension_semantics=sem, vmem_limit_bytes=VMEM_LIMIT)


def _dot(a, b):
    return jnp.dot(a, b, preferred_element_type=F32)


def _dot_nt(a, b):
    return lax.dot_general(a, b, (((1,), (1,)), ((), ())), preferred_element_type=F32)


def _gelu_erf(x):
    return 0.5 * x * (1.0 + lax.erf(x * (1.0 / math.sqrt(2.0))))


def _layernorm(h, g, b):
    mu = jnp.mean(h, axis=-1, keepdims=True)
    c = h - mu
    var = jnp.mean(c * c, axis=-1, keepdims=True)
    return c * lax.rsqrt(var + LN_EPS) * g + b


def _matmul_kernel(x_ref, w_ref, o_ref):
    o_ref[...] = _dot(x_ref[...], w_ref[...]).astype(o_ref.dtype)


def _matmul(x, w, col_off, n_cols, out_dtype, tm=1024, tn=1024):
    m, k = x.shape
    tn = min(tn, n_cols)
    tm = min(tm, m)
    assert m % tm == 0 and n_cols % tn == 0 and col_off % tn == 0
    off = col_off // tn
    return pl.pallas_call(
        _matmul_kernel,
        grid=(m // tm, n_cols // tn),
        in_specs=[pl.BlockSpec((tm, k), lambda i, j: (i, 0)),
                  pl.BlockSpec((k, tn), lambda i, j: (0, off + j))],
        out_specs=pl.BlockSpec((tm, tn), lambda i, j: (i, j)),
        out_shape=jax.ShapeDtypeStruct((m, n_cols), out_dtype),
        compiler_params=_cparams(("parallel", "arbitrary")),
        name="matmul",
    )(x, w)


def _dswa_kernel(q_ref, kc_ref, kp_ref, vc_ref, vp_ref, o_ref, lse_ref, *, nq):
    row = lax.broadcasted_iota(jnp.int32, (DSWA_BLK, DSWA_BLK), 0)
    col = lax.broadcasted_iota(jnp.int32, (DSWA_BLK, DSWA_BLK), 1)
    mask_cur = col <= row
    mask_band = col >= row
    mask_first = col >= row + jnp.where(pl.program_id(2) == 0, DSWA_BLK, 0)
    scale = 1.0 / math.sqrt(HEAD_DIM)
    for j in range(nq):
        rows = slice(j * DSWA_BLK, (j + 1) * DSWA_BLK)
        q = q_ref[0, rows, :]
        kcur = kc_ref[0, rows, :]
        vcur = vc_ref[0, rows, :]
        if j == 0:
            kprev = kp_ref[0]
            vprev = vp_ref[0]
            mask_prev = mask_first
        else:
            prows = slice((j - 1) * DSWA_BLK, j * DSWA_BLK)
            kprev = kc_ref[0, prows, :]
            vprev = vc_ref[0, prows, :]
            mask_prev = mask_band
        for h in range(DSWA_HEADS):
            hs = slice(h * HEAD_DIM, (h + 1) * HEAD_DIM)
            qh = q[:, hs]
            s_p = jnp.where(mask_prev, _dot_nt(qh, kprev[:, hs]) * scale, NEG_INF)
            s_c = jnp.where(mask_cur, _dot_nt(qh, kcur[:, hs]) * scale, NEG_INF)
            m = jnp.maximum(jnp.max(s_p, axis=-1, keepdims=True),
                            jnp.max(s_c, axis=-1, keepdims=True))
            p_p = jnp.exp(s_p - m)
            p_c = jnp.exp(s_c - m)
            l = jnp.sum(p_p, axis=-1, keepdims=True) + jnp.sum(p_c, axis=-1, keepdims=True)
            o = _dot(p_p.astype(BF16), vprev[:, hs]) + _dot(p_c.astype(BF16), vcur[:, hs])
            o_ref[0, rows, hs] = (o / l).astype(o_ref.dtype)
            lse_ref[0, rows, hs] = jnp.broadcast_to(m + jnp.log(l), (DSWA_BLK, HEAD_DIM))


def _dswa_group(qkv, group, dilation):
    b, s, w = qkv.shape
    m_len = s // dilation
    assert s % (dilation * DSWA_BLK) == 0
    nb = m_len // DSWA_BLK
    nq = min(8, nb)
    n_steps = nb // nq
    nblk = w // GROUP_W
    view = qkv.reshape(b, m_len, dilation * w)
    qc, kc, vc = group, N_GROUPS + group, 2 * N_GROUPS + group

    def cur(c):
        return pl.BlockSpec((1, nq * DSWA_BLK, GROUP_W), lambda bi, r, n: (bi, n, r * nblk + c))

    def prev(c):
        return pl.BlockSpec((1, DSWA_BLK, GROUP_W),
                            lambda bi, r, n: (bi, jnp.maximum(n * nq - 1, 0), r * nblk + c))

    out_spec = pl.BlockSpec((1, nq * DSWA_BLK, GROUP_W), lambda bi, r, n: (bi, n, r))
    o, lse = pl.pallas_call(
        functools.partial(_dswa_kernel, nq=nq),
        grid=(b, dilation, n_steps),
        in_specs=[cur(qc), cur(kc), prev(kc), cur(vc), prev(vc)],
        out_specs=[out_spec, out_spec],
        out_shape=[jax.ShapeDtypeStruct((b, m_len, dilation * GROUP_W), BF16),
                   jax.ShapeDtypeStruct((b, m_len, dilation * GROUP_W), F32)],
        compiler_params=_cparams(("parallel", "parallel", "arbitrary")),
        name=f"dswa_g{group}",
    )(view, view, view, view, view)
    return o.reshape(b, s, GROUP_W), lse.reshape(b, s, GROUP_W)


def _shift_rows(x, k, fill):
    rolled = pltpu.roll(x, k, 0)
    row = lax.broadcasted_iota(jnp.int32, x.shape, 0)
    return jnp.where(row >= k, rolled, fill)


def _rglru_kernel(xr_ref, halo_ref, yg_ref, cw_ref, cb_ref, wa_ref, ba_ref, wx_ref, bx_ref,
                  lam_ref, o_ref, h_ref, *, ts):
    si = pl.program_id(1)

    @pl.when(si == 0)
    def _():
        h_ref[...] = jnp.zeros_like(h_ref)

    x = xr_ref[0]
    halo = jnp.where(si > 0, halo_ref[0], 0.0)
    xfull = jnp.concatenate([halo, x], axis=0)
    xc = cb_ref[...] + cw_ref[CONV_W - 1:CONV_W, :] * x
    for k in range(1, CONV_W):
        xc = xc + cw_ref[CONV_W - 1 - k:CONV_W - k, :] * pltpu.roll(xfull, k, 0)[8:]

    xb = xc.astype(BF16)
    nblk = D_RNN // MXU_TILE
    ra = jnp.concatenate(
        [_dot(xb[:, c * MXU_TILE:(c + 1) * MXU_TILE], wa_ref[c]) for c in range(nblk)], axis=1)
    rx = jnp.concatenate(
        [_dot(xb[:, c * MXU_TILE:(c + 1) * MXU_TILE], wx_ref[c]) for c in range(nblk)], axis=1)
    r = jax.nn.sigmoid(ra + ba_ref[...])
    gate_i = jax.nn.sigmoid(rx + bx_ref[...])
    neg_lam = -lam_ref[...]
    softplus = jnp.maximum(neg_lam, 0.0) + jnp.log1p(jnp.exp(-jnp.abs(neg_lam)))
    log_a = (-LRU_C) * r * softplus
    a = jnp.exp(log_a)
    bterm = jnp.sqrt(1.0 - a * a) * gate_i * xc

    step = 1
    while step < ts:
        a_s = _shift_rows(a, step, 1.0)
        b_s = _shift_rows(bterm, step, 0.0)
        bterm = a * b_s + bterm
        a = a * a_s
        step *= 2
    h = a * h_ref[0:1, :] + bterm
    h_ref[0:1, :] = h[ts - 1:ts, :]
    o_ref[0] = (h * jax.nn.gelu(yg_ref[0], approximate=True)).astype(o_ref.dtype)


def _rglru(xr_yg, conv_w, conv_b, wa_bd, ba, wx_bd, bx, lam, ts=256):
    b, s, _ = xr_yg.shape
    assert s % ts == 0
    full = lambda shape: pl.BlockSpec(shape, lambda bi, si: (0,) * len(shape))
    return pl.pallas_call(
        functools.partial(_rglru_kernel, ts=ts),
        grid=(b, s // ts),
        in_specs=[pl.BlockSpec((1, ts, D_RNN), lambda bi, si: (bi, si, 0)),
                  pl.BlockSpec((1, 8, D_RNN),
                               lambda bi, si: (bi, jnp.maximum(si * (ts // 8) - 1, 0), 0)),
                  pl.BlockSpec((1, ts, D_RNN), lambda bi, si: (bi, si, 1)),
                  full((CONV_W, D_RNN)), full((1, D_RNN)),
                  full(wa_bd.shape), full((1, D_RNN)),
                  full(wx_bd.shape), full((1, D_RNN)), full((1, D_RNN))],
        out_specs=pl.BlockSpec((1, ts, D_RNN), lambda bi, si: (bi, si, 0)),
        out_shape=jax.ShapeDtypeStruct((b, s, D_RNN), BF16),
        scratch_shapes=[pltpu.VMEM((8, D_RNN), F32)],
        compiler_params=_cparams(("parallel", "arbitrary")),
        name="rglru",
    )(xr_yg, xr_yg, xr_yg, conv_w, conv_b, wa_bd, ba, wx_bd, bx, lam)


def _memattn_kernel(q_ref, kv_ref, o_ref):
    scale = 1.0 / math.sqrt(HEAD_DIM)
    for h in range(MEM_HEADS):
        hs = slice(h * HEAD_DIM, (h + 1) * HEAD_DIM)
        vs = slice(MEM_WIDTH + h * HEAD_DIM, MEM_WIDTH + (h + 1) * HEAD_DIM)
        s = _dot_nt(q_ref[0, :, hs], kv_ref[0, :, hs]) * scale
        m = jnp.max(s, axis=-1, keepdims=True)
        p = jnp.exp(s - m)
        l = jnp.sum(p, axis=-1, keepdims=True)
        o_ref[0, :, hs] = (_dot(p.astype(BF16), kv_ref[0, :, vs]) / l).astype(o_ref.dtype)


def _memattn(mq, kv, ts=1024):
    b, s, _ = mq.shape
    mem_len = kv.shape[1]
    return pl.pallas_call(
        _memattn_kernel,
        grid=(b, s // ts),
        in_specs=[pl.BlockSpec((1, ts, MEM_WIDTH), lambda bi, si: (bi, si, 0)),
                  pl.BlockSpec((1, mem_len, 2 * MEM_WIDTH), lambda bi, si: (bi, 0, 0))],
        out_specs=pl.BlockSpec((1, ts, MEM_WIDTH), lambda bi, si: (bi, si, 0)),
        out_shape=jax.ShapeDtypeStruct((b, s, MEM_WIDTH), BF16),
        compiler_params=_cparams(("parallel", "arbitrary")),
        name="memattn",
    )(mq, kv)


def _mix_kernel(o0_ref, o1_ref, o2_ref, l0_ref, l1_ref, l2_ref, rec_ref, memo_ref, gl_ref, x_ref,
                bg_ref, wa_ref, wl_ref, wm_ref, wo_ref, g_ref, b_ref, x1_ref, x1b_ref):
    l0, l1, l2 = l0_ref[...], l1_ref[...], l2_ref[...]
    mx = jnp.maximum(jnp.maximum(l0, l1), l2)
    e0, e1, e2 = jnp.exp(l0 - mx), jnp.exp(l1 - mx), jnp.exp(l2 - mx)
    attn = (e0 * o0_ref[...].astype(F32) + e1 * o1_ref[...].astype(F32)
            + e2 * o2_ref[...].astype(F32)) / (e0 + e1 + e2)
    d = D_MODEL
    gate = lambda j: jax.nn.sigmoid(gl_ref[:, j * d:(j + 1) * d] + bg_ref[j:j + 1, :])
    merged = (gate(0) * _dot(attn.astype(BF16), wa_ref[...])
              + gate(1) * _dot(rec_ref[...], wl_ref[...])
              + gate(2) * _dot(memo_ref[...], wm_ref[...]))
    mix = _dot(merged.astype(BF16), wo_ref[...])
    x1 = _layernorm(ALPHA * x_ref[...] + mix, g_ref[...], b_ref[...])
    x1_ref[...] = x1
    x1b_ref[...] = x1.astype(BF16)


def _mix(os, lses, rec, memo, gl, x, b_gate, wa, wl, wm, wo, g, bta, tt=512):
    t = x.shape[0]
    rows = lambda w: pl.BlockSpec((tt, w), lambda i: (i, 0))
    full = lambda a: pl.BlockSpec(a.shape, lambda i: (0,) * a.ndim)
    return pl.pallas_call(
        _mix_kernel,
        grid=(t // tt,),
        in_specs=[rows(GROUP_W)] * 6 + [rows(D_RNN), rows(MEM_WIDTH), rows(3 * D_MODEL), rows(D_MODEL),
                                        full(b_gate), full(wa), full(wl), full(wm), full(wo),
                                        full(g), full(bta)],
        out_specs=[rows(D_MODEL), rows(D_MODEL)],
        out_shape=[jax.ShapeDtypeStruct((t, D_MODEL), F32), jax.ShapeDtypeStruct((t, D_MODEL), BF16)],
        compiler_params=_cparams(("parallel",)),
        name="mix_ln1",
    )(*os, *lses, rec, memo, gl, x, b_gate, wa, wl, wm, wo, g, bta)


def _peer_scores_kernel(x_ref, wq_ref, kbd_ref, o_ref):
    q = _dot(x_ref[...], wq_ref[...]).astype(BF16)
    for h in range(PEER_HEADS):
        hs = slice(h * PEER_KEY_DIM, (h + 1) * PEER_KEY_DIM)
        o_ref[hs, :] = _dot_nt(kbd_ref[h], q[:, hs])


def _peer_scores(x1b, wq, kbd, tt=512):
    t = x1b.shape[0]
    w = PEER_HEADS * PEER_KEY_DIM
    return pl.pallas_call(
        _peer_scores_kernel,
        grid=(t // tt,),
        in_specs=[pl.BlockSpec((tt, D_MODEL), lambda i: (i, 0)),
                  pl.BlockSpec(wq.shape, lambda i: (0, 0)),
                  pl.BlockSpec(kbd.shape, lambda i: (0, 0, 0))],
        out_specs=pl.BlockSpec((w, tt), lambda i: (0, i)),
        out_shape=jax.ShapeDtypeStruct((w, t), F32),
        compiler_params=_cparams(("parallel",)),
        name="peer_scores",
    )(x1b, wq, kbd)


_PEER_CANDS = tuple((p, q) for p in range(PEER_TOPK) for q in range(PEER_TOPK)
                    if (p + 1) * (q + 1) <= PEER_TOPK)


def _top_positions(s):
    key = lax.broadcasted_iota(jnp.int32, s.shape, 0).astype(F32)
    pos = jnp.full(s.shape, float(PEER_TOPK), F32)
    work = s
    tops = []
    for p in range(PEER_TOPK):
        mx = jnp.max(work, axis=0, keepdims=True)
        first = jnp.min(jnp.where(work == mx, key, float(N_KEYS)), axis=0, keepdims=True)
        hit = key == first
        pos = jnp.where(hit, float(p), pos)
        work = jnp.where(hit, -jnp.inf, work)
        tops.append(mx)
    return pos, tops


def _peer_select_kernel(sc_ref, ra_ref, ea_ref, cb_ref, eb_ref):
    tt = sc_ref.shape[1]
    pos_b, tops_a, tops_b = [], [], []
    for h in range(PEER_HEADS):
        base = h * PEER_KEY_DIM
        pa, ta = _top_positions(sc_ref[base:base + N_KEYS, :])
        pb, tb = _top_positions(sc_ref[base + N_KEYS:base + 2 * N_KEYS, :])
        ra_ref[h * N_KEYS:(h + 1) * N_KEYS, :] = pa
        pos_b.append(pb)
        tops_a.append(ta)
        tops_b.append(tb)
    a_top = [jnp.concatenate([tops_a[h][p] for h in range(PEER_HEADS)], axis=0)
             for p in range(PEER_TOPK)]
    b_top = [jnp.concatenate([tops_b[h][q] for h in range(PEER_HEADS)], axis=0)
             for q in range(PEER_TOPK)]
    sums = [a_top[p] + b_top[q] for p, q in _PEER_CANDS]
    n = len(sums)
    rank = [jnp.zeros((PEER_HEADS, tt), F32) for _ in range(n)]
    for c in range(n):
        for c2 in range(c):
            ge = sums[c2] >= sums[c]
            rank[c] = rank[c] + jnp.where(ge, 1.0, 0.0)
            rank[c2] = rank[c2] + jnp.where(ge, 0.0, 1.0)
    cnt = [jnp.zeros((PEER_HEADS, tt), F32) for _ in range(PEER_TOPK)]
    zsum = jnp.zeros((PEER_HEADS, tt), F32)
    for c, (p, q) in enumerate(_PEER_CANDS):
        sel = rank[c] < float(PEER_TOPK)
        cnt[q] = cnt[q] + jnp.where(sel, 1.0, 0.0)
        zsum = zsum + jnp.where(sel, jnp.exp(sums[c] - sums[0]), 0.0)
    inv_z = 1.0 / zsum
    for h in range(PEER_HEADS):
        base = h * PEER_KEY_DIM
        rows = slice(h * N_KEYS, (h + 1) * N_KEYS)
        cb = jnp.zeros((N_KEYS, tt), F32)
        for q in range(PEER_TOPK):
            cb = cb + jnp.where(pos_b[h] == float(q), cnt[q][h:h + 1, :], 0.0)
        cb_ref[rows, :] = cb.astype(cb_ref.dtype)
        ea_ref[rows, :] = (jnp.exp(sc_ref[base:base + N_KEYS, :] - a_top[0][h:h + 1, :])
                           * inv_z[h:h + 1, :])
        eb_ref[rows, :] = jnp.exp(sc_ref[base + N_KEYS:base + 2 * N_KEYS, :]
                                  - b_top[0][h:h + 1, :]).astype(eb_ref.dtype)


def _peer_select(sc, tt=256):
    w, t = sc.shape
    rows = PEER_HEADS * N_KEYS
    out_spec = pl.BlockSpec((rows, tt), lambda i: (0, i))
    return pl.pallas_call(
        _peer_select_kernel,
        grid=(t // tt,),
        in_specs=[pl.BlockSpec((w, tt), lambda i: (0, i))],
        out_specs=[out_spec] * 4,
        out_shape=[jax.ShapeDtypeStruct((rows, t), F32), jax.ShapeDtypeStruct((rows, t), F32),
                   jax.ShapeDtypeStruct((rows, t), BF16), jax.ShapeDtypeStruct((rows, t), BF16)],
        compiler_params=_cparams(("parallel",)),
        name="peer_select",
    )(sc)


def _peer_mix_kernel(x1b_ref, x1_ref, u_ref, vt_ref, ra_ref, ea_ref, cb_ref, eb_ref,
                     g_ref, b_ref, o_ref, acc_ref, h_ref, y_ref, *, ib):
    e = pl.program_id(1)
    tt = x1b_ref.shape[0]

    @pl.when(e == 0)
    def _():
        acc_ref[...] = jnp.zeros_like(acc_ref)

    h_ref[...] = _dot_nt(u_ref[...], x1b_ref[...])

    def i_block(ii, carry):
        rows = pl.ds(pl.multiple_of(ii * N_KEYS, N_KEYS), N_KEYS)
        w = jnp.zeros((N_KEYS, tt), BF16)
        for h in range(PEER_HEADS):
            hs = slice(h * N_KEYS, (h + 1) * N_KEYS)
            ra = jnp.broadcast_to(ra_ref[h, pl.ds(ii, 1), :], (N_KEYS, tt)).astype(BF16)
            ea = jnp.broadcast_to(ea_ref[h, pl.ds(ii, 1), :], (N_KEYS, tt)).astype(BF16)
            w = w + jnp.where(cb_ref[hs, :] > ra, eb_ref[hs, :] * ea, jnp.zeros_like(w))
        y_ref[rows, :] = w * _gelu_erf(h_ref[rows, :]).astype(BF16)
        return carry

    lax.fori_loop(0, ib, i_block, 0)
    acc_ref[...] += _dot(vt_ref[...], y_ref[...])

    @pl.when(e == pl.num_programs(1) - 1)
    def _():
        o_ref[...] = _layernorm(ALPHA * x1_ref[...] + acc_ref[...].T, g_ref[...], b_ref[...])


def _peer_mix(x1b, x1, u, vt, ra, ea, cb, eb, g, bta, tt=512, te=2048):
    t = x1.shape[0]
    ib = te // N_KEYS
    ne = N_EXPERTS // te
    rows = PEER_HEADS * N_KEYS
    return pl.pallas_call(
        functools.partial(_peer_mix_kernel, ib=ib),
        grid=(t // tt, ne),
        in_specs=[pl.BlockSpec((tt, D_MODEL), lambda i, e: (i, 0)),
                  pl.BlockSpec((tt, D_MODEL), lambda i, e: (i, 0)),
                  pl.BlockSpec((te, D_MODEL), lambda i, e: (e, 0)),
                  pl.BlockSpec((D_MODEL, te), lambda i, e: (0, e)),
                  pl.BlockSpec((PEER_HEADS, ib, tt), lambda i, e: (0, e, i)),
                  pl.BlockSpec((PEER_HEADS, ib, tt), lambda i, e: (0, e, i)),
                  pl.BlockSpec((rows, tt), lambda i, e: (0, i)),
                  pl.BlockSpec((rows, tt), lambda i, e: (0, i)),
                  pl.BlockSpec((1, D_MODEL), lambda i, e: (0, 0)),
                  pl.BlockSpec((1, D_MODEL), lambda i, e: (0, 0))],
        out_specs=pl.BlockSpec((tt, D_MODEL), lambda i, e: (i, 0)),
        out_shape=jax.ShapeDtypeStruct((t, D_MODEL), F32),
        scratch_shapes=[pltpu.VMEM((D_MODEL, tt), F32),
                        pltpu.VMEM((te, tt), F32),
                        pltpu.VMEM((te, tt), BF16)],
        compiler_params=_cparams(("parallel", "arbitrary")),
        name="peer_mix_ln2",
    )(x1b, x1, u, vt, ra, ea, cb, eb, g, bta)


def _block_diag(w, per_tile):
    n, k, _ = w.shape
    w = w.reshape(n // per_tile, per_tile, k, k)
    eye = jnp.eye(per_tile, dtype=w.dtype)
    out = jnp.einsum('tpij,pq->tpiqj', w, eye)
    return out.reshape(n // per_tile, per_tile * k, per_tile * k)


def _layer(x, mem, w_in, b_gate, conv_w, conv_b, lru_wa, lru_ba, lru_wx, lru_bx, lru_lambda,
           w_mem_kv, w_br_attn, w_br_lru, w_br_mem, w_out, ln1_g, ln1_b,
           peer_wq, peer_keys, peer_u, peer_v, ln2_g, ln2_b):
    b, s, d = x.shape
    t = b * s
    xf = x.reshape(t, d)
    xb = xf.astype(BF16)
    w_in_b = w_in.astype(BF16)
    row = lambda a: a.reshape(1, -1).astype(F32)

    off_rnn = 3 * DSWA_WIDTH
    off_mq = off_rnn + 2 * D_RNN
    off_gl = off_mq + MEM_WIDTH
    qkv = _matmul(xb, w_in_b, 0, 3 * DSWA_WIDTH, BF16, tn=512).reshape(b, s, 3 * DSWA_WIDTH)
    xr_yg = _matmul(xb, w_in_b, off_rnn, 2 * D_RNN, F32, tn=512).reshape(b, s, 2 * D_RNN)
    mq = _matmul(xb, w_in_b, off_mq, MEM_WIDTH, BF16, tn=512).reshape(b, s, MEM_WIDTH)
    gl = _matmul(xb, w_in_b, off_gl, 3 * D_MODEL, F32, tn=1024)

    os, lses = [], []
    for gi, dil in enumerate(DSWA_DILATIONS):
        o_g, lse_g = _dswa_group(qkv, gi, dil)
        os.append(o_g.reshape(t, GROUP_W))
        lses.append(lse_g.reshape(t, GROUP_W))

    per_tile = MXU_TILE // LRU_BW
    wa_bd = _block_diag(lru_wa, per_tile).astype(BF16)
    wx_bd = _block_diag(lru_wx, per_tile).astype(BF16)
    rec = _rglru(xr_yg, conv_w.astype(F32), row(conv_b), wa_bd, row(lru_ba), wx_bd, row(lru_bx),
                 row(lru_lambda)).reshape(t, D_RNN)

    mem_len = mem.shape[1]
    kv = _matmul(mem.reshape(b * mem_len, d).astype(BF16), w_mem_kv.astype(BF16), 0,
                 2 * MEM_WIDTH, BF16, tn=1024).reshape(b, mem_len, 2 * MEM_WIDTH)
    memo = _memattn(mq, kv).reshape(t, MEM_WIDTH)

    x1, x1b = _mix(os, lses, rec, memo, gl, xf, b_gate.astype(F32), w_br_attn.astype(BF16),
                   w_br_lru.astype(BF16), w_br_mem.astype(BF16), w_out.astype(BF16),
                   row(ln1_g), row(ln1_b))

    kbd = _block_diag(peer_keys.reshape(PEER_HEADS * 2, N_KEYS, PEER_KEY_DIM // 2), 2).astype(BF16)
    scores = _peer_scores(x1b, peer_wq.astype(BF16), kbd)

    ra, ea, cb, eb = _peer_select(scores)
    ra = ra.reshape(PEER_HEADS, N_KEYS, t)
    ea = ea.reshape(PEER_HEADS, N_KEYS, t)

    out = _peer_mix(x1b, x1, peer_u.astype(BF16), peer_v.T.astype(BF16), ra, ea, cb, eb,
                    row(ln2_g), row(ln2_b))
    return out.reshape(b, s, d)


def kernel(x, mem, w_in, b_gate, conv_w, conv_b, lru_wa, lru_ba, lru_wx, lru_bx, lru_lambda, w_mem_kv, w_br_attn, w_br_lru, w_br_mem, w_out, ln1_g, ln1_b, peer_wq, peer_keys, peer_u, peer_v, ln2_g, ln2_b):
    h = x.astype(F32)
    depth = w_in.shape[0]
    for l in range(depth):
        h = _layer(h, mem, w_in[l], b_gate[l], conv_w[l], conv_b[l], lru_wa[l], lru_ba[l],
                   lru_wx[l], lru_bx[l], lru_lambda[l], w_mem_kv[l], w_br_attn[l], w_br_lru[l],
                   w_br_mem[l], w_out[l], ln1_g[l], ln1_b[l], peer_wq[l], peer_keys[l],
                   peer_u[l], peer_v[l], ln2_g[l], ln2_b[l])
    return h.astype(x.dtype)
```

```python
import functools
import math

import jax
import jax.numpy as jnp
from jax import lax
from jax.experimental import pallas as pl
from jax.experimental.pallas import tpu as pltpu

F32 = jnp.float32
BF16 = jnp.bfloat16

D_MODEL = 1024
N_GROUPS = 3
DSWA_DILATIONS = (1, 4, 16)
DSWA_HEADS = 4
HEAD_DIM = 128
DSWA_BLK = 128
GROUP_W = DSWA_HEADS * HEAD_DIM
DSWA_WIDTH = N_GROUPS * GROUP_W
D_RNN = 1024
LRU_BLOCKS = 16
LRU_BW = D_RNN // LRU_BLOCKS
CONV_W = 4
LRU_C = 8.0
MEM_HEADS = 4
MEM_WIDTH = MEM_HEADS * HEAD_DIM
PEER_HEADS = 8
PEER_KEY_DIM = 256
N_KEYS = 128
N_EXPERTS = N_KEYS * N_KEYS
PEER_TOPK = 16
PEER_CHUNK_KEYS = 4
BF16_ROWS = 16
ALPHA = 2.0 ** 0.25
LN_EPS = 1e-5
NEG_INF = -1e30

MXU_TILE = 256
VMEM_LIMIT = 56 * 1024 * 1024


def _cparams(sem):
    return pltpu.CompilerParams(dimension_semantics=sem, vmem_limit_bytes=VMEM_LIMIT)


def _dot(a, b):
    return jnp.dot(a, b, preferred_element_type=F32)


def _dot_nt(a, b):
    return lax.dot_general(a, b, (((1,), (1,)), ((), ())), preferred_element_type=F32)


def _gelu_erf(x):
    return 0.5 * x * (1.0 + lax.erf(x * (1.0 / math.sqrt(2.0))))


def _layernorm(h, g, b):
    mu = jnp.mean(h, axis=-1, keepdims=True)
    c = h - mu
    var = jnp.mean(c * c, axis=-1, keepdims=True)
    return c * lax.rsqrt(var + LN_EPS) * g + b


def _matmul_kernel(x_ref, w_ref, o_ref):
    o_ref[...] = _dot(x_ref[...], w_ref[...]).astype(o_ref.dtype)


def _matmul(x, w, col_off, n_cols, out_dtype, tm=1024, tn=1024):
    m, k = x.shape
    tn = min(tn, n_cols)
    tm = min(tm, m)
    assert m % tm == 0 and n_cols % tn == 0 and col_off % tn == 0
    off = col_off // tn
    return pl.pallas_call(
        _matmul_kernel,
        grid=(m // tm, n_cols // tn),
        in_specs=[pl.BlockSpec((tm, k), lambda i, j: (i, 0)),
                  pl.BlockSpec((k, tn), lambda i, j: (0, off + j))],
        out_specs=pl.BlockSpec((tm, tn), lambda i, j: (i, j)),
        out_shape=jax.ShapeDtypeStruct((m, n_cols), out_dtype),
        compiler_params=_cparams(("parallel", "arbitrary")),
        name="matmul",
    )(x, w)


def _dswa_kernel(q_ref, kc_ref, kp_ref, vc_ref, vp_ref, o_ref, lse_ref, *, nq):
    row = lax.broadcasted_iota(jnp.int32, (DSWA_BLK, DSWA_BLK), 0)
    col = lax.broadcasted_iota(jnp.int32, (DSWA_BLK, DSWA_BLK), 1)
    mask_cur = col <= row
    mask_band = col >= row
    mask_first = col >= row + jnp.where(pl.program_id(2) == 0, DSWA_BLK, 0)
    scale = 1.0 / math.sqrt(HEAD_DIM)
    for j in range(nq):
        rows = slice(j * DSWA_BLK, (j + 1) * DSWA_BLK)
        q = q_ref[0, rows, :]
        kcur = kc_ref[0, rows, :]
        vcur = vc_ref[0, rows, :]
        if j == 0:
            kprev = kp_ref[0]
            vprev = vp_ref[0]
            mask_prev = mask_first
        else:
            prows = slice((j - 1) * DSWA_BLK, j * DSWA_BLK)
            kprev = kc_ref[0, prows, :]
            vprev = vc_ref[0, prows, :]
            mask_prev = mask_band
        for h in range(DSWA_HEADS):
            hs = slice(h * HEAD_DIM, (h + 1) * HEAD_DIM)
            qh = q[:, hs]
            s_p = jnp.where(mask_prev, _dot_nt(qh, kprev[:, hs]) * scale, NEG_INF)
            s_c = jnp.where(mask_cur, _dot_nt(qh, kcur[:, hs]) * scale, NEG_INF)
            m = jnp.maximum(jnp.max(s_p, axis=-1, keepdims=True),
                            jnp.max(s_c, axis=-1, keepdims=True))
            p_p = jnp.exp(s_p - m)
            p_c = jnp.exp(s_c - m)
            l = jnp.sum(p_p, axis=-1, keepdims=True) + jnp.sum(p_c, axis=-1, keepdims=True)
            o = _dot(p_p.astype(BF16), vprev[:, hs]) + _dot(p_c.astype(BF16), vcur[:, hs])
            o_ref[0, rows, hs] = (o / l).astype(o_ref.dtype)
            lse_ref[0, rows, hs] = jnp.broadcast_to(m + jnp.log(l), (DSWA_BLK, HEAD_DIM))


def _dswa_group(qkv, group, dilation):
    b, s, w = qkv.shape
    m_len = s // dilation
    assert s % (dilation * DSWA_BLK) == 0
    nb = m_len // DSWA_BLK
    nq = min(8, nb)
    n_steps = nb // nq
    nblk = w // GROUP_W
    view = qkv.reshape(b, m_len, dilation * w)
    qc, kc, vc = group, N_GROUPS + group, 2 * N_GROUPS + group

    def cur(c):
        return pl.BlockSpec((1, nq * DSWA_BLK, GROUP_W), lambda bi, r, n: (bi, n, r * nblk + c))

    def prev(c):
        return pl.BlockSpec((1, DSWA_BLK, GROUP_W),
                            lambda bi, r, n: (bi, jnp.maximum(n * nq - 1, 0), r * nblk + c))

    out_spec = pl.BlockSpec((1, nq * DSWA_BLK, GROUP_W), lambda bi, r, n: (bi, n, r))
    o, lse = pl.pallas_call(
        functools.partial(_dswa_kernel, nq=nq),
        grid=(b, dilation, n_steps),
        in_specs=[cur(qc), cur(kc), prev(kc), cur(vc), prev(vc)],
        out_specs=[out_spec, out_spec],
        out_shape=[jax.ShapeDtypeStruct((b, m_len, dilation * GROUP_W), BF16),
                   jax.ShapeDtypeStruct((b, m_len, dilation * GROUP_W), F32)],
        compiler_params=_cparams(("parallel", "parallel", "arbitrary")),
        name=f"dswa_g{group}",
    )(view, view, view, view, view)
    return o.reshape(b, s, GROUP_W), lse.reshape(b, s, GROUP_W)


def _shift_rows(x, k, fill):
    rolled = pltpu.roll(x, k, 0)
    row = lax.broadcasted_iota(jnp.int32, x.shape, 0)
    return jnp.where(row >= k, rolled, fill)


def _rglru_kernel(xr_ref, halo_ref, yg_ref, cw_ref, cb_ref, wa_ref, ba_ref, wx_ref, bx_ref,
                  lam_ref, o_ref, h_ref, *, ts):
    si = pl.program_id(1)

    @pl.when(si == 0)
    def _():
        h_ref[...] = jnp.zeros_like(h_ref)

    x = xr_ref[0]
    halo = jnp.where(si > 0, halo_ref[0], 0.0)
    xfull = jnp.concatenate([halo, x], axis=0)
    xc = cb_ref[...] + cw_ref[CONV_W - 1:CONV_W, :] * x
    for k in range(1, CONV_W):
        xc = xc + cw_ref[CONV_W - 1 - k:CONV_W - k, :] * pltpu.roll(xfull, k, 0)[8:]

    xb = xc.astype(BF16)
    nblk = D_RNN // MXU_TILE
    ra = jnp.concatenate(
        [_dot(xb[:, c * MXU_TILE:(c + 1) * MXU_TILE], wa_ref[c]) for c in range(nblk)], axis=1)
    rx = jnp.concatenate(
        [_dot(xb[:, c * MXU_TILE:(c + 1) * MXU_TILE], wx_ref[c]) for c in range(nblk)], axis=1)
    r = jax.nn.sigmoid(ra + ba_ref[...])
    gate_i = jax.nn.sigmoid(rx + bx_ref[...])
    neg_lam = -lam_ref[...]
    softplus = jnp.maximum(neg_lam, 0.0) + jnp.log1p(jnp.exp(-jnp.abs(neg_lam)))
    log_a = (-LRU_C) * r * softplus
    a = jnp.exp(log_a)
    bterm = jnp.sqrt(1.0 - a * a) * gate_i * xc

    step = 1
    while step < ts:
        a_s = _shift_rows(a, step, 1.0)
        b_s = _shift_rows(bterm, step, 0.0)
        bterm = a * b_s + bterm
        a = a * a_s
        step *= 2
    h = a * h_ref[0:1, :] + bterm
    h_ref[0:1, :] = h[ts - 1:ts, :]
    o_ref[0] = (h * jax.nn.gelu(yg_ref[0], approximate=True)).astype(o_ref.dtype)


def _rglru(xr_yg, conv_w, conv_b, wa_bd, ba, wx_bd, bx, lam, ts=256):
    b, s, _ = xr_yg.shape
    assert s % ts == 0
    full = lambda shape: pl.BlockSpec(shape, lambda bi, si: (0,) * len(shape))
    return pl.pallas_call(
        functools.partial(_rglru_kernel, ts=ts),
        grid=(b, s // ts),
        in_specs=[pl.BlockSpec((1, ts, D_RNN), lambda bi, si: (bi, si, 0)),
                  pl.BlockSpec((1, 8, D_RNN),
                               lambda bi, si: (bi, jnp.maximum(si * (ts // 8) - 1, 0), 0)),
                  pl.BlockSpec((1, ts, D_RNN), lambda bi, si: (bi, si, 1)),
                  full((CONV_W, D_RNN)), full((1, D_RNN)),
                  full(wa_bd.shape), full((1, D_RNN)),
                  full(wx_bd.shape), full((1, D_RNN)), full((1, D_RNN))],
        out_specs=pl.BlockSpec((1, ts, D_RNN), lambda bi, si: (bi, si, 0)),
        out_shape=jax.ShapeDtypeStruct((b, s, D_RNN), BF16),
        scratch_shapes=[pltpu.VMEM((8, D_RNN), F32)],
        compiler_params=_cparams(("parallel", "arbitrary")),
        name="rglru",
    )(xr_yg, xr_yg, xr_yg, conv_w, conv_b, wa_bd, ba, wx_bd, bx, lam)


def _memattn_kernel(q_ref, kv_ref, o_ref):
    scale = 1.0 / math.sqrt(HEAD_DIM)
    for h in range(MEM_HEADS):
        hs = slice(h * HEAD_DIM, (h + 1) * HEAD_DIM)
        vs = slice(MEM_WIDTH + h * HEAD_DIM, MEM_WIDTH + (h + 1) * HEAD_DIM)
        s = _dot_nt(q_ref[0, :, hs], kv_ref[0, :, hs]) * scale
        m = jnp.max(s, axis=-1, keepdims=True)
        p = jnp.exp(s - m)
        l = jnp.sum(p, axis=-1, keepdims=True)
        o_ref[0, :, hs] = (_dot(p.astype(BF16), kv_ref[0, :, vs]) / l).astype(o_ref.dtype)


def _memattn(mq, kv, ts=1024):
    b, s, _ = mq.shape
    mem_len = kv.shape[1]
    return pl.pallas_call(
        _memattn_kernel,
        grid=(b, s // ts),
        in_specs=[pl.BlockSpec((1, ts, MEM_WIDTH), lambda bi, si: (bi, si, 0)),
                  pl.BlockSpec((1, mem_len, 2 * MEM_WIDTH), lambda bi, si: (bi, 0, 0))],
        out_specs=pl.BlockSpec((1, ts, MEM_WIDTH), lambda bi, si: (bi, si, 0)),
        out_shape=jax.ShapeDtypeStruct((b, s, MEM_WIDTH), BF16),
        compiler_params=_cparams(("parallel", "arbitrary")),
        name="memattn",
    )(mq, kv)


def _mix_kernel(o0_ref, o1_ref, o2_ref, l0_ref, l1_ref, l2_ref, rec_ref, memo_ref, gl_ref, x_ref,
                bg_ref, wa_ref, wl_ref, wm_ref, wo_ref, g_ref, b_ref, x1_ref, x1b_ref):
    l0, l1, l2 = l0_ref[...], l1_ref[...], l2_ref[...]
    mx = jnp.maximum(jnp.maximum(l0, l1), l2)
    e0, e1, e2 = jnp.exp(l0 - mx), jnp.exp(l1 - mx), jnp.exp(l2 - mx)
    attn = (e0 * o0_ref[...].astype(F32) + e1 * o1_ref[...].astype(F32)
            + e2 * o2_ref[...].astype(F32)) / (e0 + e1 + e2)
    d = D_MODEL
    gate = lambda j: jax.nn.sigmoid(gl_ref[:, j * d:(j + 1) * d] + bg_ref[j:j + 1, :])
    merged = (gate(0) * _dot(attn.astype(BF16), wa_ref[...])
              + gate(1) * _dot(rec_ref[...], wl_ref[...])
              + gate(2) * _dot(memo_ref[...], wm_ref[...]))
    mix = _dot(merged.astype(BF16), wo_ref[...])
    x1 = _layernorm(ALPHA * x_ref[...] + mix, g_ref[...], b_ref[...])
    x1_ref[...] = x1
    x1b_ref[...] = x1.astype(BF16)


def _mix(os, lses, rec, memo, gl, x, b_gate, wa, wl, wm, wo, g, bta, tt=512):
    t = x.shape[0]
    rows = lambda w: pl.BlockSpec((tt, w), lambda i: (i, 0))
    full = lambda a: pl.BlockSpec(a.shape, lambda i: (0,) * a.ndim)
    return pl.pallas_call(
        _mix_kernel,
        grid=(t // tt,),
        in_specs=[rows(GROUP_W)] * 6 + [rows(D_RNN), rows(MEM_WIDTH), rows(3 * D_MODEL), rows(D_MODEL),
                                        full(b_gate), full(wa), full(wl), full(wm), full(wo),
                                        full(g), full(bta)],
        out_specs=[rows(D_MODEL), rows(D_MODEL)],
        out_shape=[jax.ShapeDtypeStruct((t, D_MODEL), F32), jax.ShapeDtypeStruct((t, D_MODEL), BF16)],
        compiler_params=_cparams(("parallel",)),
        name="mix_ln1",
    )(*os, *lses, rec, memo, gl, x, b_gate, wa, wl, wm, wo, g, bta)


def _peer_scores_kernel(x_ref, wq_ref, kbd_ref, o_ref):
    q = _dot(x_ref[...], wq_ref[...]).astype(BF16)
    for h in range(PEER_HEADS):
        hs = slice(h * PEER_KEY_DIM, (h + 1) * PEER_KEY_DIM)
        o_ref[hs, :] = _dot_nt(kbd_ref[h], q[:, hs])


def _peer_scores(x1b, wq, kbd, tt=512):
    t = x1b.shape[0]
    w = PEER_HEADS * PEER_KEY_DIM
    return pl.pallas_call(
        _peer_scores_kernel,
        grid=(t // tt,),
        in_specs=[pl.BlockSpec((tt, D_MODEL), lambda i: (i, 0)),
                  pl.BlockSpec(wq.shape, lambda i: (0, 0)),
                  pl.BlockSpec(kbd.shape, lambda i: (0, 0, 0))],
        out_specs=pl.BlockSpec((w, tt), lambda i: (0, i)),
        out_shape=jax.ShapeDtypeStruct((w, t), F32),
        compiler_params=_cparams(("parallel",)),
        name="peer_scores",
    )(x1b, wq, kbd)


_PEER_CANDS = tuple((p, q) for p in range(PEER_TOPK) for q in range(PEER_TOPK)
                    if (p + 1) * (q + 1) <= PEER_TOPK)


def _top_positions(s):
    key = lax.broadcasted_iota(jnp.int32, s.shape, 0).astype(F32)
    pos = jnp.full(s.shape, float(PEER_TOPK), F32)
    work = s
    tops = []
    for p in range(PEER_TOPK):
        mx = jnp.max(work, axis=0, keepdims=True)
        first = jnp.min(jnp.where(work == mx, key, float(N_KEYS)), axis=0, keepdims=True)
        hit = key == first
        pos = jnp.where(hit, float(p), pos)
        work = jnp.where(hit, -jnp.inf, work)
        tops.append(mx)
    return pos, tops


def _peer_select_kernel(sc_ref, ra_ref, ea_ref, cb_ref, eb_ref):
    tt = sc_ref.shape[1]
    pos_b, tops_a, tops_b = [], [], []
    for h in range(PEER_HEADS):
        base = h * PEER_KEY_DIM
        pa, ta = _top_positions(sc_ref[base:base + N_KEYS, :])
        pb, tb = _top_positions(sc_ref[base + N_KEYS:base + 2 * N_KEYS, :])
        ra_ref[h * N_KEYS:(h + 1) * N_KEYS, :] = pa
        pos_b.append(pb)
        tops_a.append(ta)
        tops_b.append(tb)
    a_top = [jnp.concatenate([tops_a[h][p] for h in range(PEER_HEADS)], axis=0)
             for p in range(PEER_TOPK)]
    b_top = [jnp.concatenate([tops_b[h][q] for h in range(PEER_HEADS)], axis=0)
             for q in range(PEER_TOPK)]
    sums = [a_top[p] + b_top[q] for p, q in _PEER_CANDS]
    n = len(sums)
    rank = [jnp.zeros((PEER_HEADS, tt), F32) for _ in range(n)]
    for c in range(n):
        for c2 in range(c):
            ge = sums[c2] >= sums[c]
            rank[c] = rank[c] + jnp.where(ge, 1.0, 0.0)
            rank[c2] = rank[c2] + jnp.where(ge, 0.0, 1.0)
    cnt = [jnp.zeros((PEER_HEADS, tt), F32) for _ in range(PEER_TOPK)]
    zsum = jnp.zeros((PEER_HEADS, tt), F32)
    for c, (p, q) in enumerate(_PEER_CANDS):
        sel = rank[c] < float(PEER_TOPK)
        cnt[q] = cnt[q] + jnp.where(sel, 1.0, 0.0)
        zsum = zsum + jnp.where(sel, jnp.exp(sums[c] - sums[0]), 0.0)
    inv_z = 1.0 / zsum
    for h in range(PEER_HEADS):
        base = h * PEER_KEY_DIM
        rows = slice(h * N_KEYS, (h + 1) * N_KEYS)
        cb = jnp.zeros((N_KEYS, tt), F32)
        for q in range(PEER_TOPK):
            cb = cb + jnp.where(pos_b[h] == float(q), cnt[q][h:h + 1, :], 0.0)
        cb_ref[rows, :] = cb.astype(cb_ref.dtype)
        ea_ref[rows, :] = (jnp.exp(sc_ref[base:base + N_KEYS, :] - a_top[0][h:h + 1, :])
                           * inv_z[h:h + 1, :])
        eb_ref[rows, :] = jnp.exp(sc_ref[base + N_KEYS:base + 2 * N_KEYS, :]
                                  - b_top[0][h:h + 1, :]).astype(eb_ref.dtype)


def _peer_select(sc, tt=256):
    w, t = sc.shape
    rows = PEER_HEADS * N_KEYS
    out_spec = pl.BlockSpec((rows, tt), lambda i: (0, i))
    return pl.pallas_call(
        _peer_select_kernel,
        grid=(t // tt,),
        in_specs=[pl.BlockSpec((w, tt), lambda i: (0, i))],
        out_specs=[out_spec] * 4,
        out_shape=[jax.ShapeDtypeStruct((rows, t), F32), jax.ShapeDtypeStruct((rows, t), F32),
                   jax.ShapeDtypeStruct((rows, t), BF16), jax.ShapeDtypeStruct((rows, t), BF16)],
        compiler_params=_cparams(("parallel",)),
        name="peer_select",
    )(sc)


def _peer_mix_kernel(x1b_ref, x1_ref, u_ref, vt_ref, ra_ref, ea_ref, cb_ref, eb_ref,
                     g_ref, b_ref, o_ref, acc_ref, *, ib):
    e = pl.program_id(1)
    tt = x1b_ref.shape[0]

    @pl.when(e == 0)
    def _():
        acc_ref[...] = jnp.zeros_like(acc_ref)

    x1b = x1b_ref[...]
    acc = acc_ref[...]
    kg = N_KEYS // BF16_ROWS
    n_chunks = ib // PEER_CHUNK_KEYS
    chunk_rows = lambda c: slice(c * PEER_CHUNK_KEYS * N_KEYS, (c + 1) * PEER_CHUNK_KEYS * N_KEYS)
    h_next = _dot_nt(u_ref[chunk_rows(0), :], x1b)
    for c in range(n_chunks):
        rows = chunk_rows(c)
        hc = h_next
        if c + 1 < n_chunks:
            h_next = _dot_nt(u_ref[chunk_rows(c + 1), :], x1b)
        ys = []
        for k in range(PEER_CHUNK_KEYS):
            ii = c * PEER_CHUNK_KEYS + k
            w = jnp.zeros((kg, BF16_ROWS, tt), BF16)
            for h in range(PEER_HEADS):
                hs = slice(h * kg, (h + 1) * kg)
                ra = jnp.broadcast_to(ra_ref[h, ii:ii + 1, :], (BF16_ROWS, tt)).astype(BF16)
                ea = jnp.broadcast_to(ea_ref[h, ii:ii + 1, :], (BF16_ROWS, tt)).astype(BF16)
                w = w + jnp.where(cb_ref[hs] > ra[None], eb_ref[hs] * ea[None], jnp.zeros_like(w))
            act = _gelu_erf(hc[k * N_KEYS:(k + 1) * N_KEYS, :]).astype(BF16)
            ys.append(w.reshape(N_KEYS, tt) * act)
        acc = acc + _dot(vt_ref[:, rows], jnp.concatenate(ys, axis=0))
    acc_ref[...] = acc

    @pl.when(e == pl.num_programs(1) - 1)
    def _():
        o_ref[...] = _layernorm(ALPHA * x1_ref[...] + acc_ref[...].T, g_ref[...], b_ref[...])


def _peer_mix(x1b, x1, u, vt, ra, ea, cb, eb, g, bta, tt=512, te=2048):
    t = x1.shape[0]
    ib = te // N_KEYS
    ne = N_EXPERTS // te
    rows = PEER_HEADS * N_KEYS
    return pl.pallas_call(
        functools.partial(_peer_mix_kernel, ib=ib),
        grid=(t // tt, ne),
        in_specs=[pl.BlockSpec((tt, D_MODEL), lambda i, e: (i, 0)),
                  pl.BlockSpec((tt, D_MODEL), lambda i, e: (i, 0)),
                  pl.BlockSpec((te, D_MODEL), lambda i, e: (e, 0)),
                  pl.BlockSpec((D_MODEL, te), lambda i, e: (0, e)),
                  pl.BlockSpec((PEER_HEADS, ib, tt), lambda i, e: (0, e, i)),
                  pl.BlockSpec((PEER_HEADS, ib, tt), lambda i, e: (0, e, i)),
                  pl.BlockSpec((rows // BF16_ROWS, BF16_ROWS, tt), lambda i, e: (0, 0, i)),
                  pl.BlockSpec((rows // BF16_ROWS, BF16_ROWS, tt), lambda i, e: (0, 0, i)),
                  pl.BlockSpec((1, D_MODEL), lambda i, e: (0, 0)),
                  pl.BlockSpec((1, D_MODEL), lambda i, e: (0, 0))],
        out_specs=pl.BlockSpec((tt, D_MODEL), lambda i, e: (i, 0)),
        out_shape=jax.ShapeDtypeStruct((t, D_MODEL), F32),
        scratch_shapes=[pltpu.VMEM((D_MODEL, tt), F32)],
        compiler_params=_cparams(("parallel", "arbitrary")),
        name="peer_mix_ln2",
    )(x1b, x1, u, vt, ra, ea, cb, eb, g, bta)


def _block_diag(w, per_tile):
    n, k, _ = w.shape
    w = w.reshape(n // per_tile, per_tile, k, k)
    eye = jnp.eye(per_tile, dtype=w.dtype)
    out = jnp.einsum('tpij,pq->tpiqj', w, eye)
    return out.reshape(n // per_tile, per_tile * k, per_tile * k)


def _layer(x, mem, w_in, b_gate, conv_w, conv_b, lru_wa, lru_ba, lru_wx, lru_bx, lru_lambda,
           w_mem_kv, w_br_attn, w_br_lru, w_br_mem, w_out, ln1_g, ln1_b,
           peer_wq, peer_keys, peer_u, peer_v, ln2_g, ln2_b):
    b, s, d = x.shape
    t = b * s
    xf = x.reshape(t, d)
    xb = xf.astype(BF16)
    w_in_b = w_in.astype(BF16)
    row = lambda a: a.reshape(1, -1).astype(F32)

    off_rnn = 3 * DSWA_WIDTH
    off_mq = off_rnn + 2 * D_RNN
    off_gl = off_mq + MEM_WIDTH
    qkv = _matmul(xb, w_in_b, 0, 3 * DSWA_WIDTH, BF16, tn=512).reshape(b, s, 3 * DSWA_WIDTH)
    xr_yg = _matmul(xb, w_in_b, off_rnn, 2 * D_RNN, F32, tn=512).reshape(b, s, 2 * D_RNN)
    mq = _matmul(xb, w_in_b, off_mq, MEM_WIDTH, BF16, tn=512).reshape(b, s, MEM_WIDTH)
    gl = _matmul(xb, w_in_b, off_gl, 3 * D_MODEL, F32, tn=1024)

    os, lses = [], []
    for gi, dil in enumerate(DSWA_DILATIONS):
        o_g, lse_g = _dswa_group(qkv, gi, dil)
        os.append(o_g.reshape(t, GROUP_W))
        lses.append(lse_g.reshape(t, GROUP_W))

    per_tile = MXU_TILE // LRU_BW
    wa_bd = _block_diag(lru_wa, per_tile).astype(BF16)
    wx_bd = _block_diag(lru_wx, per_tile).astype(BF16)
    rec = _rglru(xr_yg, conv_w.astype(F32), row(conv_b), wa_bd, row(lru_ba), wx_bd, row(lru_bx),
                 row(lru_lambda)).reshape(t, D_RNN)

    mem_len = mem.shape[1]
    kv = _matmul(mem.reshape(b * mem_len, d).astype(BF16), w_mem_kv.astype(BF16), 0,
                 2 * MEM_WIDTH, BF16, tn=1024).reshape(b, mem_len, 2 * MEM_WIDTH)
    memo = _memattn(mq, kv).reshape(t, MEM_WIDTH)

    x1, x1b = _mix(os, lses, rec, memo, gl, xf, b_gate.astype(F32), w_br_attn.astype(BF16),
                   w_br_lru.astype(BF16), w_br_mem.astype(BF16), w_out.astype(BF16),
                   row(ln1_g), row(ln1_b))

    kbd = _block_diag(peer_keys.reshape(PEER_HEADS * 2, N_KEYS, PEER_KEY_DIM // 2), 2).astype(BF16)
    scores = _peer_scores(x1b, peer_wq.astype(BF16), kbd)

    ra, ea, cb, eb = _peer_select(scores)
    ra = ra.reshape(PEER_HEADS, N_KEYS, t)
    ea = ea.reshape(PEER_HEADS, N_KEYS, t)

    cb = cb.reshape(-1, BF16_ROWS, t)
    eb = eb.reshape(-1, BF16_ROWS, t)
    out = _peer_mix(x1b, x1, peer_u.astype(BF16), peer_v.T.astype(BF16), ra, ea, cb, eb,
                    row(ln2_g), row(ln2_b))
    return out.reshape(b, s, d)


def kernel(x, mem, w_in, b_gate, conv_w, conv_b, lru_wa, lru_ba, lru_wx, lru_bx, lru_lambda, w_mem_kv, w_br_attn, w_br_lru, w_br_mem, w_out, ln1_g, ln1_b, peer_wq, peer_keys, peer_u, peer_v, ln2_g, ln2_b):
    h = x.astype(F32)
    depth = w_in.shape[0]
    for l in range(depth):
        h = _layer(h, mem, w_in[l], b_gate[l], conv_w[l], conv_b[l], lru_wa[l], lru_ba[l],
                   lru_wx[l], lru_bx[l], lru_lambda[l], w_mem_kv[l], w_br_attn[l], w_br_lru[l],
                   w_br_mem[l], w_out[l], ln1_g[l], ln1_b[l], peer_wq[l], peer_keys[l],
                   peer_u[l], peer_v[l], ln2_g[l], ln2_b[l])
    return h.astype(x.dtype)
```

```python
import functools
import math

import jax
import jax.numpy as jnp
from jax import lax
from jax.experimental import pallas as pl
from jax.experimental.pallas import tpu as pltpu

F32 = jnp.float32
BF16 = jnp.bfloat16

D_MODEL = 1024
N_GROUPS = 3
DSWA_DILATIONS = (1, 4, 16)
DSWA_HEADS = 4
HEAD_DIM = 128
DSWA_BLK = 128
DSWA_TILE = 2048
GROUP_W = DSWA_HEADS * HEAD_DIM
DSWA_WIDTH = N_GROUPS * GROUP_W
D_RNN = 1024
LRU_BLOCKS = 16
LRU_BW = D_RNN // LRU_BLOCKS
CONV_W = 4
LRU_C = 8.0
MEM_HEADS = 4
MEM_WIDTH = MEM_HEADS * HEAD_DIM
PEER_HEADS = 8
PEER_KEY_DIM = 256
N_KEYS = 128
N_EXPERTS = N_KEYS * N_KEYS
PEER_TOPK = 16
PEER_CHUNK_KEYS = 4
BF16_ROWS = 16
ALPHA = 2.0 ** 0.25
LN_EPS = 1e-5
NEG_INF = -1e30

LANES = 128
MXU_TILE = 256
VMEM_LIMIT = 56 * 1024 * 1024


def _cparams(sem):
    return pltpu.CompilerParams(dimension_semantics=sem, vmem_limit_bytes=VMEM_LIMIT)


def _dot(a, b):
    return jnp.dot(a, b, preferred_element_type=F32)


def _dot_nt(a, b):
    return lax.dot_general(a, b, (((1,), (1,)), ((), ())), preferred_element_type=F32)


def _gelu_erf(x):
    return 0.5 * x * (1.0 + lax.erf(x * (1.0 / math.sqrt(2.0))))


def _layernorm(h, g, b):
    mu = jnp.mean(h, axis=-1, keepdims=True)
    c = h - mu
    var = jnp.mean(c * c, axis=-1, keepdims=True)
    return c * lax.rsqrt(var + LN_EPS) * g + b


def _matmul_kernel(x_ref, w_ref, o_ref):
    o_ref[...] = _dot(x_ref[...], w_ref[...]).astype(o_ref.dtype)


def _matmul(x, w, col_off, n_cols, out_dtype, tm=1024, tn=1024):
    m, k = x.shape
    tn = min(tn, n_cols)
    tm = min(tm, m)
    assert m % tm == 0 and n_cols % tn == 0 and col_off % tn == 0
    off = col_off // tn
    return pl.pallas_call(
        _matmul_kernel,
        grid=(m // tm, n_cols // tn),
        in_specs=[pl.BlockSpec((tm, k), lambda i, j: (i, 0)),
                  pl.BlockSpec((k, tn), lambda i, j: (0, off + j))],
        out_specs=pl.BlockSpec((tm, tn), lambda i, j: (i, j)),
        out_shape=jax.ShapeDtypeStruct((m, n_cols), out_dtype),
        compiler_params=_cparams(("parallel", "arbitrary")),
        name="matmul",
    )(x, w)


def _permute_kernel(x_ref, o_ref, *, dilation):
    rows = x_ref.shape[1] // dilation
    for r in range(dilation):
        o_ref[0, r] = x_ref[0, pl.ds(r, rows, stride=dilation), :].astype(o_ref.dtype)


def _permute_rows(x, dilation):
    b, s, d = x.shape
    m_len = s // dilation
    return pl.pallas_call(
        functools.partial(_permute_kernel, dilation=dilation),
        grid=(b, s // DSWA_TILE, d // LANES),
        in_specs=[pl.BlockSpec((1, DSWA_TILE, LANES), lambda bi, n, c: (bi, n, c))],
        out_specs=pl.BlockSpec((1, dilation, DSWA_TILE // dilation, LANES),
                               lambda bi, n, c: (bi, 0, n, c)),
        out_shape=jax.ShapeDtypeStruct((b, dilation, m_len, d), BF16),
        compiler_params=_cparams(("parallel", "parallel", "parallel")),
        name=f"permute_d{dilation}",
    )(x)


def _dswa_kernel(q_ref, kc_ref, kp_ref, vc_ref, vp_ref, o_ref, lse_ref, o_scr, lse_scr, *, dilation):
    nq = q_ref.shape[2] // DSWA_BLK
    row = lax.broadcasted_iota(jnp.int32, (DSWA_BLK, DSWA_BLK), 0)
    col = lax.broadcasted_iota(jnp.int32, (DSWA_BLK, DSWA_BLK), 1)
    mask_cur = col <= row
    mask_band = col >= row
    mask_first = col >= row + jnp.where(pl.program_id(1) == 0, DSWA_BLK, 0)
    scale = 1.0 / math.sqrt(HEAD_DIM)
    for r in range(dilation):
        for j in range(nq):
            rows = slice(j * DSWA_BLK, (j + 1) * DSWA_BLK)
            q = q_ref[0, r, rows, :]
            kcur = kc_ref[0, r, rows, :]
            vcur = vc_ref[0, r, rows, :]
            if j == 0:
                kprev = kp_ref[0, r]
                vprev = vp_ref[0, r]
                mask_prev = mask_first
            else:
                prows = slice((j - 1) * DSWA_BLK, j * DSWA_BLK)
                kprev = kc_ref[0, r, prows, :]
                vprev = vc_ref[0, r, prows, :]
                mask_prev = mask_band
            dst = pl.ds(j * DSWA_BLK * dilation + r, DSWA_BLK, stride=dilation)
            lse_tile = jnp.zeros((DSWA_BLK, HEAD_DIM), F32)
            for h in range(DSWA_HEADS):
                hs = slice(h * HEAD_DIM, (h + 1) * HEAD_DIM)
                qh = q[:, hs]
                s_p = jnp.where(mask_prev, _dot_nt(qh, kprev[:, hs]) * scale, NEG_INF)
                s_c = jnp.where(mask_cur, _dot_nt(qh, kcur[:, hs]) * scale, NEG_INF)
                m = jnp.maximum(jnp.max(s_p, axis=-1, keepdims=True),
                                jnp.max(s_c, axis=-1, keepdims=True))
                p_p = jnp.exp(s_p - m)
                p_c = jnp.exp(s_c - m)
                l = jnp.sum(p_p, axis=-1, keepdims=True) + jnp.sum(p_c, axis=-1, keepdims=True)
                o = _dot(p_p.astype(BF16), vprev[:, hs]) + _dot(p_c.astype(BF16), vcur[:, hs])
                o_scr[h, dst, :] = o / l
                lse_tile = jnp.where(col == h, m + jnp.log(l), lse_tile)
            lse_scr[dst, :] = lse_tile
    for h in range(DSWA_HEADS):
        o_ref[0, :, h * HEAD_DIM:(h + 1) * HEAD_DIM] = o_scr[h].astype(o_ref.dtype)
    lse_ref[0] = lse_scr[...]


def _dswa_group(qkv, group):
    b, dilation, m_len, _ = qkv.shape
    s = dilation * m_len
    assert s % DSWA_TILE == 0 and DSWA_TILE % (dilation * DSWA_BLK) == 0
    rows = DSWA_TILE // dilation
    nq = rows // DSWA_BLK

    def cur(c):
        return pl.BlockSpec((1, dilation, rows, GROUP_W), lambda bi, n: (bi, 0, n, c))

    def prev(c):
        return pl.BlockSpec((1, dilation, DSWA_BLK, GROUP_W),
                            lambda bi, n: (bi, 0, jnp.maximum(n * nq - 1, 0), c))

    return pl.pallas_call(
        functools.partial(_dswa_kernel, dilation=dilation),
        grid=(b, s // DSWA_TILE),
        in_specs=[cur(0), cur(1), prev(1), cur(2), prev(2)],
        out_specs=[pl.BlockSpec((1, DSWA_TILE, GROUP_W), lambda bi, n: (bi, n, 0)),
                   pl.BlockSpec((1, DSWA_TILE, HEAD_DIM), lambda bi, n: (bi, n, 0))],
        out_shape=[jax.ShapeDtypeStruct((b, s, GROUP_W), BF16),
                   jax.ShapeDtypeStruct((b, s, HEAD_DIM), F32)],
        scratch_shapes=[pltpu.VMEM((DSWA_HEADS, DSWA_TILE, HEAD_DIM), F32),
                        pltpu.VMEM((DSWA_TILE, HEAD_DIM), F32)],
        compiler_params=_cparams(("parallel", "arbitrary")),
        name=f"dswa_g{group}",
    )(qkv, qkv, qkv, qkv, qkv)


def _shift_rows(x, k, fill):
    rolled = pltpu.roll(x, k, 0)
    row = lax.broadcasted_iota(jnp.int32, x.shape, 0)
    return jnp.where(row >= k, rolled, fill)


def _rglru_kernel(xr_ref, halo_ref, yg_ref, cw_ref, cb_ref, wa_ref, ba_ref, wx_ref, bx_ref,
                  lam_ref, o_ref, h_ref, *, ts):
    si = pl.program_id(1)

    @pl.when(si == 0)
    def _():
        h_ref[...] = jnp.zeros_like(h_ref)

    x = xr_ref[0]
    halo = jnp.where(si > 0, halo_ref[0], 0.0)
    xfull = jnp.concatenate([halo, x], axis=0)
    xc = cb_ref[...] + cw_ref[CONV_W - 1:CONV_W, :] * x
    for k in range(1, CONV_W):
        xc = xc + cw_ref[CONV_W - 1 - k:CONV_W - k, :] * pltpu.roll(xfull, k, 0)[8:]

    xb = xc.astype(BF16)
    nblk = D_RNN // MXU_TILE
    ra = jnp.concatenate(
        [_dot(xb[:, c * MXU_TILE:(c + 1) * MXU_TILE], wa_ref[c]) for c in range(nblk)], axis=1)
    rx = jnp.concatenate(
        [_dot(xb[:, c * MXU_TILE:(c + 1) * MXU_TILE], wx_ref[c]) for c in range(nblk)], axis=1)
    r = jax.nn.sigmoid(ra + ba_ref[...])
    gate_i = jax.nn.sigmoid(rx + bx_ref[...])
    neg_lam = -lam_ref[...]
    softplus = jnp.maximum(neg_lam, 0.0) + jnp.log1p(jnp.exp(-jnp.abs(neg_lam)))
    log_a = (-LRU_C) * r * softplus
    a = jnp.exp(log_a)
    bterm = jnp.sqrt(1.0 - a * a) * gate_i * xc

    step = 1
    while step < ts:
        a_s = _shift_rows(a, step, 1.0)
        b_s = _shift_rows(bterm, step, 0.0)
        bterm = a * b_s + bterm
        a = a * a_s
        step *= 2
    h = a * h_ref[0:1, :] + bterm
    h_ref[0:1, :] = h[ts - 1:ts, :]
    o_ref[0] = (h * jax.nn.gelu(yg_ref[0], approximate=True)).astype(o_ref.dtype)


def _rglru(xr_yg, conv_w, conv_b, wa_bd, ba, wx_bd, bx, lam, ts=256):
    b, s, _ = xr_yg.shape
    assert s % ts == 0
    full = lambda shape: pl.BlockSpec(shape, lambda bi, si: (0,) * len(shape))
    return pl.pallas_call(
        functools.partial(_rglru_kernel, ts=ts),
        grid=(b, s // ts),
        in_specs=[pl.BlockSpec((1, ts, D_RNN), lambda bi, si: (bi, si, 0)),
                  pl.BlockSpec((1, 8, D_RNN),
                               lambda bi, si: (bi, jnp.maximum(si * (ts // 8) - 1, 0), 0)),
                  pl.BlockSpec((1, ts, D_RNN), lambda bi, si: (bi, si, 1)),
                  full((CONV_W, D_RNN)), full((1, D_RNN)),
                  full(wa_bd.shape), full((1, D_RNN)),
                  full(wx_bd.shape), full((1, D_RNN)), full((1, D_RNN))],
        out_specs=pl.BlockSpec((1, ts, D_RNN), lambda bi, si: (bi, si, 0)),
        out_shape=jax.ShapeDtypeStruct((b, s, D_RNN), BF16),
        scratch_shapes=[pltpu.VMEM((8, D_RNN), F32)],
        compiler_params=_cparams(("parallel", "arbitrary")),
        name="rglru",
    )(xr_yg, xr_yg, xr_yg, conv_w, conv_b, wa_bd, ba, wx_bd, bx, lam)


def _memattn_kernel(q_ref, kv_ref, o_ref):
    scale = 1.0 / math.sqrt(HEAD_DIM)
    for h in range(MEM_HEADS):
        hs = slice(h * HEAD_DIM, (h + 1) * HEAD_DIM)
        vs = slice(MEM_WIDTH + h * HEAD_DIM, MEM_WIDTH + (h + 1) * HEAD_DIM)
        s = _dot_nt(q_ref[0, :, hs], kv_ref[0, :, hs]) * scale
        m = jnp.max(s, axis=-1, keepdims=True)
        p = jnp.exp(s - m)
        l = jnp.sum(p, axis=-1, keepdims=True)
        o_ref[0, :, hs] = (_dot(p.astype(BF16), kv_ref[0, :, vs]) / l).astype(o_ref.dtype)


def _memattn(mq, kv, ts=1024):
    b, s, _ = mq.shape
    mem_len = kv.shape[1]
    return pl.pallas_call(
        _memattn_kernel,
        grid=(b, s // ts),
        in_specs=[pl.BlockSpec((1, ts, MEM_WIDTH), lambda bi, si: (bi, si, 0)),
                  pl.BlockSpec((1, mem_len, 2 * MEM_WIDTH), lambda bi, si: (bi, 0, 0))],
        out_specs=pl.BlockSpec((1, ts, MEM_WIDTH), lambda bi, si: (bi, si, 0)),
        out_shape=jax.ShapeDtypeStruct((b, s, MEM_WIDTH), BF16),
        compiler_params=_cparams(("parallel", "arbitrary")),
        name="memattn",
    )(mq, kv)


def _mix_kernel(o0_ref, o1_ref, o2_ref, l0_ref, l1_ref, l2_ref, rec_ref, memo_ref, gl_ref, x_ref,
                bg_ref, wa_ref, wl_ref, wm_ref, wo_ref, g_ref, b_ref, x1_ref, x1b_ref):
    l0, l1, l2 = l0_ref[...], l1_ref[...], l2_ref[...]
    mx = jnp.maximum(jnp.maximum(l0, l1), l2)
    e0, e1, e2 = jnp.exp(l0 - mx), jnp.exp(l1 - mx), jnp.exp(l2 - mx)
    inv = 1.0 / (e0 + e1 + e2)
    w0, w1, w2 = e0 * inv, e1 * inv, e2 * inv
    parts = []
    for h in range(DSWA_HEADS):
        hs = slice(h * HEAD_DIM, (h + 1) * HEAD_DIM)
        parts.append((w0[:, h:h + 1] * o0_ref[:, hs].astype(F32)
                      + w1[:, h:h + 1] * o1_ref[:, hs].astype(F32)
                      + w2[:, h:h + 1] * o2_ref[:, hs].astype(F32)).astype(BF16))
    attn = jnp.concatenate(parts, axis=1)
    d = D_MODEL
    gate = lambda j: jax.nn.sigmoid(gl_ref[:, j * d:(j + 1) * d] + bg_ref[j:j + 1, :])
    merged = (gate(0) * _dot(attn, wa_ref[...])
              + gate(1) * _dot(rec_ref[...], wl_ref[...])
              + gate(2) * _dot(memo_ref[...], wm_ref[...]))
    mix = _dot(merged.astype(BF16), wo_ref[...])
    x1 = _layernorm(ALPHA * x_ref[...] + mix, g_ref[...], b_ref[...])
    x1_ref[...] = x1
    x1b_ref[...] = x1.astype(BF16)


def _mix(os, lses, rec, memo, gl, x, b_gate, wa, wl, wm, wo, g, bta, tt=512):
    t = x.shape[0]
    rows = lambda w: pl.BlockSpec((tt, w), lambda i: (i, 0))
    full = lambda a: pl.BlockSpec(a.shape, lambda i: (0,) * a.ndim)
    return pl.pallas_call(
        _mix_kernel,
        grid=(t // tt,),
        in_specs=[rows(GROUP_W)] * 3 + [rows(HEAD_DIM)] * 3 + [
            rows(D_RNN), rows(MEM_WIDTH), rows(3 * D_MODEL), rows(D_MODEL),
            full(b_gate), full(wa), full(wl), full(wm), full(wo), full(g), full(bta)],
        out_specs=[rows(D_MODEL), rows(D_MODEL)],
        out_shape=[jax.ShapeDtypeStruct((t, D_MODEL), F32), jax.ShapeDtypeStruct((t, D_MODEL), BF16)],
        compiler_params=_cparams(("parallel",)),
        name="mix_ln1",
    )(*os, *lses, rec, memo, gl, x, b_gate, wa, wl, wm, wo, g, bta)


def _peer_scores_kernel(x_ref, wq_ref, kbd_ref, o_ref):
    q = _dot(x_ref[...], wq_ref[...]).astype(BF16)
    for h in range(PEER_HEADS):
        hs = slice(h * PEER_KEY_DIM, (h + 1) * PEER_KEY_DIM)
        o_ref[hs, :] = _dot_nt(kbd_ref[h], q[:, hs])


def _peer_scores(x1b, wq, kbd, tt=512):
    t = x1b.shape[0]
    w = PEER_HEADS * PEER_KEY_DIM
    return pl.pallas_call(
        _peer_scores_kernel,
        grid=(t // tt,),
        in_specs=[pl.BlockSpec((tt, D_MODEL), lambda i: (i, 0)),
                  pl.BlockSpec(wq.shape, lambda i: (0, 0)),
                  pl.BlockSpec(kbd.shape, lambda i: (0, 0, 0))],
        out_specs=pl.BlockSpec((w, tt), lambda i: (0, i)),
        out_shape=jax.ShapeDtypeStruct((w, t), F32),
        compiler_params=_cparams(("parallel",)),
        name="peer_scores",
    )(x1b, wq, kbd)


_PEER_CANDS = tuple((p, q) for p in range(PEER_TOPK) for q in range(PEER_TOPK)
                    if (p + 1) * (q + 1) <= PEER_TOPK)


def _top_positions(s):
    key = lax.broadcasted_iota(jnp.int32, s.shape, 0).astype(F32)
    pos = jnp.full(s.shape, float(PEER_TOPK), F32)
    work = s
    tops = []
    for p in range(PEER_TOPK):
        mx = jnp.max(work, axis=0, keepdims=True)
        first = jnp.min(jnp.where(work == mx, key, float(N_KEYS)), axis=0, keepdims=True)
        hit = key == first
        pos = jnp.where(hit, float(p), pos)
        work = jnp.where(hit, -jnp.inf, work)
        tops.append(mx)
    return pos, tops


def _peer_select_kernel(sc_ref, ra_ref, ea_ref, cb_ref, eb_ref):
    tt = sc_ref.shape[1]
    pos_b, tops_a, tops_b = [], [], []
    for h in range(PEER_HEADS):
        base = h * PEER_KEY_DIM
        pa, ta = _top_positions(sc_ref[base:base + N_KEYS, :])
        pb, tb = _top_positions(sc_ref[base + N_KEYS:base + 2 * N_KEYS, :])
        ra_ref[h * N_KEYS:(h + 1) * N_KEYS, :] = pa
        pos_b.append(pb)
        tops_a.append(ta)
        tops_b.append(tb)
    a_top = [jnp.concatenate([tops_a[h][p] for h in range(PEER_HEADS)], axis=0)
             for p in range(PEER_TOPK)]
    b_top = [jnp.concatenate([tops_b[h][q] for h in range(PEER_HEADS)], axis=0)
             for q in range(PEER_TOPK)]
    sums = [a_top[p] + b_top[q] for p, q in _PEER_CANDS]
    n = len(sums)
    rank = [jnp.zeros((PEER_HEADS, tt), F32) for _ in range(n)]
    for c in range(n):
        for c2 in range(c):
            ge = sums[c2] >= sums[c]
            rank[c] = rank[c] + jnp.where(ge, 1.0, 0.0)
            rank[c2] = rank[c2] + jnp.where(ge, 0.0, 1.0)
    cnt = [jnp.zeros((PEER_HEADS, tt), F32) for _ in range(PEER_TOPK)]
    zsum = jnp.zeros((PEER_HEADS, tt), F32)
    for c, (p, q) in enumerate(_PEER_CANDS):
        sel = rank[c] < float(PEER_TOPK)
        cnt[q] = cnt[q] + jnp.where(sel, 1.0, 0.0)
        zsum = zsum + jnp.where(sel, jnp.exp(sums[c] - sums[0]), 0.0)
    inv_z = 1.0 / zsum
    for h in range(PEER_HEADS):
        base = h * PEER_KEY_DIM
        rows = slice(h * N_KEYS, (h + 1) * N_KEYS)
        cb = jnp.zeros((N_KEYS, tt), F32)
        for q in range(PEER_TOPK):
            cb = cb + jnp.where(pos_b[h] == float(q), cnt[q][h:h + 1, :], 0.0)
        cb_ref[rows, :] = cb.astype(cb_ref.dtype)
        ea_ref[rows, :] = (jnp.exp(sc_ref[base:base + N_KEYS, :] - a_top[0][h:h + 1, :])
                           * inv_z[h:h + 1, :])
        eb_ref[rows, :] = jnp.exp(sc_ref[base + N_KEYS:base + 2 * N_KEYS, :]
                                  - b_top[0][h:h + 1, :]).astype(eb_ref.dtype)


def _peer_select(sc, tt=256):
    w, t = sc.shape
    rows = PEER_HEADS * N_KEYS
    out_spec = pl.BlockSpec((rows, tt), lambda i: (0, i))
    return pl.pallas_call(
        _peer_select_kernel,
        grid=(t // tt,),
        in_specs=[pl.BlockSpec((w, tt), lambda i: (0, i))],
        out_specs=[out_spec] * 4,
        out_shape=[jax.ShapeDtypeStruct((rows, t), F32), jax.ShapeDtypeStruct((rows, t), F32),
                   jax.ShapeDtypeStruct((rows, t), BF16), jax.ShapeDtypeStruct((rows, t), BF16)],
        compiler_params=_cparams(("parallel",)),
        name="peer_select",
    )(sc)


def _peer_mix_kernel(x1b_ref, x1_ref, u_ref, vt_ref, ra_ref, ea_ref, cb_ref, eb_ref,
                     g_ref, b_ref, o_ref, acc_ref, *, ib):
    e = pl.program_id(1)
    tt = x1b_ref.shape[0]

    @pl.when(e == 0)
    def _():
        acc_ref[...] = jnp.zeros_like(acc_ref)

    x1b = x1b_ref[...]
    acc = acc_ref[...]
    kg = N_KEYS // BF16_ROWS
    n_chunks = ib // PEER_CHUNK_KEYS
    chunk_rows = lambda c: slice(c * PEER_CHUNK_KEYS * N_KEYS, (c + 1) * PEER_CHUNK_KEYS * N_KEYS)
    h_next = _dot_nt(u_ref[chunk_rows(0), :], x1b)
    for c in range(n_chunks):
        rows = chunk_rows(c)
        hc = h_next
        if c + 1 < n_chunks:
            h_next = _dot_nt(u_ref[chunk_rows(c + 1), :], x1b)
        ys = []
        for k in range(PEER_CHUNK_KEYS):
            ii = c * PEER_CHUNK_KEYS + k
            w = jnp.zeros((kg, BF16_ROWS, tt), BF16)
            for h in range(PEER_HEADS):
                hs = slice(h * kg, (h + 1) * kg)
                ra = jnp.broadcast_to(ra_ref[h, ii:ii + 1, :], (BF16_ROWS, tt)).astype(BF16)
                ea = jnp.broadcast_to(ea_ref[h, ii:ii + 1, :], (BF16_ROWS, tt)).astype(BF16)
                w = w + jnp.where(cb_ref[hs] > ra[None], eb_ref[hs] * ea[None], jnp.zeros_like(w))
            act = _gelu_erf(hc[k * N_KEYS:(k + 1) * N_KEYS, :]).astype(BF16)
            ys.append(w.reshape(N_KEYS, tt) * act)
        acc = acc + _dot(vt_ref[:, rows], jnp.concatenate(ys, axis=0))
    acc_ref[...] = acc

    @pl.when(e == pl.num_programs(1) - 1)
    def _():
        o_ref[...] = _layernorm(ALPHA * x1_ref[...] + acc_ref[...].T, g_ref[...], b_ref[...])


def _peer_mix(x1b, x1, u, vt, ra, ea, cb, eb, g, bta, tt=512, te=2048):
    t = x1.shape[0]
    ib = te // N_KEYS
    ne = N_EXPERTS // te
    rows = PEER_HEADS * N_KEYS
    return pl.pallas_call(
        functools.partial(_peer_mix_kernel, ib=ib),
        grid=(t // tt, ne),
        in_specs=[pl.BlockSpec((tt, D_MODEL), lambda i, e: (i, 0)),
                  pl.BlockSpec((tt, D_MODEL), lambda i, e: (i, 0)),
                  pl.BlockSpec((te, D_MODEL), lambda i, e: (e, 0)),
                  pl.BlockSpec((D_MODEL, te), lambda i, e: (0, e)),
                  pl.BlockSpec((PEER_HEADS, ib, tt), lambda i, e: (0, e, i)),
                  pl.BlockSpec((PEER_HEADS, ib, tt), lambda i, e: (0, e, i)),
                  pl.BlockSpec((rows // BF16_ROWS, BF16_ROWS, tt), lambda i, e: (0, 0, i)),
                  pl.BlockSpec((rows // BF16_ROWS, BF16_ROWS, tt), lambda i, e: (0, 0, i)),
                  pl.BlockSpec((1, D_MODEL), lambda i, e: (0, 0)),
                  pl.BlockSpec((1, D_MODEL), lambda i, e: (0, 0))],
        out_specs=pl.BlockSpec((tt, D_MODEL), lambda i, e: (i, 0)),
        out_shape=jax.ShapeDtypeStruct((t, D_MODEL), F32),
        scratch_shapes=[pltpu.VMEM((D_MODEL, tt), F32)],
        compiler_params=_cparams(("parallel", "arbitrary")),
        name="peer_mix_ln2",
    )(x1b, x1, u, vt, ra, ea, cb, eb, g, bta)


def _block_diag(w, per_tile):
    n, k, _ = w.shape
    w = w.reshape(n // per_tile, per_tile, k, k)
    eye = jnp.eye(per_tile, dtype=w.dtype)
    out = jnp.einsum('tpij,pq->tpiqj', w, eye)
    return out.reshape(n // per_tile, per_tile * k, per_tile * k)


def _layer(x, mem, w_in, b_gate, conv_w, conv_b, lru_wa, lru_ba, lru_wx, lru_bx, lru_lambda,
           w_mem_kv, w_br_attn, w_br_lru, w_br_mem, w_out, ln1_g, ln1_b,
           peer_wq, peer_keys, peer_u, peer_v, ln2_g, ln2_b):
    b, s, d = x.shape
    t = b * s
    xf = x.reshape(t, d)
    xb = xf.astype(BF16)
    w_in_b = w_in.astype(BF16)
    row = lambda a: a.reshape(1, -1).astype(F32)

    off_rnn = 3 * DSWA_WIDTH
    off_mq = off_rnn + 2 * D_RNN
    off_gl = off_mq + MEM_WIDTH
    xr_yg = _matmul(xb, w_in_b, off_rnn, 2 * D_RNN, F32, tn=512).reshape(b, s, 2 * D_RNN)
    mq = _matmul(xb, w_in_b, off_mq, MEM_WIDTH, BF16, tn=512).reshape(b, s, MEM_WIDTH)
    gl = _matmul(xb, w_in_b, off_gl, 3 * D_MODEL, F32, tn=1024)

    os, lses = [], []
    for gi, dil in enumerate(DSWA_DILATIONS):
        xp = xb if dil == 1 else _permute_rows(x, dil).reshape(t, d)
        w_g = jnp.concatenate([w_in_b[:, part * DSWA_WIDTH + gi * GROUP_W:
                                      part * DSWA_WIDTH + (gi + 1) * GROUP_W] for part in range(3)], axis=1)
        qkv = _matmul(xp, w_g, 0, 3 * GROUP_W, BF16, tn=512).reshape(b, dil, s // dil, 3 * GROUP_W)
        o_g, lse_g = _dswa_group(qkv, gi)
        os.append(o_g.reshape(t, GROUP_W))
        lses.append(lse_g.reshape(t, HEAD_DIM))

    per_tile = MXU_TILE // LRU_BW
    wa_bd = _block_diag(lru_wa, per_tile).astype(BF16)
    wx_bd = _block_diag(lru_wx, per_tile).astype(BF16)
    rec = _rglru(xr_yg, conv_w.astype(F32), row(conv_b), wa_bd, row(lru_ba), wx_bd, row(lru_bx),
                 row(lru_lambda)).reshape(t, D_RNN)

    mem_len = mem.shape[1]
    kv = _matmul(mem.reshape(b * mem_len, d).astype(BF16), w_mem_kv.astype(BF16), 0,
                 2 * MEM_WIDTH, BF16, tn=1024).reshape(b, mem_len, 2 * MEM_WIDTH)
    memo = _memattn(mq, kv).reshape(t, MEM_WIDTH)

    x1, x1b = _mix(os, lses, rec, memo, gl, xf, b_gate.astype(F32), w_br_attn.astype(BF16),
                   w_br_lru.astype(BF16), w_br_mem.astype(BF16), w_out.astype(BF16),
                   row(ln1_g), row(ln1_b))

    kbd = _block_diag(peer_keys.reshape(PEER_HEADS * 2, N_KEYS, PEER_KEY_DIM // 2), 2).astype(BF16)
    scores = _peer_scores(x1b, peer_wq.astype(BF16), kbd)

    ra, ea, cb, eb = _peer_select(scores)
    ra = ra.reshape(PEER_HEADS, N_KEYS, t)
    ea = ea.reshape(PEER_HEADS, N_KEYS, t)

    cb = cb.reshape(-1, BF16_ROWS, t)
    eb = eb.reshape(-1, BF16_ROWS, t)
    out = _peer_mix(x1b, x1, peer_u.astype(BF16), peer_v.T.astype(BF16), ra, ea, cb, eb,
                    row(ln2_g), row(ln2_b))
    return out.reshape(b, s, d)


def kernel(x, mem, w_in, b_gate, conv_w, conv_b, lru_wa, lru_ba, lru_wx, lru_bx, lru_lambda, w_mem_kv, w_br_attn, w_br_lru, w_br_mem, w_out, ln1_g, ln1_b, peer_wq, peer_keys, peer_u, peer_v, ln2_g, ln2_b):
    h = x.astype(F32)
    depth = w_in.shape[0]
    for l in range(depth):
        h = _layer(h, mem, w_in[l], b_gate[l], conv_w[l], conv_b[l], lru_wa[l], lru_ba[l],
                   lru_wx[l], lru_bx[l], lru_lambda[l], w_mem_kv[l], w_br_attn[l], w_br_lru[l],
                   w_br_mem[l], w_out[l], ln1_g[l], ln1_b[l], peer_wq[l], peer_keys[l],
                   peer_u[l], peer_v[l], ln2_g[l], ln2_b[l])
    return h.astype(x.dtype)
```

```python
import functools
import math

import jax
import jax.numpy as jnp
from jax import lax
from jax.experimental import pallas as pl
from jax.experimental.pallas import tpu as pltpu

F32 = jnp.float32
BF16 = jnp.bfloat16

D_MODEL = 1024
N_GROUPS = 3
DSWA_DILATIONS = (1, 4, 16)
DSWA_HEADS = 4
HEAD_DIM = 128
DSWA_BLK = 128
DSWA_TILE = 2048
GROUP_W = DSWA_HEADS * HEAD_DIM
DSWA_WIDTH = N_GROUPS * GROUP_W
D_RNN = 1024
LRU_BLOCKS = 16
LRU_BW = D_RNN // LRU_BLOCKS
CONV_W = 4
LRU_C = 8.0
MEM_HEADS = 4
MEM_WIDTH = MEM_HEADS * HEAD_DIM
PEER_HEADS = 8
PEER_KEY_DIM = 256
N_KEYS = 128
N_EXPERTS = N_KEYS * N_KEYS
PEER_TOPK = 16
PEER_CHUNK_KEYS = 4
BF16_ROWS = 16
ALPHA = 2.0 ** 0.25
LN_EPS = 1e-5
NEG_INF = -1e30

LANES = 128
MXU_TILE = 256
VMEM_LIMIT = 56 * 1024 * 1024


def _cparams(sem):
    return pltpu.CompilerParams(dimension_semantics=sem, vmem_limit_bytes=VMEM_LIMIT)


def _dot(a, b):
    return jnp.dot(a, b, preferred_element_type=F32)


def _dot_nt(a, b):
    return lax.dot_general(a, b, (((1,), (1,)), ((), ())), preferred_element_type=F32)


def _gelu_erf(x):
    return 0.5 * x * (1.0 + lax.erf(x * (1.0 / math.sqrt(2.0))))


def _layernorm(h, g, b):
    mu = jnp.mean(h, axis=-1, keepdims=True)
    c = h - mu
    var = jnp.mean(c * c, axis=-1, keepdims=True)
    return c * lax.rsqrt(var + LN_EPS) * g + b


def _matmul_kernel(x_ref, w_ref, o_ref):
    o_ref[...] = _dot(x_ref[...], w_ref[...]).astype(o_ref.dtype)


def _matmul(x, w, col_off, n_cols, out_dtype, tm=1024, tn=1024):
    m, k = x.shape
    tn = min(tn, n_cols)
    tm = min(tm, m)
    assert m % tm == 0 and n_cols % tn == 0 and col_off % tn == 0
    off = col_off // tn
    return pl.pallas_call(
        _matmul_kernel,
        grid=(m // tm, n_cols // tn),
        in_specs=[pl.BlockSpec((tm, k), lambda i, j: (i, 0)),
                  pl.BlockSpec((k, tn), lambda i, j: (0, off + j))],
        out_specs=pl.BlockSpec((tm, tn), lambda i, j: (i, j)),
        out_shape=jax.ShapeDtypeStruct((m, n_cols), out_dtype),
        compiler_params=_cparams(("parallel", "arbitrary")),
        name="matmul",
    )(x, w)


def _permute_kernel(x_ref, o_ref, *, dilation):
    rows = x_ref.shape[1] // dilation
    for r in range(dilation):
        o_ref[0, r] = x_ref[0, pl.ds(r, rows, stride=dilation), :].astype(o_ref.dtype)


def _permute_rows(x, dilation):
    b, s, d = x.shape
    m_len = s // dilation
    return pl.pallas_call(
        functools.partial(_permute_kernel, dilation=dilation),
        grid=(b, s // DSWA_TILE, d // LANES),
        in_specs=[pl.BlockSpec((1, DSWA_TILE, LANES), lambda bi, n, c: (bi, n, c))],
        out_specs=pl.BlockSpec((1, dilation, DSWA_TILE // dilation, LANES),
                               lambda bi, n, c: (bi, 0, n, c)),
        out_shape=jax.ShapeDtypeStruct((b, dilation, m_len, d), BF16),
        compiler_params=_cparams(("parallel", "parallel", "parallel")),
        name=f"permute_d{dilation}",
    )(x)


def _dswa_kernel(q_ref, kc_ref, kp_ref, vc_ref, vp_ref, o_ref, lse_ref, o_scr, lse_scr, *, dilation):
    nq = q_ref.shape[2] // DSWA_BLK
    row = lax.broadcasted_iota(jnp.int32, (DSWA_BLK, DSWA_BLK), 0)
    col = lax.broadcasted_iota(jnp.int32, (DSWA_BLK, DSWA_BLK), 1)
    mask_cur = col <= row
    mask_band = col >= row
    mask_first = col >= row + jnp.where(pl.program_id(1) == 0, DSWA_BLK, 0)
    scale = 1.0 / math.sqrt(HEAD_DIM)
    for r in range(dilation):
        for j in range(nq):
            rows = slice(j * DSWA_BLK, (j + 1) * DSWA_BLK)
            q = q_ref[0, r, rows, :]
            kcur = kc_ref[0, r, rows, :]
            vcur = vc_ref[0, r, rows, :]
            if j == 0:
                kprev = kp_ref[0, r]
                vprev = vp_ref[0, r]
                mask_prev = mask_first
            else:
                prows = slice((j - 1) * DSWA_BLK, j * DSWA_BLK)
                kprev = kc_ref[0, r, prows, :]
                vprev = vc_ref[0, r, prows, :]
                mask_prev = mask_band
            dst = pl.ds(j * DSWA_BLK * dilation + r, DSWA_BLK, stride=dilation)
            lse_tile = jnp.zeros((DSWA_BLK, HEAD_DIM), F32)
            for h in range(DSWA_HEADS):
                hs = slice(h * HEAD_DIM, (h + 1) * HEAD_DIM)
                qh = q[:, hs]
                s_p = jnp.where(mask_prev, _dot_nt(qh, kprev[:, hs]) * scale, NEG_INF)
                s_c = jnp.where(mask_cur, _dot_nt(qh, kcur[:, hs]) * scale, NEG_INF)
                m = jnp.maximum(jnp.max(s_p, axis=-1, keepdims=True),
                                jnp.max(s_c, axis=-1, keepdims=True))
                p_p = jnp.exp(s_p - m)
                p_c = jnp.exp(s_c - m)
                l = jnp.sum(p_p, axis=-1, keepdims=True) + jnp.sum(p_c, axis=-1, keepdims=True)
                o = _dot(p_p.astype(BF16), vprev[:, hs]) + _dot(p_c.astype(BF16), vcur[:, hs])
                o_scr[h, dst, :] = o / l
                lse_tile = jnp.where(col == h, m + jnp.log(l), lse_tile)
            lse_scr[dst, :] = lse_tile
    for h in range(DSWA_HEADS):
        o_ref[0, :, h * HEAD_DIM:(h + 1) * HEAD_DIM] = o_scr[h].astype(o_ref.dtype)
    lse_ref[0] = lse_scr[...]


def _dswa_group(qkv, group):
    b, dilation, m_len, _ = qkv.shape
    s = dilation * m_len
    assert s % DSWA_TILE == 0 and DSWA_TILE % (dilation * DSWA_BLK) == 0
    rows = DSWA_TILE // dilation
    nq = rows // DSWA_BLK

    def cur(c):
        return pl.BlockSpec((1, dilation, rows, GROUP_W), lambda bi, n: (bi, 0, n, c))

    def prev(c):
        return pl.BlockSpec((1, dilation, DSWA_BLK, GROUP_W),
                            lambda bi, n: (bi, 0, jnp.maximum(n * nq - 1, 0), c))

    return pl.pallas_call(
        functools.partial(_dswa_kernel, dilation=dilation),
        grid=(b, s // DSWA_TILE),
        in_specs=[cur(0), cur(1), prev(1), cur(2), prev(2)],
        out_specs=[pl.BlockSpec((1, DSWA_TILE, GROUP_W), lambda bi, n: (bi, n, 0)),
                   pl.BlockSpec((1, DSWA_TILE, HEAD_DIM), lambda bi, n: (bi, n, 0))],
        out_shape=[jax.ShapeDtypeStruct((b, s, GROUP_W), BF16),
                   jax.ShapeDtypeStruct((b, s, HEAD_DIM), F32)],
        scratch_shapes=[pltpu.VMEM((DSWA_HEADS, DSWA_TILE, HEAD_DIM), F32),
                        pltpu.VMEM((DSWA_TILE, HEAD_DIM), F32)],
        compiler_params=_cparams(("parallel", "arbitrary")),
        name=f"dswa_g{group}",
    )(qkv, qkv, qkv, qkv, qkv)


def _shift_rows(x, k, fill):
    rolled = pltpu.roll(x, k, 0)
    row = lax.broadcasted_iota(jnp.int32, x.shape, 0)
    return jnp.where(row >= k, rolled, fill)


def _rglru_kernel(xr_ref, halo_ref, yg_ref, cw_ref, cb_ref, wa_ref, ba_ref, wx_ref, bx_ref,
                  lam_ref, o_ref, h_ref, *, ts):
    si = pl.program_id(1)

    @pl.when(si == 0)
    def _():
        h_ref[...] = jnp.zeros_like(h_ref)

    x = xr_ref[0]
    halo = jnp.where(si > 0, halo_ref[0], 0.0)
    xfull = jnp.concatenate([halo, x], axis=0)
    xc = cb_ref[...] + cw_ref[CONV_W - 1:CONV_W, :] * x
    for k in range(1, CONV_W):
        xc = xc + cw_ref[CONV_W - 1 - k:CONV_W - k, :] * pltpu.roll(xfull, k, 0)[8:]

    xb = xc.astype(BF16)
    nblk = D_RNN // MXU_TILE
    ra = jnp.concatenate(
        [_dot(xb[:, c * MXU_TILE:(c + 1) * MXU_TILE], wa_ref[c]) for c in range(nblk)], axis=1)
    rx = jnp.concatenate(
        [_dot(xb[:, c * MXU_TILE:(c + 1) * MXU_TILE], wx_ref[c]) for c in range(nblk)], axis=1)
    r = jax.nn.sigmoid(ra + ba_ref[...])
    gate_i = jax.nn.sigmoid(rx + bx_ref[...])
    neg_lam = -lam_ref[...]
    softplus = jnp.maximum(neg_lam, 0.0) + jnp.log1p(jnp.exp(-jnp.abs(neg_lam)))
    log_a = (-LRU_C) * r * softplus
    a = jnp.exp(log_a)
    bterm = jnp.sqrt(1.0 - a * a) * gate_i * xc

    step = 1
    while step < ts:
        a_s = _shift_rows(a, step, 1.0)
        b_s = _shift_rows(bterm, step, 0.0)
        bterm = a * b_s + bterm
        a = a * a_s
        step *= 2
    h = a * h_ref[0:1, :] + bterm
    h_ref[0:1, :] = h[ts - 1:ts, :]
    o_ref[0] = (h * jax.nn.gelu(yg_ref[0], approximate=True)).astype(o_ref.dtype)


def _rglru(xr_yg, conv_w, conv_b, wa_bd, ba, wx_bd, bx, lam, ts=256):
    b, s, _ = xr_yg.shape
    assert s % ts == 0
    full = lambda shape: pl.BlockSpec(shape, lambda bi, si: (0,) * len(shape))
    return pl.pallas_call(
        functools.partial(_rglru_kernel, ts=ts),
        grid=(b, s // ts),
        in_specs=[pl.BlockSpec((1, ts, D_RNN), lambda bi, si: (bi, si, 0)),
                  pl.BlockSpec((1, 8, D_RNN),
                               lambda bi, si: (bi, jnp.maximum(si * (ts // 8) - 1, 0), 0)),
                  pl.BlockSpec((1, ts, D_RNN), lambda bi, si: (bi, si, 1)),
                  full((CONV_W, D_RNN)), full((1, D_RNN)),
                  full(wa_bd.shape), full((1, D_RNN)),
                  full(wx_bd.shape), full((1, D_RNN)), full((1, D_RNN))],
        out_specs=pl.BlockSpec((1, ts, D_RNN), lambda bi, si: (bi, si, 0)),
        out_shape=jax.ShapeDtypeStruct((b, s, D_RNN), BF16),
        scratch_shapes=[pltpu.VMEM((8, D_RNN), F32)],
        compiler_params=_cparams(("parallel", "arbitrary")),
        name="rglru",
    )(xr_yg, xr_yg, xr_yg, conv_w, conv_b, wa_bd, ba, wx_bd, bx, lam)


def _memattn_kernel(q_ref, kv_ref, o_ref):
    scale = 1.0 / math.sqrt(HEAD_DIM)
    for h in range(MEM_HEADS):
        hs = slice(h * HEAD_DIM, (h + 1) * HEAD_DIM)
        vs = slice(MEM_WIDTH + h * HEAD_DIM, MEM_WIDTH + (h + 1) * HEAD_DIM)
        s = _dot_nt(q_ref[0, :, hs], kv_ref[0, :, hs]) * scale
        m = jnp.max(s, axis=-1, keepdims=True)
        p = jnp.exp(s - m)
        l = jnp.sum(p, axis=-1, keepdims=True)
        o_ref[0, :, hs] = (_dot(p.astype(BF16), kv_ref[0, :, vs]) / l).astype(o_ref.dtype)


def _memattn(mq, kv, ts=1024):
    b, s, _ = mq.shape
    mem_len = kv.shape[1]
    return pl.pallas_call(
        _memattn_kernel,
        grid=(b, s // ts),
        in_specs=[pl.BlockSpec((1, ts, MEM_WIDTH), lambda bi, si: (bi, si, 0)),
                  pl.BlockSpec((1, mem_len, 2 * MEM_WIDTH), lambda bi, si: (bi, 0, 0))],
        out_specs=pl.BlockSpec((1, ts, MEM_WIDTH), lambda bi, si: (bi, si, 0)),
        out_shape=jax.ShapeDtypeStruct((b, s, MEM_WIDTH), BF16),
        compiler_params=_cparams(("parallel", "arbitrary")),
        name="memattn",
    )(mq, kv)


def _mix_kernel(o0_ref, o1_ref, o2_ref, l0_ref, l1_ref, l2_ref, rec_ref, memo_ref, gl_ref, x_ref,
                bg_ref, wa_ref, wl_ref, wm_ref, wo_ref, g_ref, b_ref, x1_ref, x1b_ref):
    l0, l1, l2 = l0_ref[...], l1_ref[...], l2_ref[...]
    mx = jnp.maximum(jnp.maximum(l0, l1), l2)
    e0, e1, e2 = jnp.exp(l0 - mx), jnp.exp(l1 - mx), jnp.exp(l2 - mx)
    inv = 1.0 / (e0 + e1 + e2)
    w0, w1, w2 = e0 * inv, e1 * inv, e2 * inv
    parts = []
    for h in range(DSWA_HEADS):
        hs = slice(h * HEAD_DIM, (h + 1) * HEAD_DIM)
        parts.append((w0[:, h:h + 1] * o0_ref[:, hs].astype(F32)
                      + w1[:, h:h + 1] * o1_ref[:, hs].astype(F32)
                      + w2[:, h:h + 1] * o2_ref[:, hs].astype(F32)).astype(BF16))
    attn = jnp.concatenate(parts, axis=1)
    d = D_MODEL
    gate = lambda j: jax.nn.sigmoid(gl_ref[:, j * d:(j + 1) * d] + bg_ref[j:j + 1, :])
    merged = (gate(0) * _dot(attn, wa_ref[...])
              + gate(1) * _dot(rec_ref[...], wl_ref[...])
              + gate(2) * _dot(memo_ref[...], wm_ref[...]))
    mix = _dot(merged.astype(BF16), wo_ref[...])
    x1 = _layernorm(ALPHA * x_ref[...] + mix, g_ref[...], b_ref[...])
    x1_ref[...] = x1
    x1b_ref[...] = x1.astype(BF16)


def _mix(os, lses, rec, memo, gl, x, b_gate, wa, wl, wm, wo, g, bta, tt=512):
    t = x.shape[0]
    rows = lambda w: pl.BlockSpec((tt, w), lambda i: (i, 0))
    full = lambda a: pl.BlockSpec(a.shape, lambda i: (0,) * a.ndim)
    return pl.pallas_call(
        _mix_kernel,
        grid=(t // tt,),
        in_specs=[rows(GROUP_W)] * 3 + [rows(HEAD_DIM)] * 3 + [
            rows(D_RNN), rows(MEM_WIDTH), rows(3 * D_MODEL), rows(D_MODEL),
            full(b_gate), full(wa), full(wl), full(wm), full(wo), full(g), full(bta)],
        out_specs=[rows(D_MODEL), rows(D_MODEL)],
        out_shape=[jax.ShapeDtypeStruct((t, D_MODEL), F32), jax.ShapeDtypeStruct((t, D_MODEL), BF16)],
        compiler_params=_cparams(("parallel",)),
        name="mix_ln1",
    )(*os, *lses, rec, memo, gl, x, b_gate, wa, wl, wm, wo, g, bta)


def _peer_scores_kernel(x_ref, wq_ref, kbd_ref, o_ref):
    q = _dot(x_ref[...], wq_ref[...]).astype(BF16)
    for h in range(PEER_HEADS):
        hs = slice(h * PEER_KEY_DIM, (h + 1) * PEER_KEY_DIM)
        o_ref[hs, :] = _dot_nt(kbd_ref[h], q[:, hs])


def _peer_scores(x1b, wq, kbd, tt=512):
    t = x1b.shape[0]
    w = PEER_HEADS * PEER_KEY_DIM
    return pl.pallas_call(
        _peer_scores_kernel,
        grid=(t // tt,),
        in_specs=[pl.BlockSpec((tt, D_MODEL), lambda i: (i, 0)),
                  pl.BlockSpec(wq.shape, lambda i: (0, 0)),
                  pl.BlockSpec(kbd.shape, lambda i: (0, 0, 0))],
        out_specs=pl.BlockSpec((w, tt), lambda i: (0, i)),
        out_shape=jax.ShapeDtypeStruct((w, t), F32),
        compiler_params=_cparams(("parallel",)),
        name="peer_scores",
    )(x1b, wq, kbd)


_PEER_CANDS = tuple((p, q) for p in range(PEER_TOPK) for q in range(PEER_TOPK)
                    if (p + 1) * (q + 1) <= PEER_TOPK)


def _top_positions(s, exact_ties):
    key = lax.broadcasted_iota(jnp.int32, s.shape, 0).astype(F32)
    pos = jnp.full(s.shape, float(PEER_TOPK), F32)
    work = s
    tops = []
    for p in range(PEER_TOPK):
        mx = jnp.max(work, axis=0, keepdims=True)
        if exact_ties:
            first = jnp.min(jnp.where(work == mx, key, float(N_KEYS)), axis=0, keepdims=True)
            hit = key == first
        else:
            hit = work == mx
        pos = jnp.where(hit, float(p), pos)
        work = jnp.where(hit, -jnp.inf, work)
        tops.append(mx)
    return pos, tops


def _peer_select_body(sc_ref, ra_ref, ea_ref, cb_ref, eb_ref, exact_ties):
    tt = sc_ref.shape[1]
    kg = N_KEYS // BF16_ROWS
    pos_b, tops_a, tops_b = [], [], []
    marked = jnp.zeros((1, tt), F32)
    for h in range(PEER_HEADS):
        base = h * PEER_KEY_DIM
        pa, ta = _top_positions(sc_ref[base:base + N_KEYS, :], exact_ties)
        pb, tb = _top_positions(sc_ref[base + N_KEYS:base + 2 * N_KEYS, :], exact_ties)
        ra_ref[h * N_KEYS:(h + 1) * N_KEYS, :] = pa
        pos_b.append(pb)
        tops_a.append(ta)
        tops_b.append(tb)
        if not exact_ties:
            for pos in (pa, pb):
                n_marked = jnp.sum(jnp.where(pos < float(PEER_TOPK), 1.0, 0.0), axis=0, keepdims=True)
                marked = jnp.maximum(marked, n_marked)
    a_top = [jnp.concatenate([tops_a[h][p] for h in range(PEER_HEADS)], axis=0)
             for p in range(PEER_TOPK)]
    b_top = [jnp.concatenate([tops_b[h][q] for h in range(PEER_HEADS)], axis=0)
             for q in range(PEER_TOPK)]
    sums = [a_top[p] + b_top[q] for p, q in _PEER_CANDS]
    n = len(sums)
    rank = [jnp.zeros((PEER_HEADS, tt), F32) for _ in range(n)]
    for c in range(n):
        for c2 in range(c):
            ge = sums[c2] >= sums[c]
            rank[c] = rank[c] + jnp.where(ge, 1.0, 0.0)
            rank[c2] = rank[c2] + jnp.where(ge, 0.0, 1.0)
    cnt = [jnp.zeros((PEER_HEADS, tt), F32) for _ in range(PEER_TOPK)]
    zsum = jnp.zeros((PEER_HEADS, tt), F32)
    for c, (p, q) in enumerate(_PEER_CANDS):
        sel = rank[c] < float(PEER_TOPK)
        cnt[q] = cnt[q] + jnp.where(sel, 1.0, 0.0)
        zsum = zsum + jnp.where(sel, jnp.exp(sums[c] - sums[0]), 0.0)
    inv_z = 1.0 / zsum
    for h in range(PEER_HEADS):
        base = h * PEER_KEY_DIM
        rows = slice(h * N_KEYS, (h + 1) * N_KEYS)
        posb = pos_b[h].astype(BF16).reshape(kg, BF16_ROWS, tt)
        cb = jnp.zeros((kg, BF16_ROWS, tt), BF16)
        for q in range(PEER_TOPK):
            cnt_q = jnp.broadcast_to(cnt[q][h:h + 1, :], (BF16_ROWS, tt)).astype(BF16)
            cb = cb + jnp.where(posb == float(q), cnt_q[None], jnp.zeros_like(cb))
        cb_ref[h * kg:(h + 1) * kg] = cb
        ea_ref[rows, :] = (jnp.exp(sc_ref[base:base + N_KEYS, :] - a_top[0][h:h + 1, :])
                           * inv_z[h:h + 1, :])
        eb = jnp.exp(sc_ref[base + N_KEYS:base + 2 * N_KEYS, :] - b_top[0][h:h + 1, :])
        eb_ref[h * kg:(h + 1) * kg] = eb.astype(BF16).reshape(kg, BF16_ROWS, tt)
    return marked


def _peer_select_kernel(sc_ref, ra_ref, ea_ref, cb_ref, eb_ref):
    marked = _peer_select_body(sc_ref, ra_ref, ea_ref, cb_ref, eb_ref, exact_ties=False)

    @pl.when(jnp.max(marked) > float(PEER_TOPK))
    def _():
        _peer_select_body(sc_ref, ra_ref, ea_ref, cb_ref, eb_ref, exact_ties=True)


def _peer_select(sc, tt=256):
    w, t = sc.shape
    rows = PEER_HEADS * N_KEYS
    out_spec = pl.BlockSpec((rows, tt), lambda i: (0, i))
    out_spec_b = pl.BlockSpec((rows // BF16_ROWS, BF16_ROWS, tt), lambda i: (0, 0, i))
    shape_b = jax.ShapeDtypeStruct((rows // BF16_ROWS, BF16_ROWS, t), BF16)
    return pl.pallas_call(
        _peer_select_kernel,
        grid=(t // tt,),
        in_specs=[pl.BlockSpec((w, tt), lambda i: (0, i))],
        out_specs=[out_spec, out_spec, out_spec_b, out_spec_b],
        out_shape=[jax.ShapeDtypeStruct((rows, t), F32), jax.ShapeDtypeStruct((rows, t), F32),
                   shape_b, shape_b],
        compiler_params=_cparams(("parallel",)),
        name="peer_select",
    )(sc)


def _peer_mix_kernel(x1b_ref, x1_ref, u_ref, vt_ref, ra_ref, ea_ref, cb_ref, eb_ref,
                     g_ref, b_ref, o_ref, acc_ref, *, ib):
    e = pl.program_id(1)
    tt = x1b_ref.shape[0]

    @pl.when(e == 0)
    def _():
        acc_ref[...] = jnp.zeros_like(acc_ref)

    x1b = x1b_ref[...]
    acc = acc_ref[...]
    kg = N_KEYS // BF16_ROWS
    n_chunks = ib // PEER_CHUNK_KEYS
    chunk_rows = lambda c: slice(c * PEER_CHUNK_KEYS * N_KEYS, (c + 1) * PEER_CHUNK_KEYS * N_KEYS)
    h_next = _dot_nt(u_ref[chunk_rows(0), :], x1b)
    for c in range(n_chunks):
        rows = chunk_rows(c)
        hc = h_next
        if c + 1 < n_chunks:
            h_next = _dot_nt(u_ref[chunk_rows(c + 1), :], x1b)
        ys = []
        for k in range(PEER_CHUNK_KEYS):
            ii = c * PEER_CHUNK_KEYS + k
            w = jnp.zeros((kg, BF16_ROWS, tt), BF16)
            for h in range(PEER_HEADS):
                hs = slice(h * kg, (h + 1) * kg)
                ra = jnp.broadcast_to(ra_ref[h, ii:ii + 1, :], (BF16_ROWS, tt)).astype(BF16)
                ea = jnp.broadcast_to(ea_ref[h, ii:ii + 1, :], (BF16_ROWS, tt)).astype(BF16)
                w = w + jnp.where(cb_ref[hs] > ra[None], eb_ref[hs] * ea[None], jnp.zeros_like(w))
            act = _gelu_erf(hc[k * N_KEYS:(k + 1) * N_KEYS, :]).astype(BF16)
            ys.append(w.reshape(N_KEYS, tt) * act)
        acc = acc + _dot(vt_ref[:, rows], jnp.concatenate(ys, axis=0))
    acc_ref[...] = acc

    @pl.when(e == pl.num_programs(1) - 1)
    def _():
        o_ref[...] = _layernorm(ALPHA * x1_ref[...] + acc_ref[...].T, g_ref[...], b_ref[...])


def _peer_mix(x1b, x1, u, vt, ra, ea, cb, eb, g, bta, tt=512, te=2048):
    t = x1.shape[0]
    ib = te // N_KEYS
    ne = N_EXPERTS // te
    rows = PEER_HEADS * N_KEYS
    return pl.pallas_call(
        functools.partial(_peer_mix_kernel, ib=ib),
        grid=(t // tt, ne),
        in_specs=[pl.BlockSpec((tt, D_MODEL), lambda i, e: (i, 0)),
                  pl.BlockSpec((tt, D_MODEL), lambda i, e: (i, 0)),
                  pl.BlockSpec((te, D_MODEL), lambda i, e: (e, 0)),
                  pl.BlockSpec((D_MODEL, te), lambda i, e: (0, e)),
                  pl.BlockSpec((PEER_HEADS, ib, tt), lambda i, e: (0, e, i)),
                  pl.BlockSpec((PEER_HEADS, ib, tt), lambda i, e: (0, e, i)),
                  pl.BlockSpec((rows // BF16_ROWS, BF16_ROWS, tt), lambda i, e: (0, 0, i)),
                  pl.BlockSpec((rows // BF16_ROWS, BF16_ROWS, tt), lambda i, e: (0, 0, i)),
                  pl.BlockSpec((1, D_MODEL), lambda i, e: (0, 0)),
                  pl.BlockSpec((1, D_MODEL), lambda i, e: (0, 0))],
        out_specs=pl.BlockSpec((tt, D_MODEL), lambda i, e: (i, 0)),
        out_shape=jax.ShapeDtypeStruct((t, D_MODEL), F32),
        scratch_shapes=[pltpu.VMEM((D_MODEL, tt), F32)],
        compiler_params=_cparams(("parallel", "arbitrary")),
        name="peer_mix_ln2",
    )(x1b, x1, u, vt, ra, ea, cb, eb, g, bta)


def _block_diag(w, per_tile):
    n, k, _ = w.shape
    w = w.reshape(n // per_tile, per_tile, k, k)
    eye = jnp.eye(per_tile, dtype=w.dtype)
    out = jnp.einsum('tpij,pq->tpiqj', w, eye)
    return out.reshape(n // per_tile, per_tile * k, per_tile * k)


def _layer(x, mem, w_in, b_gate, conv_w, conv_b, lru_wa, lru_ba, lru_wx, lru_bx, lru_lambda,
           w_mem_kv, w_br_attn, w_br_lru, w_br_mem, w_out, ln1_g, ln1_b,
           peer_wq, peer_keys, peer_u, peer_v, ln2_g, ln2_b):
    b, s, d = x.shape
    t = b * s
    xf = x.reshape(t, d)
    xb = xf.astype(BF16)
    w_in_b = w_in.astype(BF16)
    row = lambda a: a.reshape(1, -1).astype(F32)

    off_rnn = 3 * DSWA_WIDTH
    off_mq = off_rnn + 2 * D_RNN
    off_gl = off_mq + MEM_WIDTH
    xr_yg = _matmul(xb, w_in_b, off_rnn, 2 * D_RNN, F32, tn=512).reshape(b, s, 2 * D_RNN)
    mq = _matmul(xb, w_in_b, off_mq, MEM_WIDTH, BF16, tn=512).reshape(b, s, MEM_WIDTH)
    gl = _matmul(xb, w_in_b, off_gl, 3 * D_MODEL, F32, tn=1024)

    os, lses = [], []
    for gi, dil in enumerate(DSWA_DILATIONS):
        xp = xb if dil == 1 else _permute_rows(x, dil).reshape(t, d)
        w_g = jnp.concatenate([w_in_b[:, part * DSWA_WIDTH + gi * GROUP_W:
                                      part * DSWA_WIDTH + (gi + 1) * GROUP_W] for part in range(3)], axis=1)
        qkv = _matmul(xp, w_g, 0, 3 * GROUP_W, BF16, tn=512).reshape(b, dil, s // dil, 3 * GROUP_W)
        o_g, lse_g = _dswa_group(qkv, gi)
        os.append(o_g.reshape(t, GROUP_W))
        lses.append(lse_g.reshape(t, HEAD_DIM))

    per_tile = MXU_TILE // LRU_BW
    wa_bd = _block_diag(lru_wa, per_tile).astype(BF16)
    wx_bd = _block_diag(lru_wx, per_tile).astype(BF16)
    rec = _rglru(xr_yg, conv_w.astype(F32), row(conv_b), wa_bd, row(lru_ba), wx_bd, row(lru_bx),
                 row(lru_lambda)).reshape(t, D_RNN)

    mem_len = mem.shape[1]
    kv = _matmul(mem.reshape(b * mem_len, d).astype(BF16), w_mem_kv.astype(BF16), 0,
                 2 * MEM_WIDTH, BF16, tn=1024).reshape(b, mem_len, 2 * MEM_WIDTH)
    memo = _memattn(mq, kv).reshape(t, MEM_WIDTH)

    x1, x1b = _mix(os, lses, rec, memo, gl, xf, b_gate.astype(F32), w_br_attn.astype(BF16),
                   w_br_lru.astype(BF16), w_br_mem.astype(BF16), w_out.astype(BF16),
                   row(ln1_g), row(ln1_b))

    kbd = _block_diag(peer_keys.reshape(PEER_HEADS * 2, N_KEYS, PEER_KEY_DIM // 2), 2).astype(BF16)
    scores = _peer_scores(x1b, peer_wq.astype(BF16), kbd)

    ra, ea, cb, eb = _peer_select(scores)
    ra = ra.reshape(PEER_HEADS, N_KEYS, t)
    ea = ea.reshape(PEER_HEADS, N_KEYS, t)

    out = _peer_mix(x1b, x1, peer_u.astype(BF16), peer_v.T.astype(BF16), ra, ea, cb, eb,
                    row(ln2_g), row(ln2_b))
    return out.reshape(b, s, d)


def kernel(x, mem, w_in, b_gate, conv_w, conv_b, lru_wa, lru_ba, lru_wx, lru_bx, lru_lambda, w_mem_kv, w_br_attn, w_br_lru, w_br_mem, w_out, ln1_g, ln1_b, peer_wq, peer_keys, peer_u, peer_v, ln2_g, ln2_b):
    h = x.astype(F32)
    depth = w_in.shape[0]
    for l in range(depth):
        h = _layer(h, mem, w_in[l], b_gate[l], conv_w[l], conv_b[l], lru_wa[l], lru_ba[l],
                   lru_wx[l], lru_bx[l], lru_lambda[l], w_mem_kv[l], w_br_attn[l], w_br_lru[l],
                   w_br_mem[l], w_out[l], ln1_g[l], ln1_b[l], peer_wq[l], peer_keys[l],
                   peer_u[l], peer_v[l], ln2_g[l], ln2_b[l])
    return h.astype(x.dtype)
```

```python
import functools
import math

import jax
import jax.numpy as jnp
from jax import lax
from jax.experimental import pallas as pl
from jax.experimental.pallas import tpu as pltpu

F32 = jnp.float32
BF16 = jnp.bfloat16

D_MODEL = 1024
N_GROUPS = 3
DSWA_DILATIONS = (1, 4, 16)
DSWA_HEADS = 4
HEAD_DIM = 128
DSWA_BLK = 128
DSWA_TILE = 2048
GROUP_W = DSWA_HEADS * HEAD_DIM
DSWA_WIDTH = N_GROUPS * GROUP_W
D_RNN = 1024
LRU_BLOCKS = 16
LRU_BW = D_RNN // LRU_BLOCKS
CONV_W = 4
LRU_C = 8.0
MEM_HEADS = 4
MEM_WIDTH = MEM_HEADS * HEAD_DIM
PEER_HEADS = 8
PEER_KEY_DIM = 256
N_KEYS = 128
N_EXPERTS = N_KEYS * N_KEYS
PEER_TOPK = 16
PEER_CHUNK_KEYS = 4
BF16_ROWS = 16
ALPHA = 2.0 ** 0.25
LN_EPS = 1e-5
NEG_INF = -1e30

LANES = 128
SUBLANES = 8
MXU_TILE = 256
VMEM_LIMIT = 56 * 1024 * 1024


def _cparams(sem):
    return pltpu.CompilerParams(dimension_semantics=sem, vmem_limit_bytes=VMEM_LIMIT)


def _dot(a, b):
    return jnp.dot(a, b, preferred_element_type=F32)


def _dot_nt(a, b):
    return lax.dot_general(a, b, (((1,), (1,)), ((), ())), preferred_element_type=F32)


def _gelu_erf(x):
    return 0.5 * x * (1.0 + lax.erf(x * (1.0 / math.sqrt(2.0))))


def _layernorm(h, g, b):
    mu = jnp.mean(h, axis=-1, keepdims=True)
    c = h - mu
    var = jnp.mean(c * c, axis=-1, keepdims=True)
    return c * lax.rsqrt(var + LN_EPS) * g + b


def _matmul_kernel(x_ref, w_ref, o_ref):
    o_ref[...] = _dot(x_ref[...], w_ref[...]).astype(o_ref.dtype)


def _matmul(x, w, col_off, n_cols, out_dtype, tm=1024, tn=1024):
    m, k = x.shape
    tn = min(tn, n_cols)
    tm = min(tm, m)
    assert m % tm == 0 and n_cols % tn == 0 and col_off % tn == 0
    off = col_off // tn
    return pl.pallas_call(
        _matmul_kernel,
        grid=(m // tm, n_cols // tn),
        in_specs=[pl.BlockSpec((tm, k), lambda i, j: (i, 0)),
                  pl.BlockSpec((k, tn), lambda i, j: (0, off + j))],
        out_specs=pl.BlockSpec((tm, tn), lambda i, j: (i, j)),
        out_shape=jax.ShapeDtypeStruct((m, n_cols), out_dtype),
        compiler_params=_cparams(("parallel", "arbitrary")),
        name="matmul",
    )(x, w)


def _permute_kernel(x_ref, o_ref, *, dilation):
    rows = x_ref.shape[1] // dilation
    for r in range(dilation):
        o_ref[0, r] = x_ref[0, pl.ds(r, rows, stride=dilation), :].astype(o_ref.dtype)


def _permute_rows(x, dilation):
    b, s, d = x.shape
    m_len = s // dilation
    return pl.pallas_call(
        functools.partial(_permute_kernel, dilation=dilation),
        grid=(b, s // DSWA_TILE, d // LANES),
        in_specs=[pl.BlockSpec((1, DSWA_TILE, LANES), lambda bi, n, c: (bi, n, c))],
        out_specs=pl.BlockSpec((1, dilation, DSWA_TILE // dilation, LANES),
                               lambda bi, n, c: (bi, 0, n, c)),
        out_shape=jax.ShapeDtypeStruct((b, dilation, m_len, d), BF16),
        compiler_params=_cparams(("parallel", "parallel", "parallel")),
        name=f"permute_d{dilation}",
    )(x)


def _dswa_kernel(q_ref, kc_ref, kp_ref, vc_ref, vp_ref, o_ref, lse_ref, o_scr, lse_scr, *, dilation):
    nq = q_ref.shape[2] // DSWA_BLK
    row = lax.broadcasted_iota(jnp.int32, (DSWA_BLK, DSWA_BLK), 0)
    col = lax.broadcasted_iota(jnp.int32, (DSWA_BLK, DSWA_BLK), 1)
    mask_cur = col <= row
    mask_band = col >= row
    mask_first = col >= row + jnp.where(pl.program_id(1) == 0, DSWA_BLK, 0)
    scale = 1.0 / math.sqrt(HEAD_DIM)
    for r in range(dilation):
        for j in range(nq):
            rows = slice(j * DSWA_BLK, (j + 1) * DSWA_BLK)
            q = q_ref[0, r, rows, :]
            kcur = kc_ref[0, r, rows, :]
            vcur = vc_ref[0, r, rows, :]
            if j == 0:
                kprev = kp_ref[0, r]
                vprev = vp_ref[0, r]
                mask_prev = mask_first
            else:
                prows = slice((j - 1) * DSWA_BLK, j * DSWA_BLK)
                kprev = kc_ref[0, r, prows, :]
                vprev = vc_ref[0, r, prows, :]
                mask_prev = mask_band
            dst = pl.ds(j * DSWA_BLK * dilation + r, DSWA_BLK, stride=dilation)
            lse_tile = jnp.zeros((DSWA_BLK, HEAD_DIM), F32)
            for h in range(DSWA_HEADS):
                hs = slice(h * HEAD_DIM, (h + 1) * HEAD_DIM)
                qh = q[:, hs]
                s_p = jnp.where(mask_prev, _dot_nt(qh, kprev[:, hs]) * scale, NEG_INF)
                s_c = jnp.where(mask_cur, _dot_nt(qh, kcur[:, hs]) * scale, NEG_INF)
                m = jnp.maximum(jnp.max(s_p, axis=-1, keepdims=True),
                                jnp.max(s_c, axis=-1, keepdims=True))
                p_p = jnp.exp(s_p - m)
                p_c = jnp.exp(s_c - m)
                l = jnp.sum(p_p, axis=-1, keepdims=True) + jnp.sum(p_c, axis=-1, keepdims=True)
                o = _dot(p_p.astype(BF16), vprev[:, hs]) + _dot(p_c.astype(BF16), vcur[:, hs])
                o_scr[h, dst, :] = o / l
                lse_tile = jnp.where(col == h, m + jnp.log(l), lse_tile)
            lse_scr[dst, :] = lse_tile
    for h in range(DSWA_HEADS):
        o_ref[0, :, h * HEAD_DIM:(h + 1) * HEAD_DIM] = o_scr[h].astype(o_ref.dtype)
    lse_ref[0] = lse_scr[...]


def _dswa_group(qkv, group):
    b, dilation, m_len, _ = qkv.shape
    s = dilation * m_len
    assert s % DSWA_TILE == 0 and DSWA_TILE % (dilation * DSWA_BLK) == 0
    rows = DSWA_TILE // dilation
    nq = rows // DSWA_BLK

    def cur(c):
        return pl.BlockSpec((1, dilation, rows, GROUP_W), lambda bi, n: (bi, 0, n, c))

    def prev(c):
        return pl.BlockSpec((1, dilation, DSWA_BLK, GROUP_W),
                            lambda bi, n: (bi, 0, jnp.maximum(n * nq - 1, 0), c))

    return pl.pallas_call(
        functools.partial(_dswa_kernel, dilation=dilation),
        grid=(b, s // DSWA_TILE),
        in_specs=[cur(0), cur(1), prev(1), cur(2), prev(2)],
        out_specs=[pl.BlockSpec((1, DSWA_TILE, GROUP_W), lambda bi, n: (bi, n, 0)),
                   pl.BlockSpec((1, DSWA_TILE, HEAD_DIM), lambda bi, n: (bi, n, 0))],
        out_shape=[jax.ShapeDtypeStruct((b, s, GROUP_W), BF16),
                   jax.ShapeDtypeStruct((b, s, HEAD_DIM), F32)],
        scratch_shapes=[pltpu.VMEM((DSWA_HEADS, DSWA_TILE, HEAD_DIM), F32),
                        pltpu.VMEM((DSWA_TILE, HEAD_DIM), F32)],
        compiler_params=_cparams(("parallel", "arbitrary")),
        name=f"dswa_g{group}",
    )(qkv, qkv, qkv, qkv, qkv)


def _linear_scan(a, b, h0):
    ts, d = a.shape
    a = a.reshape(ts // SUBLANES, SUBLANES, d)
    b = b.reshape(ts // SUBLANES, SUBLANES, d)
    row = lax.broadcasted_iota(jnp.int32, a.shape, 1)
    step = 1
    while step < SUBLANES:
        keep = row >= step
        a_s = jnp.where(keep, pltpu.roll(a, step, 1), 1.0)
        b_s = jnp.where(keep, pltpu.roll(b, step, 1), 0.0)
        b = a * b_s + b
        a = a * a_s
        step *= 2
    hs, carry = [], h0
    for g in range(ts // SUBLANES):
        hg = a[g] * carry + b[g]
        hs.append(hg)
        carry = hg[SUBLANES - 1:SUBLANES, :]
    return jnp.concatenate(hs, axis=0), carry


def _rglru_kernel(xb_ref, halo_ref, w_ref, cw_ref, cb_ref, wa_ref, ba_ref, wx_ref, bx_ref,
                  lam_ref, o_ref, h_ref, *, ts):
    si = pl.program_id(1)

    @pl.when(si == 0)
    def _():
        h_ref[...] = jnp.zeros_like(h_ref)

    z = _dot(xb_ref[0], w_ref[...])
    x = z[:, :D_RNN]
    yg = z[:, D_RNN:]
    halo = _dot(halo_ref[0], w_ref[:, :D_RNN])[BF16_ROWS - 8:, :]
    halo = jnp.where(si > 0, halo, 0.0)
    xfull = jnp.concatenate([halo, x], axis=0)
    xc = cb_ref[...] + cw_ref[CONV_W - 1:CONV_W, :] * x
    for k in range(1, CONV_W):
        xc = xc + cw_ref[CONV_W - 1 - k:CONV_W - k, :] * pltpu.roll(xfull, k, 0)[8:]

    xb = xc.astype(BF16)
    nblk = D_RNN // MXU_TILE
    ra = jnp.concatenate(
        [_dot(xb[:, c * MXU_TILE:(c + 1) * MXU_TILE], wa_ref[c]) for c in range(nblk)], axis=1)
    rx = jnp.concatenate(
        [_dot(xb[:, c * MXU_TILE:(c + 1) * MXU_TILE], wx_ref[c]) for c in range(nblk)], axis=1)
    r = jax.nn.sigmoid(ra + ba_ref[...])
    gate_i = jax.nn.sigmoid(rx + bx_ref[...])
    neg_lam = -lam_ref[...]
    softplus = jnp.maximum(neg_lam, 0.0) + jnp.log1p(jnp.exp(-jnp.abs(neg_lam)))
    log_a = (-LRU_C) * r * softplus
    a = jnp.exp(log_a)
    bterm = jnp.sqrt(1.0 - a * a) * gate_i * xc

    h, carry = _linear_scan(a, bterm, h_ref[0:1, :])
    h_ref[0:1, :] = carry
    o_ref[0] = (h * jax.nn.gelu(yg, approximate=True)).astype(o_ref.dtype)


def _rglru(xb, w_rnn, conv_w, conv_b, wa_bd, ba, wx_bd, bx, lam, ts=256):
    b, s, d = xb.shape
    assert s % ts == 0
    full = lambda shape: pl.BlockSpec(shape, lambda bi, si: (0,) * len(shape))
    return pl.pallas_call(
        functools.partial(_rglru_kernel, ts=ts),
        grid=(b, s // ts),
        in_specs=[pl.BlockSpec((1, ts, d), lambda bi, si: (bi, si, 0)),
                  pl.BlockSpec((1, BF16_ROWS, d),
                               lambda bi, si: (bi, jnp.maximum(si * (ts // BF16_ROWS) - 1, 0), 0)),
                  full(w_rnn.shape),
                  full((CONV_W, D_RNN)), full((1, D_RNN)),
                  full(wa_bd.shape), full((1, D_RNN)),
                  full(wx_bd.shape), full((1, D_RNN)), full((1, D_RNN))],
        out_specs=pl.BlockSpec((1, ts, D_RNN), lambda bi, si: (bi, si, 0)),
        out_shape=jax.ShapeDtypeStruct((b, s, D_RNN), BF16),
        scratch_shapes=[pltpu.VMEM((8, D_RNN), F32)],
        compiler_params=_cparams(("parallel", "arbitrary")),
        name="rglru",
    )(xb, xb, w_rnn, conv_w, conv_b, wa_bd, ba, wx_bd, bx, lam)


def _memattn_kernel(q_ref, kv_ref, o_ref):
    scale = 1.0 / math.sqrt(HEAD_DIM)
    for h in range(MEM_HEADS):
        hs = slice(h * HEAD_DIM, (h + 1) * HEAD_DIM)
        vs = slice(MEM_WIDTH + h * HEAD_DIM, MEM_WIDTH + (h + 1) * HEAD_DIM)
        s = _dot_nt(q_ref[0, :, hs], kv_ref[0, :, hs]) * scale
        m = jnp.max(s, axis=-1, keepdims=True)
        p = jnp.exp(s - m)
        l = jnp.sum(p, axis=-1, keepdims=True)
        o_ref[0, :, hs] = (_dot(p.astype(BF16), kv_ref[0, :, vs]) / l).astype(o_ref.dtype)


def _memattn(mq, kv, ts=1024):
    b, s, _ = mq.shape
    mem_len = kv.shape[1]
    return pl.pallas_call(
        _memattn_kernel,
        grid=(b, s // ts),
        in_specs=[pl.BlockSpec((1, ts, MEM_WIDTH), lambda bi, si: (bi, si, 0)),
                  pl.BlockSpec((1, mem_len, 2 * MEM_WIDTH), lambda bi, si: (bi, 0, 0))],
        out_specs=pl.BlockSpec((1, ts, MEM_WIDTH), lambda bi, si: (bi, si, 0)),
        out_shape=jax.ShapeDtypeStruct((b, s, MEM_WIDTH), BF16),
        compiler_params=_cparams(("parallel", "arbitrary")),
        name="memattn",
    )(mq, kv)


def _mix_kernel(o0_ref, o1_ref, o2_ref, l0_ref, l1_ref, l2_ref, rec_ref, memo_ref, x_ref,
                wg_ref, bg_ref, wa_ref, wl_ref, wm_ref, wo_ref, g_ref, b_ref, x1_ref, x1b_ref):
    l0, l1, l2 = l0_ref[...], l1_ref[...], l2_ref[...]
    mx = jnp.maximum(jnp.maximum(l0, l1), l2)
    e0, e1, e2 = jnp.exp(l0 - mx), jnp.exp(l1 - mx), jnp.exp(l2 - mx)
    inv = 1.0 / (e0 + e1 + e2)
    w0, w1, w2 = e0 * inv, e1 * inv, e2 * inv
    parts = []
    for h in range(DSWA_HEADS):
        hs = slice(h * HEAD_DIM, (h + 1) * HEAD_DIM)
        parts.append((w0[:, h:h + 1] * o0_ref[:, hs].astype(F32)
                      + w1[:, h:h + 1] * o1_ref[:, hs].astype(F32)
                      + w2[:, h:h + 1] * o2_ref[:, hs].astype(F32)).astype(BF16))
    attn = jnp.concatenate(parts, axis=1)
    d = D_MODEL
    gl = _dot(x_ref[...].astype(BF16), wg_ref[...])
    gate = lambda j: jax.nn.sigmoid(gl[:, j * d:(j + 1) * d] + bg_ref[j:j + 1, :])
    merged = (gate(0) * _dot(attn, wa_ref[...])
              + gate(1) * _dot(rec_ref[...], wl_ref[...])
              + gate(2) * _dot(memo_ref[...], wm_ref[...]))
    mix = _dot(merged.astype(BF16), wo_ref[...])
    x1 = _layernorm(ALPHA * x_ref[...] + mix, g_ref[...], b_ref[...])
    x1_ref[...] = x1
    x1b_ref[...] = x1.astype(BF16)


def _mix(os, lses, rec, memo, x, w_gate, b_gate, wa, wl, wm, wo, g, bta, tt=512):
    t = x.shape[0]
    rows = lambda w: pl.BlockSpec((tt, w), lambda i: (i, 0))
    full = lambda a: pl.BlockSpec(a.shape, lambda i: (0,) * a.ndim)
    return pl.pallas_call(
        _mix_kernel,
        grid=(t // tt,),
        in_specs=[rows(GROUP_W)] * 3 + [rows(HEAD_DIM)] * 3 + [
            rows(D_RNN), rows(MEM_WIDTH), rows(D_MODEL), full(w_gate),
            full(b_gate), full(wa), full(wl), full(wm), full(wo), full(g), full(bta)],
        out_specs=[rows(D_MODEL), rows(D_MODEL)],
        out_shape=[jax.ShapeDtypeStruct((t, D_MODEL), F32), jax.ShapeDtypeStruct((t, D_MODEL), BF16)],
        compiler_params=_cparams(("parallel",)),
        name="mix_ln1",
    )(*os, *lses, rec, memo, x, w_gate, b_gate, wa, wl, wm, wo, g, bta)


def _peer_scores_kernel(x_ref, wq_ref, kbd_ref, o_ref):
    q = _dot(x_ref[...], wq_ref[...]).astype(BF16)
    for h in range(PEER_HEADS):
        hs = slice(h * PEER_KEY_DIM, (h + 1) * PEER_KEY_DIM)
        o_ref[hs, :] = _dot_nt(kbd_ref[h], q[:, hs])


def _peer_scores(x1b, wq, kbd, tt=512):
    t = x1b.shape[0]
    w = PEER_HEADS * PEER_KEY_DIM
    return pl.pallas_call(
        _peer_scores_kernel,
        grid=(t // tt,),
        in_specs=[pl.BlockSpec((tt, D_MODEL), lambda i: (i, 0)),
                  pl.BlockSpec(wq.shape, lambda i: (0, 0)),
                  pl.BlockSpec(kbd.shape, lambda i: (0, 0, 0))],
        out_specs=pl.BlockSpec((w, tt), lambda i: (0, i)),
        out_shape=jax.ShapeDtypeStruct((w, t), F32),
        compiler_params=_cparams(("parallel",)),
        name="peer_scores",
    )(x1b, wq, kbd)


_PEER_CANDS = tuple((p, q) for p in range(PEER_TOPK) for q in range(PEER_TOPK)
                    if (p + 1) * (q + 1) <= PEER_TOPK)


def _top_positions(s, exact_ties):
    key = lax.broadcasted_iota(jnp.int32, s.shape, 0).astype(F32)
    pos = jnp.full(s.shape, float(PEER_TOPK), F32)
    work = s
    tops = []
    for p in range(PEER_TOPK):
        mx = jnp.max(work, axis=0, keepdims=True)
        if exact_ties:
            first = jnp.min(jnp.where(work == mx, key, float(N_KEYS)), axis=0, keepdims=True)
            hit = key == first
        else:
            hit = work == mx
        pos = jnp.where(hit, float(p), pos)
        work = jnp.where(hit, -jnp.inf, work)
        tops.append(mx)
    return pos, tops


def _store_lane_tiles(ref, h, val):
    for c in range(ref.shape[0]):
        ref[c, h * N_KEYS:(h + 1) * N_KEYS, :] = val[:, c * LANES:(c + 1) * LANES]


def _peer_select_body(sc_ref, ra_ref, ea_ref, cb_ref, eb_ref, exact_ties):
    tt = sc_ref.shape[1]
    kg = N_KEYS // BF16_ROWS
    pos_b, tops_a, tops_b = [], [], []
    marked = jnp.zeros((1, tt), F32)
    for h in range(PEER_HEADS):
        base = h * PEER_KEY_DIM
        pa, ta = _top_positions(sc_ref[base:base + N_KEYS, :], exact_ties)
        pb, tb = _top_positions(sc_ref[base + N_KEYS:base + 2 * N_KEYS, :], exact_ties)
        _store_lane_tiles(ra_ref, h, pa)
        pos_b.append(pb)
        tops_a.append(ta)
        tops_b.append(tb)
        if not exact_ties:
            for pos in (pa, pb):
                n_marked = jnp.sum(jnp.where(pos < float(PEER_TOPK), 1.0, 0.0), axis=0, keepdims=True)
                marked = jnp.maximum(marked, n_marked)
    a_top = [jnp.concatenate([tops_a[h][p] for h in range(PEER_HEADS)], axis=0)
             for p in range(PEER_TOPK)]
    b_top = [jnp.concatenate([tops_b[h][q] for h in range(PEER_HEADS)], axis=0)
             for q in range(PEER_TOPK)]
    sums = [a_top[p] + b_top[q] for p, q in _PEER_CANDS]
    n = len(sums)
    rank = [jnp.zeros((PEER_HEADS, tt), F32) for _ in range(n)]
    for c in range(n):
        for c2 in range(c):
            ge = sums[c2] >= sums[c]
            rank[c] = rank[c] + jnp.where(ge, 1.0, 0.0)
            rank[c2] = rank[c2] + jnp.where(ge, 0.0, 1.0)
    cnt = [jnp.zeros((PEER_HEADS, tt), F32) for _ in range(PEER_TOPK)]
    zsum = jnp.zeros((PEER_HEADS, tt), F32)
    for c, (p, q) in enumerate(_PEER_CANDS):
        sel = rank[c] < float(PEER_TOPK)
        cnt[q] = cnt[q] + jnp.where(sel, 1.0, 0.0)
        zsum = zsum + jnp.where(sel, jnp.exp(sums[c] - sums[0]), 0.0)
    inv_z = 1.0 / zsum
    for h in range(PEER_HEADS):
        base = h * PEER_KEY_DIM
        rows = slice(h * N_KEYS, (h + 1) * N_KEYS)
        posb = pos_b[h].astype(BF16).reshape(kg, BF16_ROWS, tt)
        cb = jnp.zeros((kg, BF16_ROWS, tt), BF16)
        for q in range(PEER_TOPK):
            cnt_q = jnp.broadcast_to(cnt[q][h:h + 1, :], (BF16_ROWS, tt)).astype(BF16)
            cb = cb + jnp.where(posb == float(q), cnt_q[None], jnp.zeros_like(cb))
        cb_ref[h * kg:(h + 1) * kg] = cb
        _store_lane_tiles(ea_ref, h, jnp.exp(sc_ref[base:base + N_KEYS, :] - a_top[0][h:h + 1, :])
                          * inv_z[h:h + 1, :])
        eb = jnp.exp(sc_ref[base + N_KEYS:base + 2 * N_KEYS, :] - b_top[0][h:h + 1, :])
        eb_ref[h * kg:(h + 1) * kg] = eb.astype(BF16).reshape(kg, BF16_ROWS, tt)
    return marked


def _peer_select_kernel(sc_ref, ra_ref, ea_ref, cb_ref, eb_ref):
    marked = _peer_select_body(sc_ref, ra_ref, ea_ref, cb_ref, eb_ref, exact_ties=False)

    @pl.when(jnp.max(marked) > float(PEER_TOPK))
    def _():
        _peer_select_body(sc_ref, ra_ref, ea_ref, cb_ref, eb_ref, exact_ties=True)


def _peer_select(sc, tt=256):
    w, t = sc.shape
    rows = PEER_HEADS * N_KEYS
    out_spec_a = pl.BlockSpec((tt // LANES, rows, LANES), lambda i: (i, 0, 0))
    shape_a = jax.ShapeDtypeStruct((t // LANES, rows, LANES), F32)
    out_spec_b = pl.BlockSpec((rows // BF16_ROWS, BF16_ROWS, tt), lambda i: (0, 0, i))
    shape_b = jax.ShapeDtypeStruct((rows // BF16_ROWS, BF16_ROWS, t), BF16)
    return pl.pallas_call(
        _peer_select_kernel,
        grid=(t // tt,),
        in_specs=[pl.BlockSpec((w, tt), lambda i: (0, i))],
        out_specs=[out_spec_a, out_spec_a, out_spec_b, out_spec_b],
        out_shape=[shape_a, shape_a, shape_b, shape_b],
        compiler_params=_cparams(("parallel",)),
        name="peer_select",
    )(sc)


def _row_on_sublanes(ref, h, i):
    return jnp.concatenate([ref[c, h, pl.ds(i, BF16_ROWS, stride=0), :] for c in range(ref.shape[0])],
                           axis=1)


def _masked_acts(hmat, ra_ref, ea_ref, cb_ref, eb_ref, key0):
    tt = hmat.shape[1]
    kg = N_KEYS // BF16_ROWS
    ys = []
    for k in range(hmat.shape[0] // N_KEYS):
        rows = slice(k * N_KEYS, (k + 1) * N_KEYS)
        w = jnp.zeros((kg, BF16_ROWS, tt), BF16)
        for h in range(PEER_HEADS):
            hs = slice(h * kg, (h + 1) * kg)
            ra = _row_on_sublanes(ra_ref, h, key0 + k).astype(BF16)
            ea = _row_on_sublanes(ea_ref, h, key0 + k).astype(BF16)
            w = w + eb_ref[hs] * jnp.where(cb_ref[hs] > ra[None], ea[None], jnp.zeros_like(w))
        ys.append(w.reshape(N_KEYS, tt) * _gelu_erf(hmat[rows, :]).astype(BF16))
    return jnp.concatenate(ys, axis=0)


def _peer_mix_kernel(x1b_ref, x1_ref, u_ref, vt_ref, ra_ref, ea_ref, cb_ref, eb_ref,
                     g_ref, b_ref, o_ref, acc_ref, *, ib):
    e = pl.program_id(1)

    @pl.when(e == 0)
    def _():
        acc_ref[...] = jnp.zeros_like(acc_ref)

    x1b = x1b_ref[...]
    acc = acc_ref[...]
    n_chunks = ib // PEER_CHUNK_KEYS
    chunk_rows = lambda c: slice(c * PEER_CHUNK_KEYS * N_KEYS, (c + 1) * PEER_CHUNK_KEYS * N_KEYS)
    h_next = _dot_nt(u_ref[chunk_rows(0), :], x1b)
    for c in range(n_chunks):
        hc = h_next
        if c + 1 < n_chunks:
            h_next = _dot_nt(u_ref[chunk_rows(c + 1), :], x1b)
        y = _masked_acts(hc, ra_ref, ea_ref, cb_ref, eb_ref, c * PEER_CHUNK_KEYS)
        acc = acc + _dot(vt_ref[:, chunk_rows(c)], y)
    acc_ref[...] = acc

    @pl.when(e == pl.num_programs(1) - 1)
    def _():
        o_ref[...] = _layernorm(ALPHA * x1_ref[...] + acc_ref[...].T, g_ref[...], b_ref[...])


def _peer_mix(x1b, x1, u, vt, ra, ea, cb, eb, g, bta, tt=512, te=2048):
    t = x1.shape[0]
    ib = te // N_KEYS
    ne = N_EXPERTS // te
    rows = PEER_HEADS * N_KEYS
    return pl.pallas_call(
        functools.partial(_peer_mix_kernel, ib=ib),
        grid=(t // tt, ne),
        in_specs=[pl.BlockSpec((tt, D_MODEL), lambda i, e: (i, 0)),
                  pl.BlockSpec((tt, D_MODEL), lambda i, e: (i, 0)),
                  pl.BlockSpec((te, D_MODEL), lambda i, e: (e, 0)),
                  pl.BlockSpec((D_MODEL, te), lambda i, e: (0, e)),
                  pl.BlockSpec((tt // LANES, PEER_HEADS, ib, LANES), lambda i, e: (i, 0, e, 0)),
                  pl.BlockSpec((tt // LANES, PEER_HEADS, ib, LANES), lambda i, e: (i, 0, e, 0)),
                  pl.BlockSpec((rows // BF16_ROWS, BF16_ROWS, tt), lambda i, e: (0, 0, i)),
                  pl.BlockSpec((rows // BF16_ROWS, BF16_ROWS, tt), lambda i, e: (0, 0, i)),
                  pl.BlockSpec((1, D_MODEL), lambda i, e: (0, 0)),
                  pl.BlockSpec((1, D_MODEL), lambda i, e: (0, 0))],
        out_specs=pl.BlockSpec((tt, D_MODEL), lambda i, e: (i, 0)),
        out_shape=jax.ShapeDtypeStruct((t, D_MODEL), F32),
        scratch_shapes=[pltpu.VMEM((D_MODEL, tt), F32)],
        compiler_params=_cparams(("parallel", "arbitrary")),
        name="peer_mix_ln2",
    )(x1b, x1, u, vt, ra, ea, cb, eb, g, bta)


def _block_diag(w, per_tile):
    n, k, _ = w.shape
    w = w.reshape(n // per_tile, per_tile, k, k)
    eye = jnp.eye(per_tile, dtype=w.dtype)
    out = jnp.einsum('tpij,pq->tpiqj', w, eye)
    return out.reshape(n // per_tile, per_tile * k, per_tile * k)


def _layer(x, mem, w_in, b_gate, conv_w, conv_b, lru_wa, lru_ba, lru_wx, lru_bx, lru_lambda,
           w_mem_kv, w_br_attn, w_br_lru, w_br_mem, w_out, ln1_g, ln1_b,
           peer_wq, peer_keys, peer_u, peer_v, ln2_g, ln2_b):
    b, s, d = x.shape
    t = b * s
    xf = x.reshape(t, d)
    xb = xf.astype(BF16)
    w_in_b = w_in.astype(BF16)
    row = lambda a: a.reshape(1, -1).astype(F32)

    off_rnn = 3 * DSWA_WIDTH
    off_mq = off_rnn + 2 * D_RNN
    off_gl = off_mq + MEM_WIDTH
    mq = _matmul(xb, w_in_b, off_mq, MEM_WIDTH, BF16, tn=512).reshape(b, s, MEM_WIDTH)

    os, lses = [], []
    for gi, dil in enumerate(DSWA_DILATIONS):
        xp = xb if dil == 1 else _permute_rows(x, dil).reshape(t, d)
        w_g = jnp.concatenate([w_in_b[:, part * DSWA_WIDTH + gi * GROUP_W:
                                      part * DSWA_WIDTH + (gi + 1) * GROUP_W] for part in range(3)], axis=1)
        qkv = _matmul(xp, w_g, 0, 3 * GROUP_W, BF16, tn=512).reshape(b, dil, s // dil, 3 * GROUP_W)
        o_g, lse_g = _dswa_group(qkv, gi)
        os.append(o_g.reshape(t, GROUP_W))
        lses.append(lse_g.reshape(t, HEAD_DIM))

    per_tile = MXU_TILE // LRU_BW
    wa_bd = _block_diag(lru_wa, per_tile).astype(BF16)
    wx_bd = _block_diag(lru_wx, per_tile).astype(BF16)
    rec = _rglru(xb.reshape(b, s, d), w_in_b[:, off_rnn:off_mq], conv_w.astype(F32), row(conv_b),
                 wa_bd, row(lru_ba), wx_bd, row(lru_bx), row(lru_lambda)).reshape(t, D_RNN)

    mem_len = mem.shape[1]
    kv = _matmul(mem.reshape(b * mem_len, d).astype(BF16), w_mem_kv.astype(BF16), 0,
                 2 * MEM_WIDTH, BF16, tn=1024).reshape(b, mem_len, 2 * MEM_WIDTH)
    memo = _memattn(mq, kv).reshape(t, MEM_WIDTH)

    x1, x1b = _mix(os, lses, rec, memo, xf, w_in_b[:, off_gl:], b_gate.astype(F32), w_br_attn.astype(BF16),
                   w_br_lru.astype(BF16), w_br_mem.astype(BF16), w_out.astype(BF16),
                   row(ln1_g), row(ln1_b))

    kbd = _block_diag(peer_keys.reshape(PEER_HEADS * 2, N_KEYS, PEER_KEY_DIM // 2), 2).astype(BF16)
    scores = _peer_scores(x1b, peer_wq.astype(BF16), kbd)

    ra, ea, cb, eb = _peer_select(scores)
    ra = ra.reshape(t // LANES, PEER_HEADS, N_KEYS, LANES)
    ea = ea.reshape(t // LANES, PEER_HEADS, N_KEYS, LANES)

    out = _peer_mix(x1b, x1, peer_u.astype(BF16), peer_v.T.astype(BF16), ra, ea, cb, eb,
                    row(ln2_g), row(ln2_b))
    return out.reshape(b, s, d)


def kernel(x, mem, w_in, b_gate, conv_w, conv_b, lru_wa, lru_ba, lru_wx, lru_bx, lru_lambda, w_mem_kv, w_br_attn, w_br_lru, w_br_mem, w_out, ln1_g, ln1_b, peer_wq, peer_keys, peer_u, peer_v, ln2_g, ln2_b):
    h = x.astype(F32)
    depth = w_in.shape[0]
    for l in range(depth):
        h = _layer(h, mem, w_in[l], b_gate[l], conv_w[l], conv_b[l], lru_wa[l], lru_ba[l],
                   lru_wx[l], lru_bx[l], lru_lambda[l], w_mem_kv[l], w_br_attn[l], w_br_lru[l],
                   w_br_mem[l], w_out[l], ln1_g[l], ln1_b[l], peer_wq[l], peer_keys[l],
                   peer_u[l], peer_v[l], ln2_g[l], ln2_b[l])
    return h.astype(x.dtype)
```

```python
import functools
import math

import jax
import jax.numpy as jnp
from jax import lax
from jax.experimental import pallas as pl
from jax.experimental.pallas import tpu as pltpu

F32 = jnp.float32
BF16 = jnp.bfloat16

D_MODEL = 1024
N_GROUPS = 3
DSWA_DILATIONS = (1, 4, 16)
DSWA_HEADS = 4
HEAD_DIM = 128
DSWA_BLK = 128
DSWA_TILE = 2048
GROUP_W = DSWA_HEADS * HEAD_DIM
DSWA_WIDTH = N_GROUPS * GROUP_W
D_RNN = 1024
LRU_BLOCKS = 16
LRU_BW = D_RNN // LRU_BLOCKS
CONV_W = 4
LRU_C = 8.0
MEM_HEADS = 4
MEM_WIDTH = MEM_HEADS * HEAD_DIM
PEER_HEADS = 8
PEER_KEY_DIM = 256
N_KEYS = 128
N_EXPERTS = N_KEYS * N_KEYS
PEER_TOPK = 16
PEER_CHUNK_KEYS = 4
BF16_ROWS = 16
ALPHA = 2.0 ** 0.25
LN_EPS = 1e-5
NEG_INF = -1e30

LANES = 128
SUBLANES = 8
MXU_TILE = 256
VMEM_LIMIT = 56 * 1024 * 1024


def _cparams(sem):
    return pltpu.CompilerParams(dimension_semantics=sem, vmem_limit_bytes=VMEM_LIMIT)


def _dot(a, b):
    return jnp.dot(a, b, preferred_element_type=F32)


def _dot_nt(a, b):
    return lax.dot_general(a, b, (((1,), (1,)), ((), ())), preferred_element_type=F32)


def _gelu_erf(x):
    return 0.5 * x * (1.0 + lax.erf(x * (1.0 / math.sqrt(2.0))))


def _layernorm(h, g, b):
    mu = jnp.mean(h, axis=-1, keepdims=True)
    c = h - mu
    var = jnp.mean(c * c, axis=-1, keepdims=True)
    return c * lax.rsqrt(var + LN_EPS) * g + b


def _matmul_kernel(x_ref, w_ref, o_ref):
    o_ref[...] = _dot(x_ref[...], w_ref[...]).astype(o_ref.dtype)


def _matmul(x, w, col_off, n_cols, out_dtype, tm=1024, tn=1024):
    m, k = x.shape
    tn = min(tn, n_cols)
    tm = min(tm, m)
    assert m % tm == 0 and n_cols % tn == 0 and col_off % tn == 0
    off = col_off // tn
    return pl.pallas_call(
        _matmul_kernel,
        grid=(m // tm, n_cols // tn),
        in_specs=[pl.BlockSpec((tm, k), lambda i, j: (i, 0)),
                  pl.BlockSpec((k, tn), lambda i, j: (0, off + j))],
        out_specs=pl.BlockSpec((tm, tn), lambda i, j: (i, j)),
        out_shape=jax.ShapeDtypeStruct((m, n_cols), out_dtype),
        compiler_params=_cparams(("parallel", "arbitrary")),
        name="matmul",
    )(x, w)


def _permute_kernel(x_ref, o_ref, *, dilation):
    rows = x_ref.shape[1] // dilation
    for r in range(dilation):
        o_ref[0, r] = x_ref[0, pl.ds(r, rows, stride=dilation), :].astype(o_ref.dtype)


def _permute_rows(x, dilation):
    b, s, d = x.shape
    m_len = s // dilation
    return pl.pallas_call(
        functools.partial(_permute_kernel, dilation=dilation),
        grid=(b, s // DSWA_TILE, d // LANES),
        in_specs=[pl.BlockSpec((1, DSWA_TILE, LANES), lambda bi, n, c: (bi, n, c))],
        out_specs=pl.BlockSpec((1, dilation, DSWA_TILE // dilation, LANES),
                               lambda bi, n, c: (bi, 0, n, c)),
        out_shape=jax.ShapeDtypeStruct((b, dilation, m_len, d), BF16),
        compiler_params=_cparams(("parallel", "parallel", "parallel")),
        name=f"permute_d{dilation}",
    )(x)


def _dswa_kernel(q_ref, kc_ref, kp_ref, vc_ref, vp_ref, o_ref, lse_ref, o_scr, lse_scr, *, dilation):
    nq = q_ref.shape[2] // DSWA_BLK
    row = lax.broadcasted_iota(jnp.int32, (DSWA_BLK, DSWA_BLK), 0)
    col = lax.broadcasted_iota(jnp.int32, (DSWA_BLK, DSWA_BLK), 1)
    mask_cur = col <= row
    mask_band = col >= row
    mask_first = col >= row + jnp.where(pl.program_id(1) == 0, DSWA_BLK, 0)
    scale = 1.0 / math.sqrt(HEAD_DIM)
    for r in range(dilation):
        for j in range(nq):
            rows = slice(j * DSWA_BLK, (j + 1) * DSWA_BLK)
            q = q_ref[0, r, rows, :]
            kcur = kc_ref[0, r, rows, :]
            vcur = vc_ref[0, r, rows, :]
            if j == 0:
                kprev = kp_ref[0, r]
                vprev = vp_ref[0, r]
                mask_prev = mask_first
            else:
                prows = slice((j - 1) * DSWA_BLK, j * DSWA_BLK)
                kprev = kc_ref[0, r, prows, :]
                vprev = vc_ref[0, r, prows, :]
                mask_prev = mask_band
            dst = pl.ds(j * DSWA_BLK * dilation + r, DSWA_BLK, stride=dilation)
            heads = lambda a: jnp.stack([a[:, h * HEAD_DIM:(h + 1) * HEAD_DIM]
                                         for h in range(DSWA_HEADS)], axis=0)
            q4, kp4, kc4, vp4, vc4 = heads(q), heads(kprev), heads(kcur), heads(vprev), heads(vcur)
            bdot_nt = lambda a, b: jnp.einsum('hqd,hkd->hqk', a, b, preferred_element_type=F32)
            bdot = lambda a, b: jnp.einsum('hqk,hkd->hqd', a, b, preferred_element_type=F32)
            s_p = jnp.where(mask_prev[None], bdot_nt(q4, kp4) * scale, NEG_INF)
            s_c = jnp.where(mask_cur[None], bdot_nt(q4, kc4) * scale, NEG_INF)
            m = jnp.maximum(jnp.max(s_p, axis=-1, keepdims=True), jnp.max(s_c, axis=-1, keepdims=True))
            p_p = jnp.exp(s_p - m)
            p_c = jnp.exp(s_c - m)
            l = jnp.sum(p_p, axis=-1, keepdims=True) + jnp.sum(p_c, axis=-1, keepdims=True)
            o = (bdot(p_p.astype(BF16), vp4) + bdot(p_c.astype(BF16), vc4)) / l
            lse = m + jnp.log(l)
            lse_tile = jnp.zeros((DSWA_BLK, HEAD_DIM), F32)
            for h in range(DSWA_HEADS):
                o_scr[h, dst, :] = o[h]
                lse_tile = jnp.where(col == h, lse[h], lse_tile)
            lse_scr[dst, :] = lse_tile
    for h in range(DSWA_HEADS):
        o_ref[0, :, h * HEAD_DIM:(h + 1) * HEAD_DIM] = o_scr[h].astype(o_ref.dtype)
    lse_ref[0] = lse_scr[...]


def _dswa_group(qkv, group):
    b, dilation, m_len, _ = qkv.shape
    s = dilation * m_len
    assert s % DSWA_TILE == 0 and DSWA_TILE % (dilation * DSWA_BLK) == 0
    rows = DSWA_TILE // dilation
    nq = rows // DSWA_BLK

    def cur(c):
        return pl.BlockSpec((1, dilation, rows, GROUP_W), lambda bi, n: (bi, 0, n, c))

    def prev(c):
        return pl.BlockSpec((1, dilation, DSWA_BLK, GROUP_W),
                            lambda bi, n: (bi, 0, jnp.maximum(n * nq - 1, 0), c))

    return pl.pallas_call(
        functools.partial(_dswa_kernel, dilation=dilation),
        grid=(b, s // DSWA_TILE),
        in_specs=[cur(0), cur(1), prev(1), cur(2), prev(2)],
        out_specs=[pl.BlockSpec((1, DSWA_TILE, GROUP_W), lambda bi, n: (bi, n, 0)),
                   pl.BlockSpec((1, DSWA_TILE, HEAD_DIM), lambda bi, n: (bi, n, 0))],
        out_shape=[jax.ShapeDtypeStruct((b, s, GROUP_W), BF16),
                   jax.ShapeDtypeStruct((b, s, HEAD_DIM), F32)],
        scratch_shapes=[pltpu.VMEM((DSWA_HEADS, DSWA_TILE, HEAD_DIM), F32),
                        pltpu.VMEM((DSWA_TILE, HEAD_DIM), F32)],
        compiler_params=_cparams(("parallel", "arbitrary")),
        name=f"dswa_g{group}",
    )(qkv, qkv, qkv, qkv, qkv)


def _linear_scan(a, b, h0):
    ts, d = a.shape
    a = a.reshape(ts // SUBLANES, SUBLANES, d)
    b = b.reshape(ts // SUBLANES, SUBLANES, d)
    row = lax.broadcasted_iota(jnp.int32, a.shape, 1)
    step = 1
    while step < SUBLANES:
        keep = row >= step
        a_s = jnp.where(keep, pltpu.roll(a, step, 1), 1.0)
        b_s = jnp.where(keep, pltpu.roll(b, step, 1), 0.0)
        b = a * b_s + b
        a = a * a_s
        step *= 2
    hs, carry = [], h0
    for g in range(ts // SUBLANES):
        hg = a[g] * carry + b[g]
        hs.append(hg)
        carry = hg[SUBLANES - 1:SUBLANES, :]
    return jnp.concatenate(hs, axis=0), carry


def _rglru_kernel(xb_ref, halo_ref, w_ref, cw_ref, cb_ref, wa_ref, ba_ref, wx_ref, bx_ref,
                  lam_ref, o_ref, h_ref, *, ts):
    si = pl.program_id(1)

    @pl.when(si == 0)
    def _():
        h_ref[...] = jnp.zeros_like(h_ref)

    z = _dot(xb_ref[0], w_ref[...])
    x = z[:, :D_RNN]
    yg = z[:, D_RNN:]
    halo = _dot(halo_ref[0], w_ref[:, :D_RNN])[BF16_ROWS - 8:, :]
    halo = jnp.where(si > 0, halo, 0.0)
    xfull = jnp.concatenate([halo, x], axis=0)
    xc = cb_ref[...] + cw_ref[CONV_W - 1:CONV_W, :] * x
    for k in range(1, CONV_W):
        xc = xc + cw_ref[CONV_W - 1 - k:CONV_W - k, :] * pltpu.roll(xfull, k, 0)[8:]

    xb = xc.astype(BF16)
    nblk = D_RNN // MXU_TILE
    ra = jnp.concatenate(
        [_dot(xb[:, c * MXU_TILE:(c + 1) * MXU_TILE], wa_ref[c]) for c in range(nblk)], axis=1)
    rx = jnp.concatenate(
        [_dot(xb[:, c * MXU_TILE:(c + 1) * MXU_TILE], wx_ref[c]) for c in range(nblk)], axis=1)
    r = jax.nn.sigmoid(ra + ba_ref[...])
    gate_i = jax.nn.sigmoid(rx + bx_ref[...])
    neg_lam = -lam_ref[...]
    softplus = jnp.maximum(neg_lam, 0.0) + jnp.log1p(jnp.exp(-jnp.abs(neg_lam)))
    log_a = (-LRU_C) * r * softplus
    a = jnp.exp(log_a)
    bterm = jnp.sqrt(1.0 - a * a) * gate_i * xc

    h, carry = _linear_scan(a, bterm, h_ref[0:1, :])
    h_ref[0:1, :] = carry
    o_ref[0] = (h * jax.nn.gelu(yg, approximate=True)).astype(o_ref.dtype)


def _rglru(xb, w_rnn, conv_w, conv_b, wa_bd, ba, wx_bd, bx, lam, ts=256):
    b, s, d = xb.shape
    assert s % ts == 0
    full = lambda shape: pl.BlockSpec(shape, lambda bi, si: (0,) * len(shape))
    return pl.pallas_call(
        functools.partial(_rglru_kernel, ts=ts),
        grid=(b, s // ts),
        in_specs=[pl.BlockSpec((1, ts, d), lambda bi, si: (bi, si, 0)),
                  pl.BlockSpec((1, BF16_ROWS, d),
                               lambda bi, si: (bi, jnp.maximum(si * (ts // BF16_ROWS) - 1, 0), 0)),
                  full(w_rnn.shape),
                  full((CONV_W, D_RNN)), full((1, D_RNN)),
                  full(wa_bd.shape), full((1, D_RNN)),
                  full(wx_bd.shape), full((1, D_RNN)), full((1, D_RNN))],
        out_specs=pl.BlockSpec((1, ts, D_RNN), lambda bi, si: (bi, si, 0)),
        out_shape=jax.ShapeDtypeStruct((b, s, D_RNN), BF16),
        scratch_shapes=[pltpu.VMEM((8, D_RNN), F32)],
        compiler_params=_cparams(("parallel", "arbitrary")),
        name="rglru",
    )(xb, xb, w_rnn, conv_w, conv_b, wa_bd, ba, wx_bd, bx, lam)


def _memattn_kernel(q_ref, kv_ref, o_ref):
    scale = 1.0 / math.sqrt(HEAD_DIM)
    for h in range(MEM_HEADS):
        hs = slice(h * HEAD_DIM, (h + 1) * HEAD_DIM)
        vs = slice(MEM_WIDTH + h * HEAD_DIM, MEM_WIDTH + (h + 1) * HEAD_DIM)
        s = _dot_nt(q_ref[0, :, hs], kv_ref[0, :, hs]) * scale
        m = jnp.max(s, axis=-1, keepdims=True)
        p = jnp.exp(s - m)
        l = jnp.sum(p, axis=-1, keepdims=True)
        o_ref[0, :, hs] = (_dot(p.astype(BF16), kv_ref[0, :, vs]) / l).astype(o_ref.dtype)


def _memattn(mq, kv, ts=1024):
    b, s, _ = mq.shape
    mem_len = kv.shape[1]
    return pl.pallas_call(
        _memattn_kernel,
        grid=(b, s // ts),
        in_specs=[pl.BlockSpec((1, ts, MEM_WIDTH), lambda bi, si: (bi, si, 0)),
                  pl.BlockSpec((1, mem_len, 2 * MEM_WIDTH), lambda bi, si: (bi, 0, 0))],
        out_specs=pl.BlockSpec((1, ts, MEM_WIDTH), lambda bi, si: (bi, si, 0)),
        out_shape=jax.ShapeDtypeStruct((b, s, MEM_WIDTH), BF16),
        compiler_params=_cparams(("parallel", "arbitrary")),
        name="memattn",
    )(mq, kv)


def _mix_kernel(o0_ref, o1_ref, o2_ref, l0_ref, l1_ref, l2_ref, rec_ref, memo_ref, x_ref,
                wg_ref, bg_ref, wa_ref, wl_ref, wm_ref, wo_ref, g_ref, b_ref, x1_ref, x1b_ref):
    l0, l1, l2 = l0_ref[...], l1_ref[...], l2_ref[...]
    mx = jnp.maximum(jnp.maximum(l0, l1), l2)
    e0, e1, e2 = jnp.exp(l0 - mx), jnp.exp(l1 - mx), jnp.exp(l2 - mx)
    inv = 1.0 / (e0 + e1 + e2)
    w0, w1, w2 = e0 * inv, e1 * inv, e2 * inv
    parts = []
    for h in range(DSWA_HEADS):
        hs = slice(h * HEAD_DIM, (h + 1) * HEAD_DIM)
        parts.append((w0[:, h:h + 1] * o0_ref[:, hs].astype(F32)
                      + w1[:, h:h + 1] * o1_ref[:, hs].astype(F32)
                      + w2[:, h:h + 1] * o2_ref[:, hs].astype(F32)).astype(BF16))
    attn = jnp.concatenate(parts, axis=1)
    d = D_MODEL
    gl = _dot(x_ref[...].astype(BF16), wg_ref[...])
    gate = lambda j: jax.nn.sigmoid(gl[:, j * d:(j + 1) * d] + bg_ref[j:j + 1, :])
    merged = (gate(0) * _dot(attn, wa_ref[...])
              + gate(1) * _dot(rec_ref[...], wl_ref[...])
              + gate(2) * _dot(memo_ref[...], wm_ref[...]))
    mix = _dot(merged.astype(BF16), wo_ref[...])
    x1 = _layernorm(ALPHA * x_ref[...] + mix, g_ref[...], b_ref[...])
    x1_ref[...] = x1
    x1b_ref[...] = x1.astype(BF16)


def _mix(os, lses, rec, memo, x, w_gate, b_gate, wa, wl, wm, wo, g, bta, tt=512):
    t = x.shape[0]
    rows = lambda w: pl.BlockSpec((tt, w), lambda i: (i, 0))
    full = lambda a: pl.BlockSpec(a.shape, lambda i: (0,) * a.ndim)
    return pl.pallas_call(
        _mix_kernel,
        grid=(t // tt,),
        in_specs=[rows(GROUP_W)] * 3 + [rows(HEAD_DIM)] * 3 + [
            rows(D_RNN), rows(MEM_WIDTH), rows(D_MODEL), full(w_gate),
            full(b_gate), full(wa), full(wl), full(wm), full(wo), full(g), full(bta)],
        out_specs=[rows(D_MODEL), rows(D_MODEL)],
        out_shape=[jax.ShapeDtypeStruct((t, D_MODEL), F32), jax.ShapeDtypeStruct((t, D_MODEL), BF16)],
        compiler_params=_cparams(("parallel",)),
        name="mix_ln1",
    )(*os, *lses, rec, memo, x, w_gate, b_gate, wa, wl, wm, wo, g, bta)


def _peer_scores_kernel(x_ref, wq_ref, kbd_ref, o_ref):
    q = _dot(x_ref[...], wq_ref[...]).astype(BF16)
    for h in range(PEER_HEADS):
        hs = slice(h * PEER_KEY_DIM, (h + 1) * PEER_KEY_DIM)
        o_ref[hs, :] = _dot_nt(kbd_ref[h], q[:, hs])


def _peer_scores(x1b, wq, kbd, tt=512):
    t = x1b.shape[0]
    w = PEER_HEADS * PEER_KEY_DIM
    return pl.pallas_call(
        _peer_scores_kernel,
        grid=(t // tt,),
        in_specs=[pl.BlockSpec((tt, D_MODEL), lambda i: (i, 0)),
                  pl.BlockSpec(wq.shape, lambda i: (0, 0)),
                  pl.BlockSpec(kbd.shape, lambda i: (0, 0, 0))],
        out_specs=pl.BlockSpec((w, tt), lambda i: (0, i)),
        out_shape=jax.ShapeDtypeStruct((w, t), F32),
        compiler_params=_cparams(("parallel",)),
        name="peer_scores",
    )(x1b, wq, kbd)


_PEER_CANDS = tuple((p, q) for p in range(PEER_TOPK) for q in range(PEER_TOPK)
                    if (p + 1) * (q + 1) <= PEER_TOPK)


def _top_positions(s, exact_ties):
    key = lax.broadcasted_iota(jnp.int32, s.shape, 0).astype(F32)
    pos = jnp.full(s.shape, float(PEER_TOPK), F32)
    work = s
    tops = []
    for p in range(PEER_TOPK):
        mx = jnp.max(work, axis=0, keepdims=True)
        if exact_ties:
            first = jnp.min(jnp.where(work == mx, key, float(N_KEYS)), axis=0, keepdims=True)
            hit = key == first
        else:
            hit = work == mx
        pos = jnp.where(hit, float(p), pos)
        work = jnp.where(hit, -jnp.inf, work)
        tops.append(mx)
    return pos, tops


def _store_lane_tiles(ref, h, val):
    for c in range(ref.shape[0]):
        ref[c, h * N_KEYS:(h + 1) * N_KEYS, :] = val[:, c * LANES:(c + 1) * LANES]


def _peer_select_body(sc_ref, ra_ref, ea_ref, cb_ref, eb_ref, exact_ties):
    tt = sc_ref.shape[1]
    kg = N_KEYS // BF16_ROWS
    pos_b, tops_a, tops_b = [], [], []
    marked = jnp.zeros((1, tt), F32)
    for h in range(PEER_HEADS):
        base = h * PEER_KEY_DIM
        pa, ta = _top_positions(sc_ref[base:base + N_KEYS, :], exact_ties)
        pb, tb = _top_positions(sc_ref[base + N_KEYS:base + 2 * N_KEYS, :], exact_ties)
        _store_lane_tiles(ra_ref, h, pa)
        pos_b.append(pb)
        tops_a.append(ta)
        tops_b.append(tb)
        if not exact_ties:
            for pos in (pa, pb):
                n_marked = jnp.sum(jnp.where(pos < float(PEER_TOPK), 1.0, 0.0), axis=0, keepdims=True)
                marked = jnp.maximum(marked, n_marked)
    a_top = [jnp.concatenate([tops_a[h][p] for h in range(PEER_HEADS)], axis=0)
             for p in range(PEER_TOPK)]
    b_top = [jnp.concatenate([tops_b[h][q] for h in range(PEER_HEADS)], axis=0)
             for q in range(PEER_TOPK)]
    sums = [a_top[p] + b_top[q] for p, q in _PEER_CANDS]
    n = len(sums)
    rank = [jnp.zeros((PEER_HEADS, tt), F32) for _ in range(n)]
    for c in range(n):
        for c2 in range(c):
            ge = sums[c2] >= sums[c]
            rank[c] = rank[c] + jnp.where(ge, 1.0, 0.0)
            rank[c2] = rank[c2] + jnp.where(ge, 0.0, 1.0)
    cnt = [jnp.zeros((PEER_HEADS, tt), F32) for _ in range(PEER_TOPK)]
    zsum = jnp.zeros((PEER_HEADS, tt), F32)
    for c, (p, q) in enumerate(_PEER_CANDS):
        sel = rank[c] < float(PEER_TOPK)
        cnt[q] = cnt[q] + jnp.where(sel, 1.0, 0.0)
        zsum = zsum + jnp.where(sel, jnp.exp(sums[c] - sums[0]), 0.0)
    inv_z = 1.0 / zsum
    for h in range(PEER_HEADS):
        base = h * PEER_KEY_DIM
        rows = slice(h * N_KEYS, (h + 1) * N_KEYS)
        posb = pos_b[h].astype(BF16).reshape(kg, BF16_ROWS, tt)
        cb = jnp.zeros((kg, BF16_ROWS, tt), BF16)
        for q in range(PEER_TOPK):
            cnt_q = jnp.broadcast_to(cnt[q][h:h + 1, :], (BF16_ROWS, tt)).astype(BF16)
            cb = cb + jnp.where(posb == float(q), cnt_q[None], jnp.zeros_like(cb))
        cb_ref[h * kg:(h + 1) * kg] = cb
        _store_lane_tiles(ea_ref, h, jnp.exp(sc_ref[base:base + N_KEYS, :] - a_top[0][h:h + 1, :])
                          * inv_z[h:h + 1, :])
        eb = jnp.exp(sc_ref[base + N_KEYS:base + 2 * N_KEYS, :] - b_top[0][h:h + 1, :])
        eb_ref[h * kg:(h + 1) * kg] = eb.astype(BF16).reshape(kg, BF16_ROWS, tt)
    return marked


def _peer_select_kernel(sc_ref, ra_ref, ea_ref, cb_ref, eb_ref):
    marked = _peer_select_body(sc_ref, ra_ref, ea_ref, cb_ref, eb_ref, exact_ties=False)

    @pl.when(jnp.max(marked) > float(PEER_TOPK))
    def _():
        _peer_select_body(sc_ref, ra_ref, ea_ref, cb_ref, eb_ref, exact_ties=True)


def _peer_select(sc, tt=256):
    w, t = sc.shape
    rows = PEER_HEADS * N_KEYS
    out_spec_a = pl.BlockSpec((tt // LANES, rows, LANES), lambda i: (i, 0, 0))
    shape_a = jax.ShapeDtypeStruct((t // LANES, rows, LANES), F32)
    out_spec_b = pl.BlockSpec((rows // BF16_ROWS, BF16_ROWS, tt), lambda i: (0, 0, i))
    shape_b = jax.ShapeDtypeStruct((rows // BF16_ROWS, BF16_ROWS, t), BF16)
    return pl.pallas_call(
        _peer_select_kernel,
        grid=(t // tt,),
        in_specs=[pl.BlockSpec((w, tt), lambda i: (0, i))],
        out_specs=[out_spec_a, out_spec_a, out_spec_b, out_spec_b],
        out_shape=[shape_a, shape_a, shape_b, shape_b],
        compiler_params=_cparams(("parallel",)),
        name="peer_select",
    )(sc)


def _row_on_sublanes(ref, h, i):
    return jnp.concatenate([ref[c, h, pl.ds(i, BF16_ROWS, stride=0), :] for c in range(ref.shape[0])],
                           axis=1)


def _masked_acts(hmat, ra_ref, ea_ref, cb_ref, eb_ref, key0):
    tt = hmat.shape[1]
    kg = N_KEYS // BF16_ROWS
    ys = []
    for k in range(hmat.shape[0] // N_KEYS):
        rows = slice(k * N_KEYS, (k + 1) * N_KEYS)
        w = jnp.zeros((kg, BF16_ROWS, tt), BF16)
        for h in range(PEER_HEADS):
            hs = slice(h * kg, (h + 1) * kg)
            ra = _row_on_sublanes(ra_ref, h, key0 + k).astype(BF16)
            ea = _row_on_sublanes(ea_ref, h, key0 + k).astype(BF16)
            w = w + eb_ref[hs] * jnp.where(cb_ref[hs] > ra[None], ea[None], jnp.zeros_like(w))
        ys.append(w.reshape(N_KEYS, tt) * _gelu_erf(hmat[rows, :]).astype(BF16))
    return jnp.concatenate(ys, axis=0)


def _peer_mix_kernel(x1b_ref, x1_ref, u_ref, vt_ref, ra_ref, ea_ref, cb_ref, eb_ref,
                     g_ref, b_ref, o_ref, acc_ref, *, ib):
    e = pl.program_id(1)

    @pl.when(e == 0)
    def _():
        acc_ref[...] = jnp.zeros_like(acc_ref)

    x1b = x1b_ref[...]
    acc = acc_ref[...]
    n_chunks = ib // PEER_CHUNK_KEYS
    chunk_rows = lambda c: slice(c * PEER_CHUNK_KEYS * N_KEYS, (c + 1) * PEER_CHUNK_KEYS * N_KEYS)
    h_next = _dot_nt(u_ref[chunk_rows(0), :], x1b)
    for c in range(n_chunks):
        hc = h_next
        if c + 1 < n_chunks:
            h_next = _dot_nt(u_ref[chunk_rows(c + 1), :], x1b)
        y = _masked_acts(hc, ra_ref, ea_ref, cb_ref, eb_ref, c * PEER_CHUNK_KEYS)
        acc = acc + _dot(vt_ref[:, chunk_rows(c)], y)
    acc_ref[...] = acc

    @pl.when(e == pl.num_programs(1) - 1)
    def _():
        o_ref[...] = _layernorm(ALPHA * x1_ref[...] + acc_ref[...].T, g_ref[...], b_ref[...])


def _peer_mix(x1b, x1, u, vt, ra, ea, cb, eb, g, bta, tt=512, te=2048):
    t = x1.shape[0]
    ib = te // N_KEYS
    ne = N_EXPERTS // te
    rows = PEER_HEADS * N_KEYS
    return pl.pallas_call(
        functools.partial(_peer_mix_kernel, ib=ib),
        grid=(t // tt, ne),
        in_specs=[pl.BlockSpec((tt, D_MODEL), lambda i, e: (i, 0)),
                  pl.BlockSpec((tt, D_MODEL), lambda i, e: (i, 0)),
                  pl.BlockSpec((te, D_MODEL), lambda i, e: (e, 0)),
                  pl.BlockSpec((D_MODEL, te), lambda i, e: (0, e)),
                  pl.BlockSpec((tt // LANES, PEER_HEADS, ib, LANES), lambda i, e: (i, 0, e, 0)),
                  pl.BlockSpec((tt // LANES, PEER_HEADS, ib, LANES), lambda i, e: (i, 0, e, 0)),
                  pl.BlockSpec((rows // BF16_ROWS, BF16_ROWS, tt), lambda i, e: (0, 0, i)),
                  pl.BlockSpec((rows // BF16_ROWS, BF16_ROWS, tt), lambda i, e: (0, 0, i)),
                  pl.BlockSpec((1, D_MODEL), lambda i, e: (0, 0)),
                  pl.BlockSpec((1, D_MODEL), lambda i, e: (0, 0))],
        out_specs=pl.BlockSpec((tt, D_MODEL), lambda i, e: (i, 0)),
        out_shape=jax.ShapeDtypeStruct((t, D_MODEL), F32),
        scratch_shapes=[pltpu.VMEM((D_MODEL, tt), F32)],
        compiler_params=_cparams(("parallel", "arbitrary")),
        name="peer_mix_ln2",
    )(x1b, x1, u, vt, ra, ea, cb, eb, g, bta)


def _block_diag(w, per_tile):
    n, k, _ = w.shape
    w = w.reshape(n // per_tile, per_tile, k, k)
    eye = jnp.eye(per_tile, dtype=w.dtype)
    out = jnp.einsum('tpij,pq->tpiqj', w, eye)
    return out.reshape(n // per_tile, per_tile * k, per_tile * k)


def _layer(x, mem, w_in, b_gate, conv_w, conv_b, lru_wa, lru_ba, lru_wx, lru_bx, lru_lambda,
           w_mem_kv, w_br_attn, w_br_lru, w_br_mem, w_out, ln1_g, ln1_b,
           peer_wq, peer_keys, peer_u, peer_v, ln2_g, ln2_b):
    b, s, d = x.shape
    t = b * s
    xf = x.reshape(t, d)
    xb = xf.astype(BF16)
    w_in_b = w_in.astype(BF16)
    row = lambda a: a.reshape(1, -1).astype(F32)

    off_rnn = 3 * DSWA_WIDTH
    off_mq = off_rnn + 2 * D_RNN
    off_gl = off_mq + MEM_WIDTH
    mq = _matmul(xb, w_in_b, off_mq, MEM_WIDTH, BF16, tn=512).reshape(b, s, MEM_WIDTH)

    os, lses = [], []
    for gi, dil in enumerate(DSWA_DILATIONS):
        xp = xb if dil == 1 else _permute_rows(x, dil).reshape(t, d)
        w_g = jnp.concatenate([w_in_b[:, part * DSWA_WIDTH + gi * GROUP_W:
                                      part * DSWA_WIDTH + (gi + 1) * GROUP_W] for part in range(3)], axis=1)
        qkv = _matmul(xp, w_g, 0, 3 * GROUP_W, BF16, tn=3 * GROUP_W).reshape(b, dil, s // dil, 3 * GROUP_W)
        o_g, lse_g = _dswa_group(qkv, gi)
        os.append(o_g.reshape(t, GROUP_W))
        lses.append(lse_g.reshape(t, HEAD_DIM))

    per_tile = MXU_TILE // LRU_BW
    wa_bd = _block_diag(lru_wa, per_tile).astype(BF16)
    wx_bd = _block_diag(lru_wx, per_tile).astype(BF16)
    rec = _rglru(xb.reshape(b, s, d), w_in_b[:, off_rnn:off_mq], conv_w.astype(F32), row(conv_b),
                 wa_bd, row(lru_ba), wx_bd, row(lru_bx), row(lru_lambda)).reshape(t, D_RNN)

    mem_len = mem.shape[1]
    kv = _matmul(mem.reshape(b * mem_len, d).astype(BF16), w_mem_kv.astype(BF16), 0,
                 2 * MEM_WIDTH, BF16, tn=1024).reshape(b, mem_len, 2 * MEM_WIDTH)
    memo = _memattn(mq, kv).reshape(t, MEM_WIDTH)

    x1, x1b = _mix(os, lses, rec, memo, xf, w_in_b[:, off_gl:], b_gate.astype(F32), w_br_attn.astype(BF16),
                   w_br_lru.astype(BF16), w_br_mem.astype(BF16), w_out.astype(BF16),
                   row(ln1_g), row(ln1_b))

    kbd = _block_diag(peer_keys.reshape(PEER_HEADS * 2, N_KEYS, PEER_KEY_DIM // 2), 2).astype(BF16)
    scores = _peer_scores(x1b, peer_wq.astype(BF16), kbd)

    ra, ea, cb, eb = _peer_select(scores)
    ra = ra.reshape(t // LANES, PEER_HEADS, N_KEYS, LANES)
    ea = ea.reshape(t // LANES, PEER_HEADS, N_KEYS, LANES)

    out = _peer_mix(x1b, x1, peer_u.astype(BF16), peer_v.T.astype(BF16), ra, ea, cb, eb,
                    row(ln2_g), row(ln2_b))
    return out.reshape(b, s, d)


def kernel(x, mem, w_in, b_gate, conv_w, conv_b, lru_wa, lru_ba, lru_wx, lru_bx, lru_lambda, w_mem_kv, w_br_attn, w_br_lru, w_br_mem, w_out, ln1_g, ln1_b, peer_wq, peer_keys, peer_u, peer_v, ln2_g, ln2_b):
    h = x.astype(F32)
    depth = w_in.shape[0]
    for l in range(depth):
        h = _layer(h, mem, w_in[l], b_gate[l], conv_w[l], conv_b[l], lru_wa[l], lru_ba[l],
                   lru_wx[l], lru_bx[l], lru_lambda[l], w_mem_kv[l], w_br_attn[l], w_br_lru[l],
                   w_br_mem[l], w_out[l], ln1_g[l], ln1_b[l], peer_wq[l], peer_keys[l],
                   peer_u[l], peer_v[l], ln2_g[l], ln2_b[l])
    return h.astype(x.dtype)
```

```python
import functools
import math

import jax
import jax.numpy as jnp
from jax import lax
from jax.experimental import pallas as pl
from jax.experimental.pallas import tpu as pltpu

F32 = jnp.float32
BF16 = jnp.bfloat16

D_MODEL = 1024
N_GROUPS = 3
DSWA_DILATIONS = (1, 4, 16)
DSWA_HEADS = 4
HEAD_DIM = 128
DSWA_BLK = 128
DSWA_TILE = 2048
GROUP_W = DSWA_HEADS * HEAD_DIM
DSWA_WIDTH = N_GROUPS * GROUP_W
D_RNN = 1024
LRU_BLOCKS = 16
LRU_BW = D_RNN // LRU_BLOCKS
CONV_W = 4
LRU_C = 8.0
MEM_HEADS = 4
MEM_WIDTH = MEM_HEADS * HEAD_DIM
PEER_HEADS = 8
PEER_KEY_DIM = 256
N_KEYS = 128
N_EXPERTS = N_KEYS * N_KEYS
PEER_TOPK = 16
PEER_CHUNK_KEYS = (4,) * 4
BF16_ROWS = 16
PEER_MASK_LANES = 256
ALPHA = 2.0 ** 0.25
LN_EPS = 1e-5
NEG_INF = -1e30

LANES = 128
SUBLANES = 8
MXU_TILE = 256
VMEM_LIMIT = 56 * 1024 * 1024


def _cparams(sem):
    return pltpu.CompilerParams(dimension_semantics=sem, vmem_limit_bytes=VMEM_LIMIT)


def _dot(a, b):
    return jnp.dot(a, b, preferred_element_type=F32)


def _dot_nt(a, b):
    return lax.dot_general(a, b, (((1,), (1,)), ((), ())), preferred_element_type=F32)


def _gelu_erf(x):
    return 0.5 * x * (1.0 + lax.erf(x * (1.0 / math.sqrt(2.0))))


def _layernorm(h, g, b):
    mu = jnp.mean(h, axis=-1, keepdims=True)
    c = h - mu
    var = jnp.mean(c * c, axis=-1, keepdims=True)
    return c * lax.rsqrt(var + LN_EPS) * g + b


def _matmul_kernel(x_ref, w_ref, o_ref):
    o_ref[...] = _dot(x_ref[...], w_ref[...]).astype(o_ref.dtype)


def _matmul(x, w, col_off, n_cols, out_dtype, tm=1024, tn=1024):
    m, k = x.shape
    tn = min(tn, n_cols)
    tm = min(tm, m)
    assert m % tm == 0 and n_cols % tn == 0 and col_off % tn == 0
    off = col_off // tn
    return pl.pallas_call(
        _matmul_kernel,
        grid=(m // tm, n_cols // tn),
        in_specs=[pl.BlockSpec((tm, k), lambda i, j: (i, 0)),
                  pl.BlockSpec((k, tn), lambda i, j: (0, off + j))],
        out_specs=pl.BlockSpec((tm, tn), lambda i, j: (i, j)),
        out_shape=jax.ShapeDtypeStruct((m, n_cols), out_dtype),
        compiler_params=_cparams(("parallel", "arbitrary")),
        name="matmul",
    )(x, w)


def _permute_kernel(x_ref, o_ref, *, dilation):
    rows = x_ref.shape[1] // dilation
    for bi in range(x_ref.shape[0]):
        for r in range(dilation):
            o_ref[bi, r] = x_ref[bi, pl.ds(r, rows, stride=dilation), :].astype(o_ref.dtype)


def _permute_rows(x, dilation):
    b, s, d = x.shape
    m_len = s // dilation
    return pl.pallas_call(
        functools.partial(_permute_kernel, dilation=dilation),
        grid=(s // DSWA_TILE, d // LANES),
        in_specs=[pl.BlockSpec((b, DSWA_TILE, LANES), lambda n, c: (0, n, c))],
        out_specs=pl.BlockSpec((b, dilation, DSWA_TILE // dilation, LANES),
                               lambda n, c: (0, 0, n, c)),
        out_shape=jax.ShapeDtypeStruct((b, dilation, m_len, d), BF16),
        compiler_params=_cparams(("parallel", "parallel")),
        name=f"permute_d{dilation}",
    )(x)


def _dswa_kernel(q_ref, kc_ref, kp_ref, vc_ref, vp_ref, o_ref, lse_ref, o_scr, lse_scr, *, dilation):
    nq = q_ref.shape[2] // DSWA_BLK
    row = lax.broadcasted_iota(jnp.int32, (DSWA_BLK, DSWA_BLK), 0)
    col = lax.broadcasted_iota(jnp.int32, (DSWA_BLK, DSWA_BLK), 1)
    mask_cur = col <= row
    mask_band = col >= row
    mask_first = col >= row + jnp.where(pl.program_id(1) == 0, DSWA_BLK, 0)
    scale = 1.0 / math.sqrt(HEAD_DIM)
    for r in range(dilation):
        for j in range(nq):
            rows = slice(j * DSWA_BLK, (j + 1) * DSWA_BLK)
            q = q_ref[0, r, rows, :]
            kcur = kc_ref[0, r, rows, :]
            vcur = vc_ref[0, r, rows, :]
            if j == 0:
                kprev = kp_ref[0, r]
                vprev = vp_ref[0, r]
                mask_prev = mask_first
            else:
                prows = slice((j - 1) * DSWA_BLK, j * DSWA_BLK)
                kprev = kc_ref[0, r, prows, :]
                vprev = vc_ref[0, r, prows, :]
                mask_prev = mask_band
            dst = pl.ds(j * DSWA_BLK * dilation + r, DSWA_BLK, stride=dilation)
            heads = lambda a: jnp.stack([a[:, h * HEAD_DIM:(h + 1) * HEAD_DIM]
                                         for h in range(DSWA_HEADS)], axis=0)
            q4, kp4, kc4, vp4, vc4 = heads(q), heads(kprev), heads(kcur), heads(vprev), heads(vcur)
            bdot_nt = lambda a, b: jnp.einsum('hqd,hkd->hqk', a, b, preferred_element_type=F32)
            bdot = lambda a, b: jnp.einsum('hqk,hkd->hqd', a, b, preferred_element_type=F32)
            s_p = jnp.where(mask_prev[None], bdot_nt(q4, kp4) * scale, NEG_INF)
            s_c = jnp.where(mask_cur[None], bdot_nt(q4, kc4) * scale, NEG_INF)
            m = jnp.maximum(jnp.max(s_p, axis=-1, keepdims=True), jnp.max(s_c, axis=-1, keepdims=True))
            p_p = jnp.exp(s_p - m)
            p_c = jnp.exp(s_c - m)
            l = jnp.sum(p_p, axis=-1, keepdims=True) + jnp.sum(p_c, axis=-1, keepdims=True)
            o = (bdot(p_p.astype(BF16), vp4) + bdot(p_c.astype(BF16), vc4)) / l
            lse = m + jnp.log(l)
            lse_tile = jnp.zeros((DSWA_BLK, HEAD_DIM), F32)
            for h in range(DSWA_HEADS):
                o_scr[h, dst, :] = o[h]
                lse_tile = jnp.where(col == h, lse[h], lse_tile)
            lse_scr[dst, :] = lse_tile
    for h in range(DSWA_HEADS):
        o_ref[0, :, h * HEAD_DIM:(h + 1) * HEAD_DIM] = o_scr[h].astype(o_ref.dtype)
    lse_ref[0] = lse_scr[...]


def _dswa_group(qkv, group):
    b, dilation, m_len, _ = qkv.shape
    s = dilation * m_len
    assert s % DSWA_TILE == 0 and DSWA_TILE % (dilation * DSWA_BLK) == 0
    rows = DSWA_TILE // dilation
    nq = rows // DSWA_BLK

    def cur(c):
        return pl.BlockSpec((1, dilation, rows, GROUP_W), lambda bi, n: (bi, 0, n, c))

    def prev(c):
        return pl.BlockSpec((1, dilation, DSWA_BLK, GROUP_W),
                            lambda bi, n: (bi, 0, jnp.maximum(n * nq - 1, 0), c))

    return pl.pallas_call(
        functools.partial(_dswa_kernel, dilation=dilation),
        grid=(b, s // DSWA_TILE),
        in_specs=[cur(0), cur(1), prev(1), cur(2), prev(2)],
        out_specs=[pl.BlockSpec((1, DSWA_TILE, GROUP_W), lambda bi, n: (bi, n, 0)),
                   pl.BlockSpec((1, DSWA_TILE, HEAD_DIM), lambda bi, n: (bi, n, 0))],
        out_shape=[jax.ShapeDtypeStruct((b, s, GROUP_W), BF16),
                   jax.ShapeDtypeStruct((b, s, HEAD_DIM), F32)],
        scratch_shapes=[pltpu.VMEM((DSWA_HEADS, DSWA_TILE, HEAD_DIM), F32),
                        pltpu.VMEM((DSWA_TILE, HEAD_DIM), F32)],
        compiler_params=_cparams(("parallel", "arbitrary")),
        name=f"dswa_g{group}",
    )(qkv, qkv, qkv, qkv, qkv)


def _linear_scan(a, b, h0):
    ts, d = a.shape
    a = a.reshape(ts // SUBLANES, SUBLANES, d)
    b = b.reshape(ts // SUBLANES, SUBLANES, d)
    row = lax.broadcasted_iota(jnp.int32, a.shape, 1)
    step = 1
    while step < SUBLANES:
        keep = row >= step
        a_s = jnp.where(keep, pltpu.roll(a, step, 1), 1.0)
        b_s = jnp.where(keep, pltpu.roll(b, step, 1), 0.0)
        b = a * b_s + b
        a = a * a_s
        step *= 2
    hs, carry = [], h0
    for g in range(ts // SUBLANES):
        hg = a[g] * carry + b[g]
        hs.append(hg)
        carry = hg[SUBLANES - 1:SUBLANES, :]
    return jnp.concatenate(hs, axis=0), carry


def _rglru_kernel(xb_ref, halo_ref, w_ref, cw_ref, cb_ref, wa_ref, ba_ref, wx_ref, bx_ref,
                  lam_ref, o_ref, h_ref, *, ts):
    si = pl.program_id(1)

    @pl.when(si == 0)
    def _():
        h_ref[...] = jnp.zeros_like(h_ref)

    z = _dot(xb_ref[0], w_ref[...])
    x = z[:, :D_RNN]
    yg = z[:, D_RNN:]
    halo = _dot(halo_ref[0], w_ref[:, :D_RNN])[BF16_ROWS - 8:, :]
    halo = jnp.where(si > 0, halo, 0.0)
    xfull = jnp.concatenate([halo, x], axis=0)
    xc = cb_ref[...] + cw_ref[CONV_W - 1:CONV_W, :] * x
    for k in range(1, CONV_W):
        xc = xc + cw_ref[CONV_W - 1 - k:CONV_W - k, :] * pltpu.roll(xfull, k, 0)[8:]

    xb = xc.astype(BF16)
    nblk = D_RNN // MXU_TILE
    ra = jnp.concatenate(
        [_dot(xb[:, c * MXU_TILE:(c + 1) * MXU_TILE], wa_ref[c]) for c in range(nblk)], axis=1)
    rx = jnp.concatenate(
        [_dot(xb[:, c * MXU_TILE:(c + 1) * MXU_TILE], wx_ref[c]) for c in range(nblk)], axis=1)
    r = jax.nn.sigmoid(ra + ba_ref[...])
    gate_i = jax.nn.sigmoid(rx + bx_ref[...])
    neg_lam = -lam_ref[...]
    softplus = jnp.maximum(neg_lam, 0.0) + jnp.log1p(jnp.exp(-jnp.abs(neg_lam)))
    log_a = (-LRU_C) * r * softplus
    a = jnp.exp(log_a)
    bterm = jnp.sqrt(1.0 - a * a) * gate_i * xc

    h, carry = _linear_scan(a, bterm, h_ref[0:1, :])
    h_ref[0:1, :] = carry
    o_ref[0] = (h * jax.nn.gelu(yg, approximate=True)).astype(o_ref.dtype)


def _rglru(xb, w_rnn, conv_w, conv_b, wa_bd, ba, wx_bd, bx, lam, ts=256):
    b, s, d = xb.shape
    assert s % ts == 0
    full = lambda shape: pl.BlockSpec(shape, lambda bi, si: (0,) * len(shape))
    return pl.pallas_call(
        functools.partial(_rglru_kernel, ts=ts),
        grid=(b, s // ts),
        in_specs=[pl.BlockSpec((1, ts, d), lambda bi, si: (bi, si, 0)),
                  pl.BlockSpec((1, BF16_ROWS, d),
                               lambda bi, si: (bi, jnp.maximum(si * (ts // BF16_ROWS) - 1, 0), 0)),
                  full(w_rnn.shape),
                  full((CONV_W, D_RNN)), full((1, D_RNN)),
                  full(wa_bd.shape), full((1, D_RNN)),
                  full(wx_bd.shape), full((1, D_RNN)), full((1, D_RNN))],
        out_specs=pl.BlockSpec((1, ts, D_RNN), lambda bi, si: (bi, si, 0)),
        out_shape=jax.ShapeDtypeStruct((b, s, D_RNN), BF16),
        scratch_shapes=[pltpu.VMEM((8, D_RNN), F32)],
        compiler_params=_cparams(("parallel", "arbitrary")),
        name="rglru",
    )(xb, xb, w_rnn, conv_w, conv_b, wa_bd, ba, wx_bd, bx, lam)


def _memattn_kernel(q_ref, kv_ref, o_ref):
    scale = 1.0 / math.sqrt(HEAD_DIM)
    for h in range(MEM_HEADS):
        hs = slice(h * HEAD_DIM, (h + 1) * HEAD_DIM)
        vs = slice(MEM_WIDTH + h * HEAD_DIM, MEM_WIDTH + (h + 1) * HEAD_DIM)
        s = _dot_nt(q_ref[0, :, hs], kv_ref[0, :, hs]) * scale
        m = jnp.max(s, axis=-1, keepdims=True)
        p = jnp.exp(s - m)
        l = jnp.sum(p, axis=-1, keepdims=True)
        o_ref[0, :, hs] = (_dot(p.astype(BF16), kv_ref[0, :, vs]) / l).astype(o_ref.dtype)


def _memattn(mq, kv, ts=1024):
    b, s, _ = mq.shape
    mem_len = kv.shape[1]
    return pl.pallas_call(
        _memattn_kernel,
        grid=(b, s // ts),
        in_specs=[pl.BlockSpec((1, ts, MEM_WIDTH), lambda bi, si: (bi, si, 0)),
                  pl.BlockSpec((1, mem_len, 2 * MEM_WIDTH), lambda bi, si: (bi, 0, 0))],
        out_specs=pl.BlockSpec((1, ts, MEM_WIDTH), lambda bi, si: (bi, si, 0)),
        out_shape=jax.ShapeDtypeStruct((b, s, MEM_WIDTH), BF16),
        compiler_params=_cparams(("parallel", "arbitrary")),
        name="memattn",
    )(mq, kv)


def _mix_kernel(o0_ref, o1_ref, o2_ref, l0_ref, l1_ref, l2_ref, rec_ref, memo_ref, x_ref,
                wg_ref, bg_ref, wa_ref, wl_ref, wm_ref, wo_ref, g_ref, b_ref, x1_ref, x1b_ref):
    l0, l1, l2 = l0_ref[...], l1_ref[...], l2_ref[...]
    mx = jnp.maximum(jnp.maximum(l0, l1), l2)
    e0, e1, e2 = jnp.exp(l0 - mx), jnp.exp(l1 - mx), jnp.exp(l2 - mx)
    inv = 1.0 / (e0 + e1 + e2)
    w0, w1, w2 = e0 * inv, e1 * inv, e2 * inv
    parts = []
    for h in range(DSWA_HEADS):
        hs = slice(h * HEAD_DIM, (h + 1) * HEAD_DIM)
        parts.append((w0[:, h:h + 1] * o0_ref[:, hs].astype(F32)
                      + w1[:, h:h + 1] * o1_ref[:, hs].astype(F32)
                      + w2[:, h:h + 1] * o2_ref[:, hs].astype(F32)).astype(BF16))
    attn = jnp.concatenate(parts, axis=1)
    d = D_MODEL
    gl = _dot(x_ref[...].astype(BF16), wg_ref[...])
    gate = lambda j: jax.nn.sigmoid(gl[:, j * d:(j + 1) * d] + bg_ref[j:j + 1, :])
    merged = (gate(0) * _dot(attn, wa_ref[...])
              + gate(1) * _dot(rec_ref[...], wl_ref[...])
              + gate(2) * _dot(memo_ref[...], wm_ref[...]))
    mix = _dot(merged.astype(BF16), wo_ref[...])
    x1 = _layernorm(ALPHA * x_ref[...] + mix, g_ref[...], b_ref[...])
    x1_ref[...] = x1
    x1b_ref[...] = x1.astype(BF16)


def _mix(os, lses, rec, memo, x, w_gate, b_gate, wa, wl, wm, wo, g, bta, tt=512):
    t = x.shape[0]
    rows = lambda w: pl.BlockSpec((tt, w), lambda i: (i, 0))
    full = lambda a: pl.BlockSpec(a.shape, lambda i: (0,) * a.ndim)
    return pl.pallas_call(
        _mix_kernel,
        grid=(t // tt,),
        in_specs=[rows(GROUP_W)] * 3 + [rows(HEAD_DIM)] * 3 + [
            rows(D_RNN), rows(MEM_WIDTH), rows(D_MODEL), full(w_gate),
            full(b_gate), full(wa), full(wl), full(wm), full(wo), full(g), full(bta)],
        out_specs=[rows(D_MODEL), rows(D_MODEL)],
        out_shape=[jax.ShapeDtypeStruct((t, D_MODEL), F32), jax.ShapeDtypeStruct((t, D_MODEL), BF16)],
        compiler_params=_cparams(("parallel",)),
        name="mix_ln1",
    )(*os, *lses, rec, memo, x, w_gate, b_gate, wa, wl, wm, wo, g, bta)


def _peer_scores_kernel(x_ref, wq_ref, kbd_ref, o_ref):
    q = _dot(x_ref[...], wq_ref[...]).astype(BF16)
    for h in range(PEER_HEADS):
        hs = slice(h * PEER_KEY_DIM, (h + 1) * PEER_KEY_DIM)
        o_ref[hs, :] = _dot_nt(kbd_ref[h], q[:, hs])


def _peer_scores(x1b, wq, kbd, tt=512):
    t = x1b.shape[0]
    w = PEER_HEADS * PEER_KEY_DIM
    return pl.pallas_call(
        _peer_scores_kernel,
        grid=(t // tt,),
        in_specs=[pl.BlockSpec((tt, D_MODEL), lambda i: (i, 0)),
                  pl.BlockSpec(wq.shape, lambda i: (0, 0)),
                  pl.BlockSpec(kbd.shape, lambda i: (0, 0, 0))],
        out_specs=pl.BlockSpec((w, tt), lambda i: (0, i)),
        out_shape=jax.ShapeDtypeStruct((w, t), F32),
        compiler_params=_cparams(("parallel",)),
        name="peer_scores",
    )(x1b, wq, kbd)


_PEER_CANDS = tuple((p, q) for p in range(PEER_TOPK) for q in range(PEER_TOPK)
                    if (p + 1) * (q + 1) <= PEER_TOPK)


def _top_positions(s, exact_ties):
    key = lax.broadcasted_iota(jnp.int32, s.shape, 0).astype(F32)
    pos = jnp.full(s.shape, float(PEER_TOPK), F32)
    work = s
    tops = []
    for p in range(PEER_TOPK):
        mx = jnp.max(work, axis=0, keepdims=True)
        if exact_ties:
            first = jnp.min(jnp.where(work == mx, key, float(N_KEYS)), axis=0, keepdims=True)
            hit = key == first
        else:
            hit = work == mx
        pos = jnp.where(hit, float(p), pos)
        work = jnp.where(hit, -jnp.inf, work)
        tops.append(mx)
    return pos, tops


def _store_lane_tiles(ref, h, val):
    for c in range(ref.shape[0]):
        ref[c, h * N_KEYS:(h + 1) * N_KEYS, :] = val[:, c * LANES:(c + 1) * LANES]


def _peer_select_body(sc_ref, ra_ref, ea_ref, cb_ref, eb_ref, exact_ties):
    tt = sc_ref.shape[1]
    kg = N_KEYS // BF16_ROWS
    pos_b, tops_a, tops_b = [], [], []
    marked = jnp.zeros((1, tt), F32)
    for h in range(PEER_HEADS):
        base = h * PEER_KEY_DIM
        pa, ta = _top_positions(sc_ref[base:base + N_KEYS, :], exact_ties)
        pb, tb = _top_positions(sc_ref[base + N_KEYS:base + 2 * N_KEYS, :], exact_ties)
        _store_lane_tiles(ra_ref, h, pa)
        pos_b.append(pb)
        tops_a.append(ta)
        tops_b.append(tb)
        if not exact_ties:
            for pos in (pa, pb):
                n_marked = jnp.sum(jnp.where(pos < float(PEER_TOPK), 1.0, 0.0), axis=0, keepdims=True)
                marked = jnp.maximum(marked, n_marked)
    a_top = [jnp.concatenate([tops_a[h][p] for h in range(PEER_HEADS)], axis=0)
             for p in range(PEER_TOPK)]
    b_top = [jnp.concatenate([tops_b[h][q] for h in range(PEER_HEADS)], axis=0)
             for q in range(PEER_TOPK)]
    sums = [a_top[p] + b_top[q] for p, q in _PEER_CANDS]
    n = len(sums)
    rank = [jnp.zeros((PEER_HEADS, tt), F32) for _ in range(n)]
    for c in range(n):
        for c2 in range(c):
            ge = sums[c2] >= sums[c]
            rank[c] = rank[c] + jnp.where(ge, 1.0, 0.0)
            rank[c2] = rank[c2] + jnp.where(ge, 0.0, 1.0)
    cnt = [jnp.zeros((PEER_HEADS, tt), F32) for _ in range(PEER_TOPK)]
    zsum = jnp.zeros((PEER_HEADS, tt), F32)
    for c, (p, q) in enumerate(_PEER_CANDS):
        sel = rank[c] < float(PEER_TOPK)
        cnt[q] = cnt[q] + jnp.where(sel, 1.0, 0.0)
        zsum = zsum + jnp.where(sel, jnp.exp(sums[c] - sums[0]), 0.0)
    inv_z = 1.0 / zsum
    for h in range(PEER_HEADS):
        base = h * PEER_KEY_DIM
        rows = slice(h * N_KEYS, (h + 1) * N_KEYS)
        posb = pos_b[h].astype(BF16).reshape(kg, BF16_ROWS, tt)
        cb = jnp.zeros((kg, BF16_ROWS, tt), BF16)
        for q in range(PEER_TOPK):
            cnt_q = jnp.broadcast_to(cnt[q][h:h + 1, :], (BF16_ROWS, tt)).astype(BF16)
            cb = cb + jnp.where(posb == float(q), cnt_q[None], jnp.zeros_like(cb))
        cb_ref[h * kg:(h + 1) * kg] = cb
        _store_lane_tiles(ea_ref, h, jnp.exp(sc_ref[base:base + N_KEYS, :] - a_top[0][h:h + 1, :])
                          * inv_z[h:h + 1, :])
        eb = jnp.exp(sc_ref[base + N_KEYS:base + 2 * N_KEYS, :] - b_top[0][h:h + 1, :])
        eb_ref[h * kg:(h + 1) * kg] = eb.astype(BF16).reshape(kg, BF16_ROWS, tt)
    return marked


def _peer_select_kernel(sc_ref, ra_ref, ea_ref, cb_ref, eb_ref):
    marked = _peer_select_body(sc_ref, ra_ref, ea_ref, cb_ref, eb_ref, exact_ties=False)

    @pl.when(jnp.max(marked) > float(PEER_TOPK))
    def _():
        _peer_select_body(sc_ref, ra_ref, ea_ref, cb_ref, eb_ref, exact_ties=True)


def _peer_select(sc, tt=256):
    w, t = sc.shape
    rows = PEER_HEADS * N_KEYS
    out_spec_a = pl.BlockSpec((tt // LANES, rows, LANES), lambda i: (i, 0, 0))
    shape_a = jax.ShapeDtypeStruct((t // LANES, rows, LANES), F32)
    out_spec_b = pl.BlockSpec((rows // BF16_ROWS, BF16_ROWS, tt), lambda i: (0, 0, i))
    shape_b = jax.ShapeDtypeStruct((rows // BF16_ROWS, BF16_ROWS, t), BF16)
    return pl.pallas_call(
        _peer_select_kernel,
        grid=(t // tt,),
        in_specs=[pl.BlockSpec((w, tt), lambda i: (0, i))],
        out_specs=[out_spec_a, out_spec_a, out_spec_b, out_spec_b],
        out_shape=[shape_a, shape_a, shape_b, shape_b],
        compiler_params=_cparams(("parallel",)),
        name="peer_select",
    )(sc)


def _row_on_sublanes(ref, h, i, lane0, lanes):
    tiles = range(lane0 // LANES, (lane0 + lanes) // LANES)
    return jnp.concatenate([ref[c, h, pl.ds(i, BF16_ROWS, stride=0), :] for c in tiles], axis=1)


def _masked_acts(hmat, ra_ref, ea_ref, cb_ref, eb_ref, key0):
    tt = hmat.shape[1]
    kg = N_KEYS // BF16_ROWS
    ys = []
    for k in range(hmat.shape[0] // N_KEYS):
        rows = slice(k * N_KEYS, (k + 1) * N_KEYS)
        ws = []
        for l0 in range(0, tt, PEER_MASK_LANES):
            ls = slice(l0, l0 + PEER_MASK_LANES)
            w = jnp.zeros((kg, BF16_ROWS, PEER_MASK_LANES), BF16)
            for h in range(PEER_HEADS):
                hs = slice(h * kg, (h + 1) * kg)
                ra = _row_on_sublanes(ra_ref, h, key0 + k, l0, PEER_MASK_LANES).astype(BF16)
                ea = _row_on_sublanes(ea_ref, h, key0 + k, l0, PEER_MASK_LANES).astype(BF16)
                w = w + eb_ref[hs, :, ls] * jnp.where(cb_ref[hs, :, ls] > ra[None], ea[None],
                                                      jnp.zeros_like(w))
            ws.append(w.reshape(N_KEYS, PEER_MASK_LANES))
        ys.append(jnp.concatenate(ws, axis=1) * _gelu_erf(hmat[rows, :]).astype(BF16))
    return jnp.concatenate(ys, axis=0)


def _peer_mix_kernel(x1b_ref, x1_ref, u_ref, vt_ref, ra_ref, ea_ref, cb_ref, eb_ref,
                     g_ref, b_ref, o_ref, acc_ref, *, ib):
    e = pl.program_id(1)

    @pl.when(e == 0)
    def _():
        acc_ref[...] = jnp.zeros_like(acc_ref)

    x1b = x1b_ref[...]
    acc = acc_ref[...]
    assert sum(PEER_CHUNK_KEYS) == ib
    starts = [sum(PEER_CHUNK_KEYS[:c]) for c in range(len(PEER_CHUNK_KEYS) + 1)]
    chunk_rows = lambda c: slice(starts[c] * N_KEYS, starts[c + 1] * N_KEYS)
    n_chunks = len(PEER_CHUNK_KEYS)
    h_next = _dot_nt(u_ref[chunk_rows(0), :], x1b)
    for c in range(n_chunks):
        hc = h_next
        if c + 1 < n_chunks:
            h_next = _dot_nt(u_ref[chunk_rows(c + 1), :], x1b)
        y = _masked_acts(hc, ra_ref, ea_ref, cb_ref, eb_ref, starts[c])
        acc = acc + _dot(vt_ref[:, chunk_rows(c)], y)
    acc_ref[...] = acc

    @pl.when(e == pl.num_programs(1) - 1)
    def _():
        o_ref[...] = _layernorm(ALPHA * x1_ref[...] + acc_ref[...].T, g_ref[...], b_ref[...])


def _peer_mix(x1b, x1, u, vt, ra, ea, cb, eb, g, bta, tt=512, te=sum(PEER_CHUNK_KEYS) * N_KEYS):
    t = x1.shape[0]
    ib = te // N_KEYS
    ne = N_EXPERTS // te
    rows = PEER_HEADS * N_KEYS
    return pl.pallas_call(
        functools.partial(_peer_mix_kernel, ib=ib),
        grid=(t // tt, ne),
        in_specs=[pl.BlockSpec((tt, D_MODEL), lambda i, e: (i, 0)),
                  pl.BlockSpec((tt, D_MODEL), lambda i, e: (i, 0)),
                  pl.BlockSpec((te, D_MODEL), lambda i, e: (e, 0)),
                  pl.BlockSpec((D_MODEL, te), lambda i, e: (0, e)),
                  pl.BlockSpec((tt // LANES, PEER_HEADS, ib, LANES), lambda i, e: (i, 0, e, 0)),
                  pl.BlockSpec((tt // LANES, PEER_HEADS, ib, LANES), lambda i, e: (i, 0, e, 0)),
                  pl.BlockSpec((rows // BF16_ROWS, BF16_ROWS, tt), lambda i, e: (0, 0, i)),
                  pl.BlockSpec((rows // BF16_ROWS, BF16_ROWS, tt), lambda i, e: (0, 0, i)),
                  pl.BlockSpec((1, D_MODEL), lambda i, e: (0, 0)),
                  pl.BlockSpec((1, D_MODEL), lambda i, e: (0, 0))],
        out_specs=pl.BlockSpec((tt, D_MODEL), lambda i, e: (i, 0)),
        out_shape=jax.ShapeDtypeStruct((t, D_MODEL), F32),
        scratch_shapes=[pltpu.VMEM((D_MODEL, tt), F32)],
        compiler_params=_cparams(("parallel", "arbitrary")),
        name="peer_mix_ln2",
    )(x1b, x1, u, vt, ra, ea, cb, eb, g, bta)


def _block_diag(w, per_tile):
    n, k, _ = w.shape
    w = w.reshape(n // per_tile, per_tile, k, k)
    eye = jnp.eye(per_tile, dtype=w.dtype)
    out = jnp.einsum('tpij,pq->tpiqj', w, eye)
    return out.reshape(n // per_tile, per_tile * k, per_tile * k)


def _layer(x, mem, w_in, b_gate, conv_w, conv_b, lru_wa, lru_ba, lru_wx, lru_bx, lru_lambda,
           w_mem_kv, w_br_attn, w_br_lru, w_br_mem, w_out, ln1_g, ln1_b,
           peer_wq, peer_keys, peer_u, peer_v, ln2_g, ln2_b):
    b, s, d = x.shape
    t = b * s
    xf = x.reshape(t, d)
    xb = xf.astype(BF16)
    w_in_b = w_in.astype(BF16)
    row = lambda a: a.reshape(1, -1).astype(F32)

    off_rnn = 3 * DSWA_WIDTH
    off_mq = off_rnn + 2 * D_RNN
    off_gl = off_mq + MEM_WIDTH
    mq = _matmul(xb, w_in_b, off_mq, MEM_WIDTH, BF16, tn=512).reshape(b, s, MEM_WIDTH)

    os, lses = [], []
    for gi, dil in enumerate(DSWA_DILATIONS):
        xp = xb if dil == 1 else _permute_rows(x, dil).reshape(t, d)
        w_g = jnp.concatenate([w_in_b[:, part * DSWA_WIDTH + gi * GROUP_W:
                                      part * DSWA_WIDTH + (gi + 1) * GROUP_W] for part in range(3)], axis=1)
        qkv = _matmul(xp, w_g, 0, 3 * GROUP_W, BF16, tn=3 * GROUP_W).reshape(b, dil, s // dil, 3 * GROUP_W)
        o_g, lse_g = _dswa_group(qkv, gi)
        os.append(o_g.reshape(t, GROUP_W))
        lses.append(lse_g.reshape(t, HEAD_DIM))

    per_tile = MXU_TILE // LRU_BW
    wa_bd = _block_diag(lru_wa, per_tile).astype(BF16)
    wx_bd = _block_diag(lru_wx, per_tile).astype(BF16)
    rec = _rglru(xb.reshape(b, s, d), w_in_b[:, off_rnn:off_mq], conv_w.astype(F32), row(conv_b),
                 wa_bd, row(lru_ba), wx_bd, row(lru_bx), row(lru_lambda)).reshape(t, D_RNN)

    mem_len = mem.shape[1]
    kv = _matmul(mem.reshape(b * mem_len, d).astype(BF16), w_mem_kv.astype(BF16), 0,
                 2 * MEM_WIDTH, BF16, tn=1024).reshape(b, mem_len, 2 * MEM_WIDTH)
    memo = _memattn(mq, kv).reshape(t, MEM_WIDTH)

    x1, x1b = _mix(os, lses, rec, memo, xf, w_in_b[:, off_gl:], b_gate.astype(F32), w_br_attn.astype(BF16),
                   w_br_lru.astype(BF16), w_br_mem.astype(BF16), w_out.astype(BF16),
                   row(ln1_g), row(ln1_b))

    kbd = _block_diag(peer_keys.reshape(PEER_HEADS * 2, N_KEYS, PEER_KEY_DIM // 2), 2).astype(BF16)
    scores = _peer_scores(x1b, peer_wq.astype(BF16), kbd)

    ra, ea, cb, eb = _peer_select(scores)
    ra = ra.reshape(t // LANES, PEER_HEADS, N_KEYS, LANES)
    ea = ea.reshape(t // LANES, PEER_HEADS, N_KEYS, LANES)

    out = _peer_mix(x1b, x1, peer_u.astype(BF16), peer_v.T.astype(BF16), ra, ea, cb, eb,
                    row(ln2_g), row(ln2_b))
    return out.reshape(b, s, d)


def kernel(x, mem, w_in, b_gate, conv_w, conv_b, lru_wa, lru_ba, lru_wx, lru_bx, lru_lambda, w_mem_kv, w_br_attn, w_br_lru, w_br_mem, w_out, ln1_g, ln1_b, peer_wq, peer_keys, peer_u, peer_v, ln2_g, ln2_b):
    h = x.astype(F32)
    depth = w_in.shape[0]
    for l in range(depth):
        h = _layer(h, mem, w_in[l], b_gate[l], conv_w[l], conv_b[l], lru_wa[l], lru_ba[l],
                   lru_wx[l], lru_bx[l], lru_lambda[l], w_mem_kv[l], w_br_attn[l], w_br_lru[l],
                   w_br_mem[l], w_out[l], ln1_g[l], ln1_b[l], peer_wq[l], peer_keys[l],
                   peer_u[l], peer_v[l], ln2_g[l], ln2_b[l])
    return h.astype(x.dtype)
```

```python
import functools
import math

import jax
import jax.numpy as jnp
from jax import lax
from jax.experimental import pallas as pl
from jax.experimental.pallas import tpu as pltpu

F32 = jnp.float32
BF16 = jnp.bfloat16

D_MODEL = 1024
N_GROUPS = 3
DSWA_DILATIONS = (1, 4, 16)
DSWA_HEADS = 4
HEAD_DIM = 128
DSWA_BLK = 128
DSWA_TILE = 2048
GROUP_W = DSWA_HEADS * HEAD_DIM
DSWA_WIDTH = N_GROUPS * GROUP_W
D_RNN = 1024
LRU_BLOCKS = 16
LRU_BW = D_RNN // LRU_BLOCKS
CONV_W = 4
LRU_C = 8.0
MEM_HEADS = 4
MEM_WIDTH = MEM_HEADS * HEAD_DIM
PEER_HEADS = 8
PEER_KEY_DIM = 256
N_KEYS = 128
N_EXPERTS = N_KEYS * N_KEYS
PEER_TOPK = 16
PEER_CHUNK_KEYS = (8,) * 2
BF16_ROWS = 16
PEER_MASK_LANES = 256
ALPHA = 2.0 ** 0.25
LN_EPS = 1e-5
NEG_INF = -1e30

LANES = 128
SUBLANES = 8
MXU_TILE = 256
VMEM_LIMIT = 56 * 1024 * 1024


def _cparams(sem):
    return pltpu.CompilerParams(dimension_semantics=sem, vmem_limit_bytes=VMEM_LIMIT)


def _dot(a, b):
    return jnp.dot(a, b, preferred_element_type=F32)


def _dot_nt(a, b):
    return lax.dot_general(a, b, (((1,), (1,)), ((), ())), preferred_element_type=F32)


def _gelu_erf(x):
    return 0.5 * x * (1.0 + lax.erf(x * (1.0 / math.sqrt(2.0))))


def _layernorm(h, g, b):
    mu = jnp.mean(h, axis=-1, keepdims=True)
    c = h - mu
    var = jnp.mean(c * c, axis=-1, keepdims=True)
    return c * lax.rsqrt(var + LN_EPS) * g + b


def _matmul_kernel(x_ref, w_ref, o_ref):
    o_ref[...] = _dot(x_ref[...], w_ref[...]).astype(o_ref.dtype)


def _matmul(x, w, col_off, n_cols, out_dtype, tm=1024, tn=1024):
    m, k = x.shape
    tn = min(tn, n_cols)
    tm = min(tm, m)
    assert m % tm == 0 and n_cols % tn == 0 and col_off % tn == 0
    off = col_off // tn
    return pl.pallas_call(
        _matmul_kernel,
        grid=(m // tm, n_cols // tn),
        in_specs=[pl.BlockSpec((tm, k), lambda i, j: (i, 0)),
                  pl.BlockSpec((k, tn), lambda i, j: (0, off + j))],
        out_specs=pl.BlockSpec((tm, tn), lambda i, j: (i, j)),
        out_shape=jax.ShapeDtypeStruct((m, n_cols), out_dtype),
        compiler_params=_cparams(("parallel", "arbitrary")),
        name="matmul",
    )(x, w)


def _permute_kernel(x_ref, o_ref, *, dilation):
    rows = x_ref.shape[1] // dilation
    for bi in range(x_ref.shape[0]):
        for r in range(dilation):
            o_ref[bi, r] = x_ref[bi, pl.ds(r, rows, stride=dilation), :].astype(o_ref.dtype)


def _permute_rows(x, dilation):
    b, s, d = x.shape
    m_len = s // dilation
    return pl.pallas_call(
        functools.partial(_permute_kernel, dilation=dilation),
        grid=(s // DSWA_TILE, d // LANES),
        in_specs=[pl.BlockSpec((b, DSWA_TILE, LANES), lambda n, c: (0, n, c))],
        out_specs=pl.BlockSpec((b, dilation, DSWA_TILE // dilation, LANES),
                               lambda n, c: (0, 0, n, c)),
        out_shape=jax.ShapeDtypeStruct((b, dilation, m_len, d), BF16),
        compiler_params=_cparams(("parallel", "parallel")),
        name=f"permute_d{dilation}",
    )(x)


def _dswa_kernel(q_ref, kc_ref, kp_ref, vc_ref, vp_ref, o_ref, lse_ref, o_scr, lse_scr, *, dilation):
    nq = q_ref.shape[2] // DSWA_BLK
    row = lax.broadcasted_iota(jnp.int32, (DSWA_BLK, DSWA_BLK), 0)
    col = lax.broadcasted_iota(jnp.int32, (DSWA_BLK, DSWA_BLK), 1)
    mask_cur = col <= row
    mask_band = col >= row
    mask_first = col >= row + jnp.where(pl.program_id(1) == 0, DSWA_BLK, 0)
    scale = 1.0 / math.sqrt(HEAD_DIM)
    for r in range(dilation):
        for j in range(nq):
            rows = slice(j * DSWA_BLK, (j + 1) * DSWA_BLK)
            q = q_ref[0, r, rows, :]
            kcur = kc_ref[0, r, rows, :]
            vcur = vc_ref[0, r, rows, :]
            if j == 0:
                kprev = kp_ref[0, r]
                vprev = vp_ref[0, r]
                mask_prev = mask_first
            else:
                prows = slice((j - 1) * DSWA_BLK, j * DSWA_BLK)
                kprev = kc_ref[0, r, prows, :]
                vprev = vc_ref[0, r, prows, :]
                mask_prev = mask_band
            dst = pl.ds(j * DSWA_BLK * dilation + r, DSWA_BLK, stride=dilation)
            heads = lambda a: jnp.stack([a[:, h * HEAD_DIM:(h + 1) * HEAD_DIM]
                                         for h in range(DSWA_HEADS)], axis=0)
            q4, kp4, kc4, vp4, vc4 = heads(q), heads(kprev), heads(kcur), heads(vprev), heads(vcur)
            bdot_nt = lambda a, b: jnp.einsum('hqd,hkd->hqk', a, b, preferred_element_type=F32)
            bdot = lambda a, b: jnp.einsum('hqk,hkd->hqd', a, b, preferred_element_type=F32)
            s_p = jnp.where(mask_prev[None], bdot_nt(q4, kp4) * scale, NEG_INF)
            s_c = jnp.where(mask_cur[None], bdot_nt(q4, kc4) * scale, NEG_INF)
            m = jnp.maximum(jnp.max(s_p, axis=-1, keepdims=True), jnp.max(s_c, axis=-1, keepdims=True))
            p_p = jnp.exp(s_p - m)
            p_c = jnp.exp(s_c - m)
            l = jnp.sum(p_p, axis=-1, keepdims=True) + jnp.sum(p_c, axis=-1, keepdims=True)
            o = (bdot(p_p.astype(BF16), vp4) + bdot(p_c.astype(BF16), vc4)) / l
            lse = m + jnp.log(l)
            lse_tile = jnp.zeros((DSWA_BLK, HEAD_DIM), F32)
            for h in range(DSWA_HEADS):
                o_scr[h, dst, :] = o[h]
                lse_tile = jnp.where(col == h, lse[h], lse_tile)
            lse_scr[dst, :] = lse_tile
    for h in range(DSWA_HEADS):
        o_ref[0, :, h * HEAD_DIM:(h + 1) * HEAD_DIM] = o_scr[h].astype(o_ref.dtype)
    lse_ref[0] = lse_scr[...]


def _dswa_group(qkv, group):
    b, dilation, m_len, _ = qkv.shape
    s = dilation * m_len
    assert s % DSWA_TILE == 0 and DSWA_TILE % (dilation * DSWA_BLK) == 0
    rows = DSWA_TILE // dilation
    nq = rows // DSWA_BLK

    def cur(c):
        return pl.BlockSpec((1, dilation, rows, GROUP_W), lambda bi, n: (bi, 0, n, c))

    def prev(c):
        return pl.BlockSpec((1, dilation, DSWA_BLK, GROUP_W),
                            lambda bi, n: (bi, 0, jnp.maximum(n * nq - 1, 0), c))

    return pl.pallas_call(
        functools.partial(_dswa_kernel, dilation=dilation),
        grid=(b, s // DSWA_TILE),
        in_specs=[cur(0), cur(1), prev(1), cur(2), prev(2)],
        out_specs=[pl.BlockSpec((1, DSWA_TILE, GROUP_W), lambda bi, n: (bi, n, 0)),
                   pl.BlockSpec((1, DSWA_TILE, HEAD_DIM), lambda bi, n: (bi, n, 0))],
        out_shape=[jax.ShapeDtypeStruct((b, s, GROUP_W), BF16),
                   jax.ShapeDtypeStruct((b, s, HEAD_DIM), F32)],
        scratch_shapes=[pltpu.VMEM((DSWA_HEADS, DSWA_TILE, HEAD_DIM), F32),
                        pltpu.VMEM((DSWA_TILE, HEAD_DIM), F32)],
        compiler_params=_cparams(("parallel", "arbitrary")),
        name=f"dswa_g{group}",
    )(qkv, qkv, qkv, qkv, qkv)


def _linear_scan(a, b, h0):
    ts, d = a.shape
    a = a.reshape(ts // SUBLANES, SUBLANES, d)
    b = b.reshape(ts // SUBLANES, SUBLANES, d)
    row = lax.broadcasted_iota(jnp.int32, a.shape, 1)
    step = 1
    while step < SUBLANES:
        keep = row >= step
        a_s = jnp.where(keep, pltpu.roll(a, step, 1), 1.0)
        b_s = jnp.where(keep, pltpu.roll(b, step, 1), 0.0)
        b = a * b_s + b
        a = a * a_s
        step *= 2
    hs, carry = [], h0
    for g in range(ts // SUBLANES):
        hg = a[g] * carry + b[g]
        hs.append(hg)
        carry = hg[SUBLANES - 1:SUBLANES, :]
    return jnp.concatenate(hs, axis=0), carry


def _rglru_kernel(xb_ref, halo_ref, w_ref, cw_ref, cb_ref, wa_ref, ba_ref, wx_ref, bx_ref,
                  lam_ref, o_ref, h_ref, *, ts):
    si = pl.program_id(1)

    @pl.when(si == 0)
    def _():
        h_ref[...] = jnp.zeros_like(h_ref)

    z = _dot(xb_ref[0], w_ref[...])
    x = z[:, :D_RNN]
    yg = z[:, D_RNN:]
    halo = _dot(halo_ref[0], w_ref[:, :D_RNN])[BF16_ROWS - 8:, :]
    halo = jnp.where(si > 0, halo, 0.0)
    xfull = jnp.concatenate([halo, x], axis=0)
    xc = cb_ref[...] + cw_ref[CONV_W - 1:CONV_W, :] * x
    for k in range(1, CONV_W):
        xc = xc + cw_ref[CONV_W - 1 - k:CONV_W - k, :] * pltpu.roll(xfull, k, 0)[8:]

    xb = xc.astype(BF16)
    nblk = D_RNN // MXU_TILE
    ra = jnp.concatenate(
        [_dot(xb[:, c * MXU_TILE:(c + 1) * MXU_TILE], wa_ref[c]) for c in range(nblk)], axis=1)
    rx = jnp.concatenate(
        [_dot(xb[:, c * MXU_TILE:(c + 1) * MXU_TILE], wx_ref[c]) for c in range(nblk)], axis=1)
    r = jax.nn.sigmoid(ra + ba_ref[...])
    gate_i = jax.nn.sigmoid(rx + bx_ref[...])
    neg_lam = -lam_ref[...]
    softplus = jnp.maximum(neg_lam, 0.0) + jnp.log1p(jnp.exp(-jnp.abs(neg_lam)))
    log_a = (-LRU_C) * r * softplus
    a = jnp.exp(log_a)
    bterm = jnp.sqrt(1.0 - a * a) * gate_i * xc

    h, carry = _linear_scan(a, bterm, h_ref[0:1, :])
    h_ref[0:1, :] = carry
    o_ref[0] = (h * jax.nn.gelu(yg, approximate=True)).astype(o_ref.dtype)


def _rglru(xb, w_rnn, conv_w, conv_b, wa_bd, ba, wx_bd, bx, lam, ts=256):
    b, s, d = xb.shape
    assert s % ts == 0
    full = lambda shape: pl.BlockSpec(shape, lambda bi, si: (0,) * len(shape))
    return pl.pallas_call(
        functools.partial(_rglru_kernel, ts=ts),
        grid=(b, s // ts),
        in_specs=[pl.BlockSpec((1, ts, d), lambda bi, si: (bi, si, 0)),
                  pl.BlockSpec((1, BF16_ROWS, d),
                               lambda bi, si: (bi, jnp.maximum(si * (ts // BF16_ROWS) - 1, 0), 0)),
                  full(w_rnn.shape),
                  full((CONV_W, D_RNN)), full((1, D_RNN)),
                  full(wa_bd.shape), full((1, D_RNN)),
                  full(wx_bd.shape), full((1, D_RNN)), full((1, D_RNN))],
        out_specs=pl.BlockSpec((1, ts, D_RNN), lambda bi, si: (bi, si, 0)),
        out_shape=jax.ShapeDtypeStruct((b, s, D_RNN), BF16),
        scratch_shapes=[pltpu.VMEM((8, D_RNN), F32)],
        compiler_params=_cparams(("parallel", "arbitrary")),
        name="rglru",
    )(xb, xb, w_rnn, conv_w, conv_b, wa_bd, ba, wx_bd, bx, lam)


def _memattn_kernel(q_ref, kv_ref, o_ref):
    scale = 1.0 / math.sqrt(HEAD_DIM)
    for h in range(MEM_HEADS):
        hs = slice(h * HEAD_DIM, (h + 1) * HEAD_DIM)
        vs = slice(MEM_WIDTH + h * HEAD_DIM, MEM_WIDTH + (h + 1) * HEAD_DIM)
        s = _dot_nt(q_ref[0, :, hs], kv_ref[0, :, hs]) * scale
        m = jnp.max(s, axis=-1, keepdims=True)
        p = jnp.exp(s - m)
        l = jnp.sum(p, axis=-1, keepdims=True)
        o_ref[0, :, hs] = (_dot(p.astype(BF16), kv_ref[0, :, vs]) / l).astype(o_ref.dtype)


def _memattn(mq, kv, ts=1024):
    b, s, _ = mq.shape
    mem_len = kv.shape[1]
    return pl.pallas_call(
        _memattn_kernel,
        grid=(b, s // ts),
        in_specs=[pl.BlockSpec((1, ts, MEM_WIDTH), lambda bi, si: (bi, si, 0)),
                  pl.BlockSpec((1, mem_len, 2 * MEM_WIDTH), lambda bi, si: (bi, 0, 0))],
        out_specs=pl.BlockSpec((1, ts, MEM_WIDTH), lambda bi, si: (bi, si, 0)),
        out_shape=jax.ShapeDtypeStruct((b, s, MEM_WIDTH), BF16),
        compiler_params=_cparams(("parallel", "arbitrary")),
        name="memattn",
    )(mq, kv)


def _mix_kernel(o0_ref, o1_ref, o2_ref, l0_ref, l1_ref, l2_ref, rec_ref, memo_ref, x_ref,
                wg_ref, bg_ref, wa_ref, wl_ref, wm_ref, wo_ref, g_ref, b_ref, x1_ref, x1b_ref):
    l0, l1, l2 = l0_ref[...], l1_ref[...], l2_ref[...]
    mx = jnp.maximum(jnp.maximum(l0, l1), l2)
    e0, e1, e2 = jnp.exp(l0 - mx), jnp.exp(l1 - mx), jnp.exp(l2 - mx)
    inv = 1.0 / (e0 + e1 + e2)
    w0, w1, w2 = e0 * inv, e1 * inv, e2 * inv
    parts = []
    for h in range(DSWA_HEADS):
        hs = slice(h * HEAD_DIM, (h + 1) * HEAD_DIM)
        parts.append((w0[:, h:h + 1] * o0_ref[:, hs].astype(F32)
                      + w1[:, h:h + 1] * o1_ref[:, hs].astype(F32)
                      + w2[:, h:h + 1] * o2_ref[:, hs].astype(F32)).astype(BF16))
    attn = jnp.concatenate(parts, axis=1)
    d = D_MODEL
    gl = _dot(x_ref[...].astype(BF16), wg_ref[...])
    gate = lambda j: jax.nn.sigmoid(gl[:, j * d:(j + 1) * d] + bg_ref[j:j + 1, :])
    merged = (gate(0) * _dot(attn, wa_ref[...])
              + gate(1) * _dot(rec_ref[...], wl_ref[...])
              + gate(2) * _dot(memo_ref[...], wm_ref[...]))
    mix = _dot(merged.astype(BF16), wo_ref[...])
    x1 = _layernorm(ALPHA * x_ref[...] + mix, g_ref[...], b_ref[...])
    x1_ref[...] = x1
    x1b_ref[...] = x1.astype(BF16)


def _mix(os, lses, rec, memo, x, w_gate, b_gate, wa, wl, wm, wo, g, bta, tt=512):
    t = x.shape[0]
    rows = lambda w: pl.BlockSpec((tt, w), lambda i: (i, 0))
    full = lambda a: pl.BlockSpec(a.shape, lambda i: (0,) * a.ndim)
    return pl.pallas_call(
        _mix_kernel,
        grid=(t // tt,),
        in_specs=[rows(GROUP_W)] * 3 + [rows(HEAD_DIM)] * 3 + [
            rows(D_RNN), rows(MEM_WIDTH), rows(D_MODEL), full(w_gate),
            full(b_gate), full(wa), full(wl), full(wm), full(wo), full(g), full(bta)],
        out_specs=[rows(D_MODEL), rows(D_MODEL)],
        out_shape=[jax.ShapeDtypeStruct((t, D_MODEL), F32), jax.ShapeDtypeStruct((t, D_MODEL), BF16)],
        compiler_params=_cparams(("parallel",)),
        name="mix_ln1",
    )(*os, *lses, rec, memo, x, w_gate, b_gate, wa, wl, wm, wo, g, bta)


def _peer_scores_kernel(x_ref, wq_ref, kbd_ref, o_ref):
    q = _dot(x_ref[...], wq_ref[...]).astype(BF16)
    for h in range(PEER_HEADS):
        hs = slice(h * PEER_KEY_DIM, (h + 1) * PEER_KEY_DIM)
        o_ref[hs, :] = _dot_nt(kbd_ref[h], q[:, hs])


def _peer_scores(x1b, wq, kbd, tt=512):
    t = x1b.shape[0]
    w = PEER_HEADS * PEER_KEY_DIM
    return pl.pallas_call(
        _peer_scores_kernel,
        grid=(t // tt,),
        in_specs=[pl.BlockSpec((tt, D_MODEL), lambda i: (i, 0)),
                  pl.BlockSpec(wq.shape, lambda i: (0, 0)),
                  pl.BlockSpec(kbd.shape, lambda i: (0, 0, 0))],
        out_specs=pl.BlockSpec((w, tt), lambda i: (0, i)),
        out_shape=jax.ShapeDtypeStruct((w, t), F32),
        compiler_params=_cparams(("parallel",)),
        name="peer_scores",
    )(x1b, wq, kbd)


_PEER_CANDS = tuple((p, q) for p in range(PEER_TOPK) for q in range(PEER_TOPK)
                    if (p + 1) * (q + 1) <= PEER_TOPK)


def _top_positions(s, exact_ties):
    key = lax.broadcasted_iota(jnp.int32, s.shape, 0).astype(F32)
    pos = jnp.full(s.shape, float(PEER_TOPK), F32)
    work = s
    tops = []
    for p in range(PEER_TOPK):
        mx = jnp.max(work, axis=0, keepdims=True)
        if exact_ties:
            first = jnp.min(jnp.where(work == mx, key, float(N_KEYS)), axis=0, keepdims=True)
            hit = key == first
        else:
            hit = work == mx
        pos = jnp.where(hit, float(p), pos)
        work = jnp.where(hit, -jnp.inf, work)
        tops.append(mx)
    return pos, tops


def _store_lane_tiles(ref, h, val):
    for c in range(ref.shape[0]):
        ref[c, h * N_KEYS:(h + 1) * N_KEYS, :] = val[:, c * LANES:(c + 1) * LANES]


def _peer_select_body(sc_ref, ra_ref, ea_ref, cb_ref, eb_ref, exact_ties):
    tt = sc_ref.shape[1]
    kg = N_KEYS // BF16_ROWS
    pos_b, tops_a, tops_b = [], [], []
    marked = jnp.zeros((1, tt), F32)
    for h in range(PEER_HEADS):
        base = h * PEER_KEY_DIM
        pa, ta = _top_positions(sc_ref[base:base + N_KEYS, :], exact_ties)
        pb, tb = _top_positions(sc_ref[base + N_KEYS:base + 2 * N_KEYS, :], exact_ties)
        _store_lane_tiles(ra_ref, h, pa)
        pos_b.append(pb)
        tops_a.append(ta)
        tops_b.append(tb)
        if not exact_ties:
            for pos in (pa, pb):
                n_marked = jnp.sum(jnp.where(pos < float(PEER_TOPK), 1.0, 0.0), axis=0, keepdims=True)
                marked = jnp.maximum(marked, n_marked)
    a_top = [jnp.concatenate([tops_a[h][p] for h in range(PEER_HEADS)], axis=0)
             for p in range(PEER_TOPK)]
    b_top = [jnp.concatenate([tops_b[h][q] for h in range(PEER_HEADS)], axis=0)
             for q in range(PEER_TOPK)]
    sums = [a_top[p] + b_top[q] for p, q in _PEER_CANDS]
    n = len(sums)
    rank = [jnp.zeros((PEER_HEADS, tt), F32) for _ in range(n)]
    for c in range(n):
        for c2 in range(c):
            ge = sums[c2] >= sums[c]
            rank[c] = rank[c] + jnp.where(ge, 1.0, 0.0)
            rank[c2] = rank[c2] + jnp.where(ge, 0.0, 1.0)
    cnt = [jnp.zeros((PEER_HEADS, tt), F32) for _ in range(PEER_TOPK)]
    zsum = jnp.zeros((PEER_HEADS, tt), F32)
    for c, (p, q) in enumerate(_PEER_CANDS):
        sel = rank[c] < float(PEER_TOPK)
        cnt[q] = cnt[q] + jnp.where(sel, 1.0, 0.0)
        zsum = zsum + jnp.where(sel, jnp.exp(sums[c] - sums[0]), 0.0)
    inv_z = 1.0 / zsum
    for h in range(PEER_HEADS):
        base = h * PEER_KEY_DIM
        rows = slice(h * N_KEYS, (h + 1) * N_KEYS)
        posb = pos_b[h].astype(BF16).reshape(kg, BF16_ROWS, tt)
        cb = jnp.zeros((kg, BF16_ROWS, tt), BF16)
        for q in range(PEER_TOPK):
            cnt_q = jnp.broadcast_to(cnt[q][h:h + 1, :], (BF16_ROWS, tt)).astype(BF16)
            cb = cb + jnp.where(posb == float(q), cnt_q[None], jnp.zeros_like(cb))
        cb_ref[h * kg:(h + 1) * kg] = cb
        _store_lane_tiles(ea_ref, h, jnp.exp(sc_ref[base:base + N_KEYS, :] - a_top[0][h:h + 1, :])
                          * inv_z[h:h + 1, :])
        eb = jnp.exp(sc_ref[base + N_KEYS:base + 2 * N_KEYS, :] - b_top[0][h:h + 1, :])
        eb_ref[h * kg:(h + 1) * kg] = eb.astype(BF16).reshape(kg, BF16_ROWS, tt)
    return marked


def _peer_select_kernel(sc_ref, ra_ref, ea_ref, cb_ref, eb_ref):
    marked = _peer_select_body(sc_ref, ra_ref, ea_ref, cb_ref, eb_ref, exact_ties=False)

    @pl.when(jnp.max(marked) > float(PEER_TOPK))
    def _():
        _peer_select_body(sc_ref, ra_ref, ea_ref, cb_ref, eb_ref, exact_ties=True)


def _peer_select(sc, tt=256):
    w, t = sc.shape
    rows = PEER_HEADS * N_KEYS
    out_spec_a = pl.BlockSpec((tt // LANES, rows, LANES), lambda i: (i, 0, 0))
    shape_a = jax.ShapeDtypeStruct((t // LANES, rows, LANES), F32)
    out_spec_b = pl.BlockSpec((rows // BF16_ROWS, BF16_ROWS, tt), lambda i: (0, 0, i))
    shape_b = jax.ShapeDtypeStruct((rows // BF16_ROWS, BF16_ROWS, t), BF16)
    return pl.pallas_call(
        _peer_select_kernel,
        grid=(t // tt,),
        in_specs=[pl.BlockSpec((w, tt), lambda i: (0, i))],
        out_specs=[out_spec_a, out_spec_a, out_spec_b, out_spec_b],
        out_shape=[shape_a, shape_a, shape_b, shape_b],
        compiler_params=_cparams(("parallel",)),
        name="peer_select",
    )(sc)


def _row_on_sublanes(ref, h, i, lane0, lanes):
    tiles = range(lane0 // LANES, (lane0 + lanes) // LANES)
    return jnp.concatenate([ref[c, h, pl.ds(i, BF16_ROWS, stride=0), :] for c in tiles], axis=1)


def _masked_acts(hmat, ra_ref, ea_ref, cb_ref, eb_ref, key0):
    tt = hmat.shape[1]
    kg = N_KEYS // BF16_ROWS
    ys = []
    for k in range(hmat.shape[0] // N_KEYS):
        rows = slice(k * N_KEYS, (k + 1) * N_KEYS)
        ws = []
        for l0 in range(0, tt, PEER_MASK_LANES):
            ls = slice(l0, l0 + PEER_MASK_LANES)
            w = jnp.zeros((kg, BF16_ROWS, PEER_MASK_LANES), BF16)
            for h in range(PEER_HEADS):
                hs = slice(h * kg, (h + 1) * kg)
                ra = _row_on_sublanes(ra_ref, h, key0 + k, l0, PEER_MASK_LANES).astype(BF16)
                ea = _row_on_sublanes(ea_ref, h, key0 + k, l0, PEER_MASK_LANES).astype(BF16)
                w = w + eb_ref[hs, :, ls] * jnp.where(cb_ref[hs, :, ls] > ra[None], ea[None],
                                                      jnp.zeros_like(w))
            ws.append(w.reshape(N_KEYS, PEER_MASK_LANES))
        ys.append(jnp.concatenate(ws, axis=1) * _gelu_erf(hmat[rows, :]).astype(BF16))
    return jnp.concatenate(ys, axis=0)


def _peer_mix_kernel(x1b_ref, x1_ref, u_ref, vt_ref, ra_ref, ea_ref, cb_ref, eb_ref,
                     g_ref, b_ref, o_ref, acc_ref, *, ib):
    e = pl.program_id(1)

    @pl.when(e == 0)
    def _():
        acc_ref[...] = jnp.zeros_like(acc_ref)

    x1b = x1b_ref[...]
    acc = acc_ref[...]
    assert sum(PEER_CHUNK_KEYS) == ib
    starts = [sum(PEER_CHUNK_KEYS[:c]) for c in range(len(PEER_CHUNK_KEYS) + 1)]
    chunk_rows = lambda c: slice(starts[c] * N_KEYS, starts[c + 1] * N_KEYS)
    n_chunks = len(PEER_CHUNK_KEYS)
    h_next = _dot_nt(u_ref[chunk_rows(0), :], x1b)
    for c in range(n_chunks):
        hc = h_next
        if c + 1 < n_chunks:
            h_next = _dot_nt(u_ref[chunk_rows(c + 1), :], x1b)
        y = _masked_acts(hc, ra_ref, ea_ref, cb_ref, eb_ref, starts[c])
        acc = acc + _dot(vt_ref[:, chunk_rows(c)], y)
    acc_ref[...] = acc

    @pl.when(e == pl.num_programs(1) - 1)
    def _():
        o_ref[...] = _layernorm(ALPHA * x1_ref[...] + acc_ref[...].T, g_ref[...], b_ref[...])


def _peer_mix(x1b, x1, u, vt, ra, ea, cb, eb, g, bta, tt=512, te=sum(PEER_CHUNK_KEYS) * N_KEYS):
    t = x1.shape[0]
    ib = te // N_KEYS
    ne = N_EXPERTS // te
    rows = PEER_HEADS * N_KEYS
    return pl.pallas_call(
        functools.partial(_peer_mix_kernel, ib=ib),
        grid=(t // tt, ne),
        in_specs=[pl.BlockSpec((tt, D_MODEL), lambda i, e: (i, 0)),
                  pl.BlockSpec((tt, D_MODEL), lambda i, e: (i, 0)),
                  pl.BlockSpec((te, D_MODEL), lambda i, e: (e, 0)),
                  pl.BlockSpec((D_MODEL, te), lambda i, e: (0, e)),
                  pl.BlockSpec((tt // LANES, PEER_HEADS, ib, LANES), lambda i, e: (i, 0, e, 0)),
                  pl.BlockSpec((tt // LANES, PEER_HEADS, ib, LANES), lambda i, e: (i, 0, e, 0)),
                  pl.BlockSpec((rows // BF16_ROWS, BF16_ROWS, tt), lambda i, e: (0, 0, i)),
                  pl.BlockSpec((rows // BF16_ROWS, BF16_ROWS, tt), lambda i, e: (0, 0, i)),
                  pl.BlockSpec((1, D_MODEL), lambda i, e: (0, 0)),
                  pl.BlockSpec((1, D_MODEL), lambda i, e: (0, 0))],
        out_specs=pl.BlockSpec((tt, D_MODEL), lambda i, e: (i, 0)),
        out_shape=jax.ShapeDtypeStruct((t, D_MODEL), F32),
        scratch_shapes=[pltpu.VMEM((D_MODEL, tt), F32)],
        compiler_params=_cparams(("parallel", "arbitrary")),
        name="peer_mix_ln2",
    )(x1b, x1, u, vt, ra, ea, cb, eb, g, bta)


def _block_diag(w, per_tile):
    n, k, _ = w.shape
    w = w.reshape(n // per_tile, per_tile, k, k)
    eye = jnp.eye(per_tile, dtype=w.dtype)
    out = jnp.einsum('tpij,pq->tpiqj', w, eye)
    return out.reshape(n // per_tile, per_tile * k, per_tile * k)


def _layer(x, mem, w_in, b_gate, conv_w, conv_b, lru_wa, lru_ba, lru_wx, lru_bx, lru_lambda,
           w_mem_kv, w_br_attn, w_br_lru, w_br_mem, w_out, ln1_g, ln1_b,
           peer_wq, peer_keys, peer_u, peer_v, ln2_g, ln2_b):
    b, s, d = x.shape
    t = b * s
    xf = x.reshape(t, d)
    xb = xf.astype(BF16)
    w_in_b = w_in.astype(BF16)
    row = lambda a: a.reshape(1, -1).astype(F32)

    off_rnn = 3 * DSWA_WIDTH
    off_mq = off_rnn + 2 * D_RNN
    off_gl = off_mq + MEM_WIDTH
    mq = _matmul(xb, w_in_b, off_mq, MEM_WIDTH, BF16, tn=512).reshape(b, s, MEM_WIDTH)

    os, lses = [], []
    for gi, dil in enumerate(DSWA_DILATIONS):
        xp = xb if dil == 1 else _permute_rows(x, dil).reshape(t, d)
        w_g = jnp.concatenate([w_in_b[:, part * DSWA_WIDTH + gi * GROUP_W:
                                      part * DSWA_WIDTH + (gi + 1) * GROUP_W] for part in range(3)], axis=1)
        qkv = _matmul(xp, w_g, 0, 3 * GROUP_W, BF16, tn=3 * GROUP_W).reshape(b, dil, s // dil, 3 * GROUP_W)
        o_g, lse_g = _dswa_group(qkv, gi)
        os.append(o_g.reshape(t, GROUP_W))
        lses.append(lse_g.reshape(t, HEAD_DIM))

    per_tile = MXU_TILE // LRU_BW
    wa_bd = _block_diag(lru_wa, per_tile).astype(BF16)
    wx_bd = _block_diag(lru_wx, per_tile).astype(BF16)
    rec = _rglru(xb.reshape(b, s, d), w_in_b[:, off_rnn:off_mq], conv_w.astype(F32), row(conv_b),
                 wa_bd, row(lru_ba), wx_bd, row(lru_bx), row(lru_lambda)).reshape(t, D_RNN)

    mem_len = mem.shape[1]
    kv = _matmul(mem.reshape(b * mem_len, d).astype(BF16), w_mem_kv.astype(BF16), 0,
                 2 * MEM_WIDTH, BF16, tn=1024).reshape(b, mem_len, 2 * MEM_WIDTH)
    memo = _memattn(mq, kv).reshape(t, MEM_WIDTH)

    x1, x1b = _mix(os, lses, rec, memo, xf, w_in_b[:, off_gl:], b_gate.astype(F32), w_br_attn.astype(BF16),
                   w_br_lru.astype(BF16), w_br_mem.astype(BF16), w_out.astype(BF16),
                   row(ln1_g), row(ln1_b))

    kbd = _block_diag(peer_keys.reshape(PEER_HEADS * 2, N_KEYS, PEER_KEY_DIM // 2), 2).astype(BF16)
    scores = _peer_scores(x1b, peer_wq.astype(BF16), kbd)

    ra, ea, cb, eb = _peer_select(scores)
    ra = ra.reshape(t // LANES, PEER_HEADS, N_KEYS, LANES)
    ea = ea.reshape(t // LANES, PEER_HEADS, N_KEYS, LANES)

    out = _peer_mix(x1b, x1, peer_u.astype(BF16), peer_v.T.astype(BF16), ra, ea, cb, eb,
                    row(ln2_g), row(ln2_b))
    return out.reshape(b, s, d)


def kernel(x, mem, w_in, b_gate, conv_w, conv_b, lru_wa, lru_ba, lru_wx, lru_bx, lru_lambda, w_mem_kv, w_br_attn, w_br_lru, w_br_mem, w_out, ln1_g, ln1_b, peer_wq, peer_keys, peer_u, peer_v, ln2_g, ln2_b):
    h = x.astype(F32)
    depth = w_in.shape[0]
    for l in range(depth):
        h = _layer(h, mem, w_in[l], b_gate[l], conv_w[l], conv_b[l], lru_wa[l], lru_ba[l],
                   lru_wx[l], lru_bx[l], lru_lambda[l], w_mem_kv[l], w_br_attn[l], w_br_lru[l],
                   w_br_mem[l], w_out[l], ln1_g[l], ln1_b[l], peer_wq[l], peer_keys[l],
                   peer_u[l], peer_v[l], ln2_g[l], ln2_b[l])
    return h.astype(x.dtype)
```

```python
import functools
import math

import jax
import jax.numpy as jnp
from jax import lax
from jax.experimental import pallas as pl
from jax.experimental.pallas import tpu as pltpu

F32 = jnp.float32
BF16 = jnp.bfloat16

D_MODEL = 1024
N_GROUPS = 3
DSWA_DILATIONS = (1, 4, 16)
DSWA_HEADS = 4
HEAD_DIM = 128
DSWA_BLK = 128
DSWA_TILE = 2048
GROUP_W = DSWA_HEADS * HEAD_DIM
DSWA_WIDTH = N_GROUPS * GROUP_W
D_RNN = 1024
LRU_BLOCKS = 16
LRU_BW = D_RNN // LRU_BLOCKS
CONV_W = 4
LRU_C = 8.0
MEM_HEADS = 4
MEM_WIDTH = MEM_HEADS * HEAD_DIM
PEER_HEADS = 8
PEER_KEY_DIM = 256
N_KEYS = 128
N_EXPERTS = N_KEYS * N_KEYS
PEER_TOPK = 16
PEER_CHUNK_KEYS = (8,) * 2
BF16_ROWS = 16
PEER_MASK_LANES = 256
ALPHA = 2.0 ** 0.25
LN_EPS = 1e-5
NEG_INF = -1e30

LANES = 128
SUBLANES = 8
MXU_TILE = 256
VMEM_LIMIT = 56 * 1024 * 1024


def _cparams(sem):
    return pltpu.CompilerParams(dimension_semantics=sem, vmem_limit_bytes=VMEM_LIMIT)


def _dot(a, b):
    return jnp.dot(a, b, preferred_element_type=F32)


def _dot_nt(a, b):
    return lax.dot_general(a, b, (((1,), (1,)), ((), ())), preferred_element_type=F32)


def _gelu_erf(x):
    return 0.5 * x * (1.0 + lax.erf(x * (1.0 / math.sqrt(2.0))))


def _layernorm(h, g, b):
    mu = jnp.mean(h, axis=-1, keepdims=True)
    c = h - mu
    var = jnp.mean(c * c, axis=-1, keepdims=True)
    return c * lax.rsqrt(var + LN_EPS) * g + b


def _matmul_kernel(x_ref, w_ref, o_ref):
    o_ref[...] = _dot(x_ref[...], w_ref[...]).astype(o_ref.dtype)


def _matmul(x, w, col_off, n_cols, out_dtype, tm=1024, tn=1024):
    m, k = x.shape
    tn = min(tn, n_cols)
    tm = min(tm, m)
    assert m % tm == 0 and n_cols % tn == 0 and col_off % tn == 0
    off = col_off // tn
    return pl.pallas_call(
        _matmul_kernel,
        grid=(m // tm, n_cols // tn),
        in_specs=[pl.BlockSpec((tm, k), lambda i, j: (i, 0)),
                  pl.BlockSpec((k, tn), lambda i, j: (0, off + j))],
        out_specs=pl.BlockSpec((tm, tn), lambda i, j: (i, j)),
        out_shape=jax.ShapeDtypeStruct((m, n_cols), out_dtype),
        compiler_params=_cparams(("parallel", "arbitrary")),
        name="matmul",
    )(x, w)


def _permute_kernel(x_ref, o_ref, *, dilation):
    rows = x_ref.shape[1] // dilation
    for bi in range(x_ref.shape[0]):
        for r in range(dilation):
            o_ref[bi, r] = x_ref[bi, pl.ds(r, rows, stride=dilation), :].astype(o_ref.dtype)


def _permute_rows(x, dilation):
    b, s, d = x.shape
    m_len = s // dilation
    return pl.pallas_call(
        functools.partial(_permute_kernel, dilation=dilation),
        grid=(s // DSWA_TILE, d // LANES),
        in_specs=[pl.BlockSpec((b, DSWA_TILE, LANES), lambda n, c: (0, n, c))],
        out_specs=pl.BlockSpec((b, dilation, DSWA_TILE // dilation, LANES),
                               lambda n, c: (0, 0, n, c)),
        out_shape=jax.ShapeDtypeStruct((b, dilation, m_len, d), BF16),
        compiler_params=_cparams(("parallel", "parallel")),
        name=f"permute_d{dilation}",
    )(x)


def _dswa_kernel(q_ref, kc_ref, kp_ref, vc_ref, vp_ref, o_ref, lse_ref, o_scr, lse_scr, *, dilation):
    nq = q_ref.shape[2] // DSWA_BLK
    row = lax.broadcasted_iota(jnp.int32, (DSWA_BLK, DSWA_BLK), 0)
    col = lax.broadcasted_iota(jnp.int32, (DSWA_BLK, DSWA_BLK), 1)
    mask_cur = col <= row
    mask_band = col >= row
    mask_first = col >= row + jnp.where(pl.program_id(1) == 0, DSWA_BLK, 0)
    scale = 1.0 / math.sqrt(HEAD_DIM)
    for r in range(dilation):
        for j in range(nq):
            rows = slice(j * DSWA_BLK, (j + 1) * DSWA_BLK)
            q = q_ref[0, r, rows, :]
            kcur = kc_ref[0, r, rows, :]
            vcur = vc_ref[0, r, rows, :]
            if j == 0:
                kprev = kp_ref[0, r]
                vprev = vp_ref[0, r]
                mask_prev = mask_first
            else:
                prows = slice((j - 1) * DSWA_BLK, j * DSWA_BLK)
                kprev = kc_ref[0, r, prows, :]
                vprev = vc_ref[0, r, prows, :]
                mask_prev = mask_band
            dst = pl.ds(j * DSWA_BLK * dilation + r, DSWA_BLK, stride=dilation)
            heads = lambda a: jnp.stack([a[:, h * HEAD_DIM:(h + 1) * HEAD_DIM]
                                         for h in range(DSWA_HEADS)], axis=0)
            q4, kp4, kc4, vp4, vc4 = heads(q), heads(kprev), heads(kcur), heads(vprev), heads(vcur)
            bdot_nt = lambda a, b: jnp.einsum('hqd,hkd->hqk', a, b, preferred_element_type=F32)
            bdot = lambda a, b: jnp.einsum('hqk,hkd->hqd', a, b, preferred_element_type=F32)
            s_p = jnp.where(mask_prev[None], bdot_nt(q4, kp4) * scale, NEG_INF)
            s_c = jnp.where(mask_cur[None], bdot_nt(q4, kc4) * scale, NEG_INF)
            m = jnp.maximum(jnp.max(s_p, axis=-1, keepdims=True), jnp.max(s_c, axis=-1, keepdims=True))
            p_p = jnp.exp(s_p - m)
            p_c = jnp.exp(s_c - m)
            l = jnp.sum(p_p, axis=-1, keepdims=True) + jnp.sum(p_c, axis=-1, keepdims=True)
            o = (bdot(p_p.astype(BF16), vp4) + bdot(p_c.astype(BF16), vc4)) / l
            lse = m + jnp.log(l)
            lse_tile = jnp.zeros((DSWA_BLK, HEAD_DIM), F32)
            for h in range(DSWA_HEADS):
                o_scr[h, dst, :] = o[h]
                lse_tile = jnp.where(col == h, lse[h], lse_tile)
            lse_scr[dst, :] = lse_tile
    for h in range(DSWA_HEADS):
        o_ref[0, :, h * HEAD_DIM:(h + 1) * HEAD_DIM] = o_scr[h].astype(o_ref.dtype)
    lse_ref[0] = lse_scr[...]


def _dswa_group(qkv, group):
    b, dilation, m_len, _ = qkv.shape
    s = dilation * m_len
    assert s % DSWA_TILE == 0 and DSWA_TILE % (dilation * DSWA_BLK) == 0
    rows = DSWA_TILE // dilation
    nq = rows // DSWA_BLK

    def cur(c):
        return pl.BlockSpec((1, dilation, rows, GROUP_W), lambda bi, n: (bi, 0, n, c))

    def prev(c):
        return pl.BlockSpec((1, dilation, DSWA_BLK, GROUP_W),
                            lambda bi, n: (bi, 0, jnp.maximum(n * nq - 1, 0), c))

    return pl.pallas_call(
        functools.partial(_dswa_kernel, dilation=dilation),
        grid=(b, s // DSWA_TILE),
        in_specs=[cur(0), cur(1), prev(1), cur(2), prev(2)],
        out_specs=[pl.BlockSpec((1, DSWA_TILE, GROUP_W), lambda bi, n: (bi, n, 0)),
                   pl.BlockSpec((1, DSWA_TILE, HEAD_DIM), lambda bi, n: (bi, n, 0))],
        out_shape=[jax.ShapeDtypeStruct((b, s, GROUP_W), BF16),
                   jax.ShapeDtypeStruct((b, s, HEAD_DIM), F32)],
        scratch_shapes=[pltpu.VMEM((DSWA_HEADS, DSWA_TILE, HEAD_DIM), F32),
                        pltpu.VMEM((DSWA_TILE, HEAD_DIM), F32)],
        compiler_params=_cparams(("parallel", "arbitrary")),
        name=f"dswa_g{group}",
    )(qkv, qkv, qkv, qkv, qkv)


def _linear_scan(a, b, h0):
    ts, d = a.shape
    a = a.reshape(ts // SUBLANES, SUBLANES, d)
    b = b.reshape(ts // SUBLANES, SUBLANES, d)
    row = lax.broadcasted_iota(jnp.int32, a.shape, 1)
    step = 1
    while step < SUBLANES:
        keep = row >= step
        a_s = jnp.where(keep, pltpu.roll(a, step, 1), 1.0)
        b_s = jnp.where(keep, pltpu.roll(b, step, 1), 0.0)
        b = a * b_s + b
        a = a * a_s
        step *= 2
    hs, carry = [], h0
    for g in range(ts // SUBLANES):
        hg = a[g] * carry + b[g]
        hs.append(hg)
        carry = hg[SUBLANES - 1:SUBLANES, :]
    return jnp.concatenate(hs, axis=0), carry


def _rglru_kernel(xb_ref, halo_ref, w_ref, cw_ref, cb_ref, wa_ref, ba_ref, wx_ref, bx_ref,
                  lam_ref, o_ref, h_ref, *, ts):
    si = pl.program_id(1)

    @pl.when(si == 0)
    def _():
        h_ref[...] = jnp.zeros_like(h_ref)

    z = _dot(xb_ref[0], w_ref[...])
    x = z[:, :D_RNN]
    yg = z[:, D_RNN:]
    halo = _dot(halo_ref[0], w_ref[:, :D_RNN])[BF16_ROWS - SUBLANES:, :]
    halo = jnp.where(si > 0, halo, 0.0)
    xfull = jnp.concatenate([halo, x], axis=0)
    xc = cb_ref[...] + cw_ref[CONV_W - 1:CONV_W, :] * x
    for k in range(1, CONV_W):
        xc = xc + cw_ref[CONV_W - 1 - k:CONV_W - k, :] * pltpu.roll(xfull, k, 0)[SUBLANES:]

    xb = xc.astype(BF16)
    nblk = D_RNN // MXU_TILE
    ra = jnp.concatenate(
        [_dot(xb[:, c * MXU_TILE:(c + 1) * MXU_TILE], wa_ref[c]) for c in range(nblk)], axis=1)
    rx = jnp.concatenate(
        [_dot(xb[:, c * MXU_TILE:(c + 1) * MXU_TILE], wx_ref[c]) for c in range(nblk)], axis=1)
    r = jax.nn.sigmoid(ra + ba_ref[...])
    gate_i = jax.nn.sigmoid(rx + bx_ref[...])
    neg_lam = -lam_ref[...]
    softplus = jnp.maximum(neg_lam, 0.0) + jnp.log1p(jnp.exp(-jnp.abs(neg_lam)))
    log_a = (-LRU_C) * r * softplus
    a = jnp.exp(log_a)
    bterm = jnp.sqrt(1.0 - a * a) * gate_i * xc

    h, carry = _linear_scan(a, bterm, h_ref[0:1, :])
    h_ref[0:1, :] = carry
    o_ref[0] = (h * jax.nn.gelu(yg, approximate=True)).astype(o_ref.dtype)


def _rglru(xb, w_rnn, conv_w, conv_b, wa_bd, ba, wx_bd, bx, lam, ts=512):
    b, s, d = xb.shape
    assert s % ts == 0
    full = lambda shape: pl.BlockSpec(shape, lambda bi, si: (0,) * len(shape))
    return pl.pallas_call(
        functools.partial(_rglru_kernel, ts=ts),
        grid=(b, s // ts),
        in_specs=[pl.BlockSpec((1, ts, d), lambda bi, si: (bi, si, 0)),
                  pl.BlockSpec((1, BF16_ROWS, d),
                               lambda bi, si: (bi, jnp.maximum(si * (ts // BF16_ROWS) - 1, 0), 0)),
                  full(w_rnn.shape),
                  full((CONV_W, D_RNN)), full((1, D_RNN)),
                  full(wa_bd.shape), full((1, D_RNN)),
                  full(wx_bd.shape), full((1, D_RNN)), full((1, D_RNN))],
        out_specs=pl.BlockSpec((1, ts, D_RNN), lambda bi, si: (bi, si, 0)),
        out_shape=jax.ShapeDtypeStruct((b, s, D_RNN), BF16),
        scratch_shapes=[pltpu.VMEM((SUBLANES, D_RNN), F32)],
        compiler_params=_cparams(("parallel", "arbitrary")),
        name="rglru",
    )(xb, xb, w_rnn, conv_w, conv_b, wa_bd, ba, wx_bd, bx, lam)


def _memattn_kernel(q_ref, kv_ref, o_ref):
    scale = 1.0 / math.sqrt(HEAD_DIM)
    for h in range(MEM_HEADS):
        hs = slice(h * HEAD_DIM, (h + 1) * HEAD_DIM)
        vs = slice(MEM_WIDTH + h * HEAD_DIM, MEM_WIDTH + (h + 1) * HEAD_DIM)
        s = _dot_nt(q_ref[0, :, hs], kv_ref[0, :, hs]) * scale
        m = jnp.max(s, axis=-1, keepdims=True)
        p = jnp.exp(s - m)
        l = jnp.sum(p, axis=-1, keepdims=True)
        o_ref[0, :, hs] = (_dot(p.astype(BF16), kv_ref[0, :, vs]) / l).astype(o_ref.dtype)


def _memattn(mq, kv, ts=1024):
    b, s, _ = mq.shape
    mem_len = kv.shape[1]
    return pl.pallas_call(
        _memattn_kernel,
        grid=(b, s // ts),
        in_specs=[pl.BlockSpec((1, ts, MEM_WIDTH), lambda bi, si: (bi, si, 0)),
                  pl.BlockSpec((1, mem_len, 2 * MEM_WIDTH), lambda bi, si: (bi, 0, 0))],
        out_specs=pl.BlockSpec((1, ts, MEM_WIDTH), lambda bi, si: (bi, si, 0)),
        out_shape=jax.ShapeDtypeStruct((b, s, MEM_WIDTH), BF16),
        compiler_params=_cparams(("parallel", "arbitrary")),
        name="memattn",
    )(mq, kv)


def _mix_kernel(o0_ref, o1_ref, o2_ref, l0_ref, l1_ref, l2_ref, rec_ref, memo_ref, x_ref,
                wg_ref, bg_ref, wa_ref, wl_ref, wm_ref, wo_ref, g_ref, b_ref, x1_ref, x1b_ref):
    l0, l1, l2 = l0_ref[...], l1_ref[...], l2_ref[...]
    mx = jnp.maximum(jnp.maximum(l0, l1), l2)
    e0, e1, e2 = jnp.exp(l0 - mx), jnp.exp(l1 - mx), jnp.exp(l2 - mx)
    inv = 1.0 / (e0 + e1 + e2)
    w0, w1, w2 = e0 * inv, e1 * inv, e2 * inv
    parts = []
    for h in range(DSWA_HEADS):
        hs = slice(h * HEAD_DIM, (h + 1) * HEAD_DIM)
        parts.append((w0[:, h:h + 1] * o0_ref[:, hs].astype(F32)
                      + w1[:, h:h + 1] * o1_ref[:, hs].astype(F32)
                      + w2[:, h:h + 1] * o2_ref[:, hs].astype(F32)).astype(BF16))
    attn = jnp.concatenate(parts, axis=1)
    d = D_MODEL
    gl = _dot(x_ref[...].astype(BF16), wg_ref[...])
    gate = lambda j: jax.nn.sigmoid(gl[:, j * d:(j + 1) * d] + bg_ref[j:j + 1, :])
    merged = (gate(0) * _dot(attn, wa_ref[...])
              + gate(1) * _dot(rec_ref[...], wl_ref[...])
              + gate(2) * _dot(memo_ref[...], wm_ref[...]))
    mix = _dot(merged.astype(BF16), wo_ref[...])
    x1 = _layernorm(ALPHA * x_ref[...] + mix, g_ref[...], b_ref[...])
    x1_ref[...] = x1
    x1b_ref[...] = x1.astype(BF16)


def _mix(os, lses, rec, memo, x, w_gate, b_gate, wa, wl, wm, wo, g, bta, tt=512):
    t = x.shape[0]
    rows = lambda w: pl.BlockSpec((tt, w), lambda i: (i, 0))
    full = lambda a: pl.BlockSpec(a.shape, lambda i: (0,) * a.ndim, pipeline_mode=pl.Buffered(1))
    return pl.pallas_call(
        _mix_kernel,
        grid=(t // tt,),
        in_specs=[rows(GROUP_W)] * 3 + [rows(HEAD_DIM)] * 3 + [
            rows(D_RNN), rows(MEM_WIDTH), rows(D_MODEL), full(w_gate),
            full(b_gate), full(wa), full(wl), full(wm), full(wo), full(g), full(bta)],
        out_specs=[rows(D_MODEL), rows(D_MODEL)],
        out_shape=[jax.ShapeDtypeStruct((t, D_MODEL), F32), jax.ShapeDtypeStruct((t, D_MODEL), BF16)],
        compiler_params=_cparams(("parallel",)),
        name="mix_ln1",
    )(*os, *lses, rec, memo, x, w_gate, b_gate, wa, wl, wm, wo, g, bta)


def _peer_scores_kernel(x_ref, wq_ref, kbd_ref, o_ref):
    q = _dot(x_ref[...], wq_ref[...]).astype(BF16)
    for h in range(PEER_HEADS):
        hs = slice(h * PEER_KEY_DIM, (h + 1) * PEER_KEY_DIM)
        o_ref[hs, :] = _dot_nt(kbd_ref[h], q[:, hs])


def _peer_scores(x1b, wq, kbd, tt=1024):
    t = x1b.shape[0]
    w = PEER_HEADS * PEER_KEY_DIM
    return pl.pallas_call(
        _peer_scores_kernel,
        grid=(t // tt,),
        in_specs=[pl.BlockSpec((tt, D_MODEL), lambda i: (i, 0)),
                  pl.BlockSpec(wq.shape, lambda i: (0, 0)),
                  pl.BlockSpec(kbd.shape, lambda i: (0, 0, 0))],
        out_specs=pl.BlockSpec((w, tt), lambda i: (0, i)),
        out_shape=jax.ShapeDtypeStruct((w, t), F32),
        compiler_params=_cparams(("parallel",)),
        name="peer_scores",
    )(x1b, wq, kbd)


_PEER_CANDS = tuple((p, q) for p in range(PEER_TOPK) for q in range(PEER_TOPK)
                    if (p + 1) * (q + 1) <= PEER_TOPK)


def _top_positions(s, exact_ties):
    key = lax.broadcasted_iota(jnp.int32, s.shape, 0).astype(F32)
    pos = jnp.full(s.shape, float(PEER_TOPK), F32)
    work = s
    tops = []
    for p in range(PEER_TOPK):
        mx = jnp.max(work, axis=0, keepdims=True)
        if exact_ties:
            first = jnp.min(jnp.where(work == mx, key, float(N_KEYS)), axis=0, keepdims=True)
            hit = key == first
        else:
            hit = work == mx
        pos = jnp.where(hit, float(p), pos)
        work = jnp.where(hit, -jnp.inf, work)
        tops.append(mx)
    return pos, tops


def _store_lane_tiles(ref, h, val):
    for c in range(ref.shape[0]):
        ref[c, h * N_KEYS:(h + 1) * N_KEYS, :] = val[:, c * LANES:(c + 1) * LANES]


def _peer_select_body(sc_ref, ra_ref, ea_ref, cb_ref, eb_ref, exact_ties):
    tt = sc_ref.shape[1]
    kg = N_KEYS // BF16_ROWS
    pos_b, tops_a, tops_b = [], [], []
    marked = jnp.zeros((1, tt), F32)
    for h in range(PEER_HEADS):
        base = h * PEER_KEY_DIM
        pa, ta = _top_positions(sc_ref[base:base + N_KEYS, :], exact_ties)
        pb, tb = _top_positions(sc_ref[base + N_KEYS:base + 2 * N_KEYS, :], exact_ties)
        _store_lane_tiles(ra_ref, h, pa)
        pos_b.append(pb)
        tops_a.append(ta)
        tops_b.append(tb)
        if not exact_ties:
            for pos in (pa, pb):
                n_marked = jnp.sum(jnp.where(pos < float(PEER_TOPK), 1.0, 0.0), axis=0, keepdims=True)
                marked = jnp.maximum(marked, n_marked)
    a_top = [jnp.concatenate([tops_a[h][p] for h in range(PEER_HEADS)], axis=0)
             for p in range(PEER_TOPK)]
    b_top = [jnp.concatenate([tops_b[h][q] for h in range(PEER_HEADS)], axis=0)
             for q in range(PEER_TOPK)]
    sums = [a_top[p] + b_top[q] for p, q in _PEER_CANDS]
    n = len(sums)
    rank = [jnp.zeros((PEER_HEADS, tt), F32) for _ in range(n)]
    for c in range(n):
        for c2 in range(c):
            ge = sums[c2] >= sums[c]
            rank[c] = rank[c] + jnp.where(ge, 1.0, 0.0)
            rank[c2] = rank[c2] + jnp.where(ge, 0.0, 1.0)
    cnt = [jnp.zeros((PEER_HEADS, tt), F32) for _ in range(PEER_TOPK)]
    zsum = jnp.zeros((PEER_HEADS, tt), F32)
    for c, (p, q) in enumerate(_PEER_CANDS):
        sel = rank[c] < float(PEER_TOPK)
        cnt[q] = cnt[q] + jnp.where(sel, 1.0, 0.0)
        zsum = zsum + jnp.where(sel, jnp.exp(sums[c] - sums[0]), 0.0)
    inv_z = 1.0 / zsum
    for h in range(PEER_HEADS):
        base = h * PEER_KEY_DIM
        rows = slice(h * N_KEYS, (h + 1) * N_KEYS)
        posb = pos_b[h].astype(BF16).reshape(kg, BF16_ROWS, tt)
        cb = jnp.zeros((kg, BF16_ROWS, tt), BF16)
        for q in range(PEER_TOPK):
            cnt_q = jnp.broadcast_to(cnt[q][h:h + 1, :], (BF16_ROWS, tt)).astype(BF16)
            cb = cb + jnp.where(posb == float(q), cnt_q[None], jnp.zeros_like(cb))
        cb_ref[h * kg:(h + 1) * kg] = cb
        _store_lane_tiles(ea_ref, h, jnp.exp(sc_ref[base:base + N_KEYS, :] - a_top[0][h:h + 1, :])
                          * inv_z[h:h + 1, :])
        eb = jnp.exp(sc_ref[base + N_KEYS:base + 2 * N_KEYS, :] - b_top[0][h:h + 1, :])
        eb_ref[h * kg:(h + 1) * kg] = eb.astype(BF16).reshape(kg, BF16_ROWS, tt)
    return marked


def _peer_select_kernel(sc_ref, ra_ref, ea_ref, cb_ref, eb_ref):
    marked = _peer_select_body(sc_ref, ra_ref, ea_ref, cb_ref, eb_ref, exact_ties=False)

    @pl.when(jnp.max(marked) > float(PEER_TOPK))
    def _():
        _peer_select_body(sc_ref, ra_ref, ea_ref, cb_ref, eb_ref, exact_ties=True)


def _peer_select(sc, tt=256):
    w, t = sc.shape
    rows = PEER_HEADS * N_KEYS
    out_spec_a = pl.BlockSpec((tt // LANES, rows, LANES), lambda i: (i, 0, 0))
    shape_a = jax.ShapeDtypeStruct((t // LANES, rows, LANES), F32)
    out_spec_b = pl.BlockSpec((rows // BF16_ROWS, BF16_ROWS, tt), lambda i: (0, 0, i))
    shape_b = jax.ShapeDtypeStruct((rows // BF16_ROWS, BF16_ROWS, t), BF16)
    return pl.pallas_call(
        _peer_select_kernel,
        grid=(t // tt,),
        in_specs=[pl.BlockSpec((w, tt), lambda i: (0, i))],
        out_specs=[out_spec_a, out_spec_a, out_spec_b, out_spec_b],
        out_shape=[shape_a, shape_a, shape_b, shape_b],
        compiler_params=_cparams(("parallel",)),
        name="peer_select",
    )(sc)


def _row_on_sublanes(ref, h, i, lane0, lanes):
    tiles = range(lane0 // LANES, (lane0 + lanes) // LANES)
    return jnp.concatenate([ref[c, h, pl.ds(i, BF16_ROWS, stride=0), :] for c in tiles], axis=1)


def _masked_acts(hmat, ra_ref, ea_ref, cb_ref, eb_ref, key0):
    tt = hmat.shape[1]
    kg = N_KEYS // BF16_ROWS
    ys = []
    for k in range(hmat.shape[0] // N_KEYS):
        rows = slice(k * N_KEYS, (k + 1) * N_KEYS)
        ws = []
        for l0 in range(0, tt, PEER_MASK_LANES):
            ls = slice(l0, l0 + PEER_MASK_LANES)
            w = jnp.zeros((kg, BF16_ROWS, PEER_MASK_LANES), BF16)
            for h in range(PEER_HEADS):
                hs = slice(h * kg, (h + 1) * kg)
                ra = _row_on_sublanes(ra_ref, h, key0 + k, l0, PEER_MASK_LANES).astype(BF16)
                ea = _row_on_sublanes(ea_ref, h, key0 + k, l0, PEER_MASK_LANES).astype(BF16)
                w = w + eb_ref[hs, :, ls] * jnp.where(cb_ref[hs, :, ls] > ra[None], ea[None],
                                                      jnp.zeros_like(w))
            ws.append(w.reshape(N_KEYS, PEER_MASK_LANES))
        ys.append(jnp.concatenate(ws, axis=1) * _gelu_erf(hmat[rows, :]).astype(BF16))
    return jnp.concatenate(ys, axis=0)


def _peer_mix_kernel(x1b_ref, x1_ref, u_ref, vt_ref, ra_ref, ea_ref, cb_ref, eb_ref,
                     g_ref, b_ref, o_ref, acc_ref, *, ib):
    e = pl.program_id(1)

    @pl.when(e == 0)
    def _():
        acc_ref[...] = jnp.zeros_like(acc_ref)

    x1b = x1b_ref[...]
    acc = acc_ref[...]
    assert sum(PEER_CHUNK_KEYS) == ib
    starts = [sum(PEER_CHUNK_KEYS[:c]) for c in range(len(PEER_CHUNK_KEYS) + 1)]
    chunk_rows = lambda c: slice(starts[c] * N_KEYS, starts[c + 1] * N_KEYS)
    n_chunks = len(PEER_CHUNK_KEYS)
    h_next = _dot_nt(u_ref[chunk_rows(0), :], x1b)
    for c in range(n_chunks):
        hc = h_next
        if c + 1 < n_chunks:
            h_next = _dot_nt(u_ref[chunk_rows(c + 1), :], x1b)
        y = _masked_acts(hc, ra_ref, ea_ref, cb_ref, eb_ref, starts[c])
        acc = acc + _dot(vt_ref[:, chunk_rows(c)], y)
    acc_ref[...] = acc

    @pl.when(e == pl.num_programs(1) - 1)
    def _():
        o_ref[...] = _layernorm(ALPHA * x1_ref[...] + acc_ref[...].T, g_ref[...], b_ref[...])


def _peer_mix(x1b, x1, u, vt, ra, ea, cb, eb, g, bta, tt=512, te=sum(PEER_CHUNK_KEYS) * N_KEYS):
    t = x1.shape[0]
    ib = te // N_KEYS
    ne = N_EXPERTS // te
    rows = PEER_HEADS * N_KEYS
    return pl.pallas_call(
        functools.partial(_peer_mix_kernel, ib=ib),
        grid=(t // tt, ne),
        in_specs=[pl.BlockSpec((tt, D_MODEL), lambda i, e: (i, 0)),
                  pl.BlockSpec((tt, D_MODEL), lambda i, e: (i, 0)),
                  pl.BlockSpec((te, D_MODEL), lambda i, e: (e, 0)),
                  pl.BlockSpec((D_MODEL, te), lambda i, e: (0, e)),
                  pl.BlockSpec((tt // LANES, PEER_HEADS, ib, LANES), lambda i, e: (i, 0, e, 0)),
                  pl.BlockSpec((tt // LANES, PEER_HEADS, ib, LANES), lambda i, e: (i, 0, e, 0)),
                  pl.BlockSpec((rows // BF16_ROWS, BF16_ROWS, tt), lambda i, e: (0, 0, i)),
                  pl.BlockSpec((rows // BF16_ROWS, BF16_ROWS, tt), lambda i, e: (0, 0, i)),
                  pl.BlockSpec((1, D_MODEL), lambda i, e: (0, 0)),
                  pl.BlockSpec((1, D_MODEL), lambda i, e: (0, 0))],
        out_specs=pl.BlockSpec((tt, D_MODEL), lambda i, e: (i, 0)),
        out_shape=jax.ShapeDtypeStruct((t, D_MODEL), F32),
        scratch_shapes=[pltpu.VMEM((D_MODEL, tt), F32)],
        compiler_params=_cparams(("parallel", "arbitrary")),
        name="peer_mix_ln2",
    )(x1b, x1, u, vt, ra, ea, cb, eb, g, bta)


def _block_diag(w, per_tile):
    n, k, _ = w.shape
    w = w.reshape(n // per_tile, per_tile, k, k)
    eye = jnp.eye(per_tile, dtype=w.dtype)
    out = jnp.einsum('tpij,pq->tpiqj', w, eye)
    return out.reshape(n // per_tile, per_tile * k, per_tile * k)


def _layer(x, mem, w_in, b_gate, conv_w, conv_b, lru_wa, lru_ba, lru_wx, lru_bx, lru_lambda,
           w_mem_kv, w_br_attn, w_br_lru, w_br_mem, w_out, ln1_g, ln1_b,
           peer_wq, peer_keys, peer_u, peer_v, ln2_g, ln2_b):
    b, s, d = x.shape
    t = b * s
    xf = x.reshape(t, d)
    xb = xf.astype(BF16)
    w_in_b = w_in.astype(BF16)
    row = lambda a: a.reshape(1, -1).astype(F32)

    off_rnn = 3 * DSWA_WIDTH
    off_mq = off_rnn + 2 * D_RNN
    off_gl = off_mq + MEM_WIDTH
    mq = _matmul(xb, w_in_b, off_mq, MEM_WIDTH, BF16, tn=512).reshape(b, s, MEM_WIDTH)

    os, lses = [], []
    for gi, dil in enumerate(DSWA_DILATIONS):
        xp = xb if dil == 1 else _permute_rows(x, dil).reshape(t, d)
        w_g = jnp.concatenate([w_in_b[:, part * DSWA_WIDTH + gi * GROUP_W:
                                      part * DSWA_WIDTH + (gi + 1) * GROUP_W] for part in range(3)], axis=1)
        qkv = _matmul(xp, w_g, 0, 3 * GROUP_W, BF16, tn=3 * GROUP_W).reshape(b, dil, s // dil, 3 * GROUP_W)
        o_g, lse_g = _dswa_group(qkv, gi)
        os.append(o_g.reshape(t, GROUP_W))
        lses.append(lse_g.reshape(t, HEAD_DIM))

    per_tile = MXU_TILE // LRU_BW
    wa_bd = _block_diag(lru_wa, per_tile).astype(BF16)
    wx_bd = _block_diag(lru_wx, per_tile).astype(BF16)
    rec = _rglru(xb.reshape(b, s, d), w_in_b[:, off_rnn:off_mq], conv_w.astype(F32), row(conv_b),
                 wa_bd, row(lru_ba), wx_bd, row(lru_bx), row(lru_lambda)).reshape(t, D_RNN)

    mem_len = mem.shape[1]
    kv = _matmul(mem.reshape(b * mem_len, d).astype(BF16), w_mem_kv.astype(BF16), 0,
                 2 * MEM_WIDTH, BF16, tn=1024).reshape(b, mem_len, 2 * MEM_WIDTH)
    memo = _memattn(mq, kv).reshape(t, MEM_WIDTH)

    x1, x1b = _mix(os, lses, rec, memo, xf, w_in_b[:, off_gl:], b_gate.astype(F32), w_br_attn.astype(BF16),
                   w_br_lru.astype(BF16), w_br_mem.astype(BF16), w_out.astype(BF16),
                   row(ln1_g), row(ln1_b))

    kbd = _block_diag(peer_keys.reshape(PEER_HEADS * 2, N_KEYS, PEER_KEY_DIM // 2), 2).astype(BF16)
    scores = _peer_scores(x1b, peer_wq.astype(BF16), kbd)

    ra, ea, cb, eb = _peer_select(scores)
    ra = ra.reshape(t // LANES, PEER_HEADS, N_KEYS, LANES)
    ea = ea.reshape(t // LANES, PEER_HEADS, N_KEYS, LANES)

    out = _peer_mix(x1b, x1, peer_u.astype(BF16), peer_v.T.astype(BF16), ra, ea, cb, eb,
                    row(ln2_g), row(ln2_b))
    return out.reshape(b, s, d)


def kernel(x, mem, w_in, b_gate, conv_w, conv_b, lru_wa, lru_ba, lru_wx, lru_bx, lru_lambda, w_mem_kv, w_br_attn, w_br_lru, w_br_mem, w_out, ln1_g, ln1_b, peer_wq, peer_keys, peer_u, peer_v, ln2_g, ln2_b):
    h = x.astype(F32)
    depth = w_in.shape[0]
    for l in range(depth):
        h = _layer(h, mem, w_in[l], b_gate[l], conv_w[l], conv_b[l], lru_wa[l], lru_ba[l],
                   lru_wx[l], lru_bx[l], lru_lambda[l], w_mem_kv[l], w_br_attn[l], w_br_lru[l],
                   w_br_mem[l], w_out[l], ln1_g[l], ln1_b[l], peer_wq[l], peer_keys[l],
                   peer_u[l], peer_v[l], ln2_g[l], ln2_b[l])
    return h.astype(x.dtype)
```

```python
import functools
import math

import jax
import jax.numpy as jnp
from jax import lax
from jax.experimental import pallas as pl
from jax.experimental.pallas import tpu as pltpu

F32 = jnp.float32
BF16 = jnp.bfloat16

D_MODEL = 1024
N_GROUPS = 3
DSWA_DILATIONS = (1, 4, 16)
DSWA_HEADS = 4
HEAD_DIM = 128
DSWA_BLK = 128
DSWA_BATCH_BLOCKS = 4
DSWA_TILE = 2048
GROUP_W = DSWA_HEADS * HEAD_DIM
DSWA_WIDTH = N_GROUPS * GROUP_W
D_RNN = 1024
LRU_BLOCKS = 16
LRU_BW = D_RNN // LRU_BLOCKS
CONV_W = 4
LRU_C = 8.0
MEM_HEADS = 4
MEM_WIDTH = MEM_HEADS * HEAD_DIM
PEER_HEADS = 8
PEER_KEY_DIM = 256
N_KEYS = 128
N_EXPERTS = N_KEYS * N_KEYS
PEER_TOPK = 16
PEER_CHUNK_KEYS = (8,) * 2
BF16_ROWS = 16
PEER_MASK_LANES = 256
ALPHA = 2.0 ** 0.25
LN_EPS = 1e-5
NEG_INF = -1e30

LANES = 128
SUBLANES = 8
MXU_TILE = 256
VMEM_LIMIT = 56 * 1024 * 1024


def _cparams(sem):
    return pltpu.CompilerParams(dimension_semantics=sem, vmem_limit_bytes=VMEM_LIMIT)


def _dot(a, b):
    return jnp.dot(a, b, preferred_element_type=F32)


def _dot_nt(a, b):
    return lax.dot_general(a, b, (((1,), (1,)), ((), ())), preferred_element_type=F32)


def _gelu_erf(x):
    return 0.5 * x * (1.0 + lax.erf(x * (1.0 / math.sqrt(2.0))))


def _layernorm(h, g, b):
    mu = jnp.mean(h, axis=-1, keepdims=True)
    c = h - mu
    var = jnp.mean(c * c, axis=-1, keepdims=True)
    return c * lax.rsqrt(var + LN_EPS) * g + b


def _matmul_kernel(x_ref, w_ref, o_ref):
    o_ref[...] = _dot(x_ref[...], w_ref[...]).astype(o_ref.dtype)


def _matmul(x, w, col_off, n_cols, out_dtype, tm=1024, tn=1024):
    m, k = x.shape
    tn = min(tn, n_cols)
    tm = min(tm, m)
    assert m % tm == 0 and n_cols % tn == 0 and col_off % tn == 0
    off = col_off // tn
    return pl.pallas_call(
        _matmul_kernel,
        grid=(m // tm, n_cols // tn),
        in_specs=[pl.BlockSpec((tm, k), lambda i, j: (i, 0)),
                  pl.BlockSpec((k, tn), lambda i, j: (0, off + j))],
        out_specs=pl.BlockSpec((tm, tn), lambda i, j: (i, j)),
        out_shape=jax.ShapeDtypeStruct((m, n_cols), out_dtype),
        compiler_params=_cparams(("parallel", "arbitrary")),
        name="matmul",
    )(x, w)


def _permute_kernel(x_ref, o_ref, *, dilation):
    rows = x_ref.shape[1] // dilation
    for bi in range(x_ref.shape[0]):
        for r in range(dilation):
            o_ref[bi, r] = x_ref[bi, pl.ds(r, rows, stride=dilation), :].astype(o_ref.dtype)


def _permute_rows(x, dilation):
    b, s, d = x.shape
    m_len = s // dilation
    return pl.pallas_call(
        functools.partial(_permute_kernel, dilation=dilation),
        grid=(s // DSWA_TILE, d // LANES),
        in_specs=[pl.BlockSpec((b, DSWA_TILE, LANES), lambda n, c: (0, n, c))],
        out_specs=pl.BlockSpec((b, dilation, DSWA_TILE // dilation, LANES),
                               lambda n, c: (0, 0, n, c)),
        out_shape=jax.ShapeDtypeStruct((b, dilation, m_len, d), BF16),
        compiler_params=_cparams(("parallel", "parallel")),
        name=f"permute_d{dilation}",
    )(x)


def _dswa_kernel(q_ref, kc_ref, kp_ref, vc_ref, vp_ref, o_ref, lse_ref, o_scr, lse_scr, *, dilation):
    nq = q_ref.shape[2] // DSWA_BLK
    row = lax.broadcasted_iota(jnp.int32, (DSWA_BLK, DSWA_BLK), 0)
    col = lax.broadcasted_iota(jnp.int32, (DSWA_BLK, DSWA_BLK), 1)
    mask_cur = col <= row
    mask_band = col >= row
    mask_first = col >= row + jnp.where(pl.program_id(1) == 0, DSWA_BLK, 0)
    scale = 1.0 / math.sqrt(HEAD_DIM)
    bdot_nt = lambda a, b: jnp.einsum('hqd,hkd->hqk', a, b, preferred_element_type=F32)
    bdot = lambda a, b: jnp.einsum('hqk,hkd->hqd', a, b, preferred_element_type=F32)

    def heads(blocks):
        return jnp.stack([a[:, h * HEAD_DIM:(h + 1) * HEAD_DIM]
                          for a in blocks for h in range(DSWA_HEADS)], axis=0)

    def attend(blocks):
        qs, kps, kcs, vps, vcs = [], [], [], [], []
        for r, j in blocks:
            rows = slice(j * DSWA_BLK, (j + 1) * DSWA_BLK)
            prows = slice((j - 1) * DSWA_BLK, j * DSWA_BLK)
            qs.append(q_ref[0, r, rows, :])
            kcs.append(kc_ref[0, r, rows, :])
            vcs.append(vc_ref[0, r, rows, :])
            kps.append(kp_ref[0, r] if j == 0 else kc_ref[0, r, prows, :])
            vps.append(vp_ref[0, r] if j == 0 else vc_ref[0, r, prows, :])
        mask_prev = mask_first if blocks[0][1] == 0 else mask_band
        q4 = heads(qs)
        s_p = jnp.where(mask_prev[None], bdot_nt(q4, heads(kps)) * scale, NEG_INF)
        s_c = jnp.where(mask_cur[None], bdot_nt(q4, heads(kcs)) * scale, NEG_INF)
        m = jnp.maximum(jnp.max(s_p, axis=-1, keepdims=True), jnp.max(s_c, axis=-1, keepdims=True))
        p_p = jnp.exp(s_p - m)
        p_c = jnp.exp(s_c - m)
        l = jnp.sum(p_p, axis=-1, keepdims=True) + jnp.sum(p_c, axis=-1, keepdims=True)
        o = (bdot(p_p.astype(BF16), heads(vps)) + bdot(p_c.astype(BF16), heads(vcs))) / l
        lse = m + jnp.log(l)
        for bi, (r, j) in enumerate(blocks):
            dst = pl.ds(j * DSWA_BLK * dilation + r, DSWA_BLK, stride=dilation)
            lse_tile = jnp.zeros((DSWA_BLK, HEAD_DIM), F32)
            for h in range(DSWA_HEADS):
                o_scr[h, dst, :] = o[bi * DSWA_HEADS + h]
                lse_tile = jnp.where(col == h, lse[bi * DSWA_HEADS + h], lse_tile)
            lse_scr[dst, :] = lse_tile

    for first in (True, False):
        group = [(r, j) for j in range(nq) for r in range(dilation) if (j == 0) == first]
        for i in range(0, len(group), DSWA_BATCH_BLOCKS):
            attend(group[i:i + DSWA_BATCH_BLOCKS])
    for h in range(DSWA_HEADS):
        o_ref[0, :, h * HEAD_DIM:(h + 1) * HEAD_DIM] = o_scr[h].astype(o_ref.dtype)
    lse_ref[0] = lse_scr[...]


def _dswa_group(qkv, group):
    b, dilation, m_len, _ = qkv.shape
    s = dilation * m_len
    assert s % DSWA_TILE == 0 and DSWA_TILE % (dilation * DSWA_BLK) == 0
    rows = DSWA_TILE // dilation
    nq = rows // DSWA_BLK

    def cur(c):
        return pl.BlockSpec((1, dilation, rows, GROUP_W), lambda bi, n: (bi, 0, n, c))

    def prev(c):
        return pl.BlockSpec((1, dilation, DSWA_BLK, GROUP_W),
                            lambda bi, n: (bi, 0, jnp.maximum(n * nq - 1, 0), c))

    return pl.pallas_call(
        functools.partial(_dswa_kernel, dilation=dilation),
        grid=(b, s // DSWA_TILE),
        in_specs=[cur(0), cur(1), prev(1), cur(2), prev(2)],
        out_specs=[pl.BlockSpec((1, DSWA_TILE, GROUP_W), lambda bi, n: (bi, n, 0)),
                   pl.BlockSpec((1, DSWA_TILE, HEAD_DIM), lambda bi, n: (bi, n, 0))],
        out_shape=[jax.ShapeDtypeStruct((b, s, GROUP_W), BF16),
                   jax.ShapeDtypeStruct((b, s, HEAD_DIM), F32)],
        scratch_shapes=[pltpu.VMEM((DSWA_HEADS, DSWA_TILE, HEAD_DIM), F32),
                        pltpu.VMEM((DSWA_TILE, HEAD_DIM), F32)],
        compiler_params=_cparams(("parallel", "arbitrary")),
        name=f"dswa_g{group}",
    )(qkv, qkv, qkv, qkv, qkv)


def _linear_scan(a, b, h0):
    ts, d = a.shape
    a = a.reshape(ts // SUBLANES, SUBLANES, d)
    b = b.reshape(ts // SUBLANES, SUBLANES, d)
    row = lax.broadcasted_iota(jnp.int32, a.shape, 1)
    step = 1
    while step < SUBLANES:
        keep = row >= step
        a_s = jnp.where(keep, pltpu.roll(a, step, 1), 1.0)
        b_s = jnp.where(keep, pltpu.roll(b, step, 1), 0.0)
        b = a * b_s + b
        a = a * a_s
        step *= 2
    hs, carry = [], h0
    for g in range(ts // SUBLANES):
        hg = a[g] * carry + b[g]
        hs.append(hg)
        carry = hg[SUBLANES - 1:SUBLANES, :]
    return jnp.concatenate(hs, axis=0), carry


def _rglru_kernel(xb_ref, halo_ref, w_ref, cw_ref, cb_ref, wa_ref, ba_ref, wx_ref, bx_ref,
                  lam_ref, o_ref, h_ref, *, ts):
    si = pl.program_id(1)

    @pl.when(si == 0)
    def _():
        h_ref[...] = jnp.zeros_like(h_ref)

    z = _dot(xb_ref[0], w_ref[...])
    x = z[:, :D_RNN]
    yg = z[:, D_RNN:]
    halo = _dot(halo_ref[0], w_ref[:, :D_RNN])[BF16_ROWS - SUBLANES:, :]
    halo = jnp.where(si > 0, halo, 0.0)
    xfull = jnp.concatenate([halo, x], axis=0)
    xc = cb_ref[...] + cw_ref[CONV_W - 1:CONV_W, :] * x
    for k in range(1, CONV_W):
        xc = xc + cw_ref[CONV_W - 1 - k:CONV_W - k, :] * pltpu.roll(xfull, k, 0)[SUBLANES:]

    xb = xc.astype(BF16)
    nblk = D_RNN // MXU_TILE
    ra = jnp.concatenate(
        [_dot(xb[:, c * MXU_TILE:(c + 1) * MXU_TILE], wa_ref[c]) for c in range(nblk)], axis=1)
    rx = jnp.concatenate(
        [_dot(xb[:, c * MXU_TILE:(c + 1) * MXU_TILE], wx_ref[c]) for c in range(nblk)], axis=1)
    r = jax.nn.sigmoid(ra + ba_ref[...])
    gate_i = jax.nn.sigmoid(rx + bx_ref[...])
    neg_lam = -lam_ref[...]
    softplus = jnp.maximum(neg_lam, 0.0) + jnp.log1p(jnp.exp(-jnp.abs(neg_lam)))
    log_a = (-LRU_C) * r * softplus
    a = jnp.exp(log_a)
    bterm = jnp.sqrt(1.0 - a * a) * gate_i * xc

    h, carry = _linear_scan(a, bterm, h_ref[0:1, :])
    h_ref[0:1, :] = carry
    o_ref[0] = (h * jax.nn.gelu(yg, approximate=True)).astype(o_ref.dtype)


def _rglru(xb, w_rnn, conv_w, conv_b, wa_bd, ba, wx_bd, bx, lam, ts=512):
    b, s, d = xb.shape
    assert s % ts == 0
    full = lambda shape: pl.BlockSpec(shape, lambda bi, si: (0,) * len(shape))
    return pl.pallas_call(
        functools.partial(_rglru_kernel, ts=ts),
        grid=(b, s // ts),
        in_specs=[pl.BlockSpec((1, ts, d), lambda bi, si: (bi, si, 0)),
                  pl.BlockSpec((1, BF16_ROWS, d),
                               lambda bi, si: (bi, jnp.maximum(si * (ts // BF16_ROWS) - 1, 0), 0)),
                  full(w_rnn.shape),
                  full((CONV_W, D_RNN)), full((1, D_RNN)),
                  full(wa_bd.shape), full((1, D_RNN)),
                  full(wx_bd.shape), full((1, D_RNN)), full((1, D_RNN))],
        out_specs=pl.BlockSpec((1, ts, D_RNN), lambda bi, si: (bi, si, 0)),
        out_shape=jax.ShapeDtypeStruct((b, s, D_RNN), BF16),
        scratch_shapes=[pltpu.VMEM((SUBLANES, D_RNN), F32)],
        compiler_params=_cparams(("parallel", "arbitrary")),
        name="rglru",
    )(xb, xb, w_rnn, conv_w, conv_b, wa_bd, ba, wx_bd, bx, lam)


def _memattn_kernel(q_ref, kv_ref, o_ref):
    scale = 1.0 / math.sqrt(HEAD_DIM)
    for h in range(MEM_HEADS):
        hs = slice(h * HEAD_DIM, (h + 1) * HEAD_DIM)
        vs = slice(MEM_WIDTH + h * HEAD_DIM, MEM_WIDTH + (h + 1) * HEAD_DIM)
        s = _dot_nt(q_ref[0, :, hs], kv_ref[0, :, hs]) * scale
        m = jnp.max(s, axis=-1, keepdims=True)
        p = jnp.exp(s - m)
        l = jnp.sum(p, axis=-1, keepdims=True)
        o_ref[0, :, hs] = (_dot(p.astype(BF16), kv_ref[0, :, vs]) / l).astype(o_ref.dtype)


def _memattn(mq, kv, ts=1024):
    b, s, _ = mq.shape
    mem_len = kv.shape[1]
    return pl.pallas_call(
        _memattn_kernel,
        grid=(b, s // ts),
        in_specs=[pl.BlockSpec((1, ts, MEM_WIDTH), lambda bi, si: (bi, si, 0)),
                  pl.BlockSpec((1, mem_len, 2 * MEM_WIDTH), lambda bi, si: (bi, 0, 0))],
        out_specs=pl.BlockSpec((1, ts, MEM_WIDTH), lambda bi, si: (bi, si, 0)),
        out_shape=jax.ShapeDtypeStruct((b, s, MEM_WIDTH), BF16),
        compiler_params=_cparams(("parallel", "arbitrary")),
        name="memattn",
    )(mq, kv)


def _mix_kernel(o0_ref, o1_ref, o2_ref, l0_ref, l1_ref, l2_ref, rec_ref, memo_ref, x_ref,
                wg_ref, bg_ref, wa_ref, wl_ref, wm_ref, wo_ref, g_ref, b_ref, x1_ref, x1b_ref):
    l0, l1, l2 = l0_ref[...], l1_ref[...], l2_ref[...]
    mx = jnp.maximum(jnp.maximum(l0, l1), l2)
    e0, e1, e2 = jnp.exp(l0 - mx), jnp.exp(l1 - mx), jnp.exp(l2 - mx)
    inv = 1.0 / (e0 + e1 + e2)
    w0, w1, w2 = e0 * inv, e1 * inv, e2 * inv
    parts = []
    for h in range(DSWA_HEADS):
        hs = slice(h * HEAD_DIM, (h + 1) * HEAD_DIM)
        parts.append((w0[:, h:h + 1] * o0_ref[:, hs].astype(F32)
                      + w1[:, h:h + 1] * o1_ref[:, hs].astype(F32)
                      + w2[:, h:h + 1] * o2_ref[:, hs].astype(F32)).astype(BF16))
    attn = jnp.concatenate(parts, axis=1)
    d = D_MODEL
    gl = _dot(x_ref[...].astype(BF16), wg_ref[...])
    gate = lambda j: jax.nn.sigmoid(gl[:, j * d:(j + 1) * d] + bg_ref[j:j + 1, :])
    merged = (gate(0) * _dot(attn, wa_ref[...])
              + gate(1) * _dot(rec_ref[...], wl_ref[...])
              + gate(2) * _dot(memo_ref[...], wm_ref[...]))
    mix = _dot(merged.astype(BF16), wo_ref[...])
    x1 = _layernorm(ALPHA * x_ref[...] + mix, g_ref[...], b_ref[...])
    x1_ref[...] = x1
    x1b_ref[...] = x1.astype(BF16)


def _mix(os, lses, rec, memo, x, w_gate, b_gate, wa, wl, wm, wo, g, bta, tt=512):
    t = x.shape[0]
    rows = lambda w: pl.BlockSpec((tt, w), lambda i: (i, 0))
    full = lambda a: pl.BlockSpec(a.shape, lambda i: (0,) * a.ndim, pipeline_mode=pl.Buffered(1))
    return pl.pallas_call(
        _mix_kernel,
        grid=(t // tt,),
        in_specs=[rows(GROUP_W)] * 3 + [rows(HEAD_DIM)] * 3 + [
            rows(D_RNN), rows(MEM_WIDTH), rows(D_MODEL), full(w_gate),
            full(b_gate), full(wa), full(wl), full(wm), full(wo), full(g), full(bta)],
        out_specs=[rows(D_MODEL), rows(D_MODEL)],
        out_shape=[jax.ShapeDtypeStruct((t, D_MODEL), F32), jax.ShapeDtypeStruct((t, D_MODEL), BF16)],
        compiler_params=_cparams(("parallel",)),
        name="mix_ln1",
    )(*os, *lses, rec, memo, x, w_gate, b_gate, wa, wl, wm, wo, g, bta)


def _peer_scores_kernel(x_ref, wq_ref, kbd_ref, o_ref):
    q = _dot(x_ref[...], wq_ref[...]).astype(BF16)
    for h in range(PEER_HEADS):
        hs = slice(h * PEER_KEY_DIM, (h + 1) * PEER_KEY_DIM)
        o_ref[hs, :] = _dot_nt(kbd_ref[h], q[:, hs])


def _peer_scores(x1b, wq, kbd, tt=1024):
    t = x1b.shape[0]
    w = PEER_HEADS * PEER_KEY_DIM
    return pl.pallas_call(
        _peer_scores_kernel,
        grid=(t // tt,),
        in_specs=[pl.BlockSpec((tt, D_MODEL), lambda i: (i, 0)),
                  pl.BlockSpec(wq.shape, lambda i: (0, 0)),
                  pl.BlockSpec(kbd.shape, lambda i: (0, 0, 0))],
        out_specs=pl.BlockSpec((w, tt), lambda i: (0, i)),
        out_shape=jax.ShapeDtypeStruct((w, t), F32),
        compiler_params=_cparams(("parallel",)),
        name="peer_scores",
    )(x1b, wq, kbd)


_PEER_CANDS = tuple((p, q) for p in range(PEER_TOPK) for q in range(PEER_TOPK)
                    if (p + 1) * (q + 1) <= PEER_TOPK)


def _top_positions(s, exact_ties):
    key = lax.broadcasted_iota(jnp.int32, s.shape, 0).astype(F32)
    pos = jnp.full(s.shape, float(PEER_TOPK), F32)
    work = s
    tops = []
    for p in range(PEER_TOPK):
        mx = jnp.max(work, axis=0, keepdims=True)
        if exact_ties:
            first = jnp.min(jnp.where(work == mx, key, float(N_KEYS)), axis=0, keepdims=True)
            hit = key == first
        else:
            hit = work == mx
        pos = jnp.where(hit, float(p), pos)
        work = jnp.where(hit, -jnp.inf, work)
        tops.append(mx)
    return pos, tops


def _store_lane_tiles(ref, h, val):
    for c in range(ref.shape[0]):
        ref[c, h * N_KEYS:(h + 1) * N_KEYS, :] = val[:, c * LANES:(c + 1) * LANES]


def _peer_select_body(sc_ref, ra_ref, ea_ref, cb_ref, eb_ref, exact_ties):
    tt = sc_ref.shape[1]
    kg = N_KEYS // BF16_ROWS
    pos_b, tops_a, tops_b = [], [], []
    marked = jnp.zeros((1, tt), F32)
    for h in range(PEER_HEADS):
        base = h * PEER_KEY_DIM
        pa, ta = _top_positions(sc_ref[base:base + N_KEYS, :], exact_ties)
        pb, tb = _top_positions(sc_ref[base + N_KEYS:base + 2 * N_KEYS, :], exact_ties)
        _store_lane_tiles(ra_ref, h, pa)
        pos_b.append(pb)
        tops_a.append(ta)
        tops_b.append(tb)
        if not exact_ties:
            for pos in (pa, pb):
                n_marked = jnp.sum(jnp.where(pos < float(PEER_TOPK), 1.0, 0.0), axis=0, keepdims=True)
                marked = jnp.maximum(marked, n_marked)
    a_top = [jnp.concatenate([tops_a[h][p] for h in range(PEER_HEADS)], axis=0)
             for p in range(PEER_TOPK)]
    b_top = [jnp.concatenate([tops_b[h][q] for h in range(PEER_HEADS)], axis=0)
             for q in range(PEER_TOPK)]
    sums = [a_top[p] + b_top[q] for p, q in _PEER_CANDS]
    n = len(sums)
    rank = [jnp.zeros((PEER_HEADS, tt), F32) for _ in range(n)]
    for c in range(n):
        for c2 in range(c):
            ge = sums[c2] >= sums[c]
            rank[c] = rank[c] + jnp.where(ge, 1.0, 0.0)
            rank[c2] = rank[c2] + jnp.where(ge, 0.0, 1.0)
    cnt = [jnp.zeros((PEER_HEADS, tt), F32) for _ in range(PEER_TOPK)]
    zsum = jnp.zeros((PEER_HEADS, tt), F32)
    for c, (p, q) in enumerate(_PEER_CANDS):
        sel = rank[c] < float(PEER_TOPK)
        cnt[q] = cnt[q] + jnp.where(sel, 1.0, 0.0)
        zsum = zsum + jnp.where(sel, jnp.exp(sums[c] - sums[0]), 0.0)
    inv_z = 1.0 / zsum
    for h in range(PEER_HEADS):
        base = h * PEER_KEY_DIM
        rows = slice(h * N_KEYS, (h + 1) * N_KEYS)
        posb = pos_b[h].astype(BF16).reshape(kg, BF16_ROWS, tt)
        cb = jnp.zeros((kg, BF16_ROWS, tt), BF16)
        for q in range(PEER_TOPK):
            cnt_q = jnp.broadcast_to(cnt[q][h:h + 1, :], (BF16_ROWS, tt)).astype(BF16)
            cb = cb + jnp.where(posb == float(q), cnt_q[None], jnp.zeros_like(cb))
        cb_ref[h * kg:(h + 1) * kg] = cb
        _store_lane_tiles(ea_ref, h, jnp.exp(sc_ref[base:base + N_KEYS, :] - a_top[0][h:h + 1, :])
                          * inv_z[h:h + 1, :])
        eb = jnp.exp(sc_ref[base + N_KEYS:base + 2 * N_KEYS, :] - b_top[0][h:h + 1, :])
        eb_ref[h * kg:(h + 1) * kg] = eb.astype(BF16).reshape(kg, BF16_ROWS, tt)
    return marked


def _peer_select_kernel(sc_ref, ra_ref, ea_ref, cb_ref, eb_ref):
    marked = _peer_select_body(sc_ref, ra_ref, ea_ref, cb_ref, eb_ref, exact_ties=False)

    @pl.when(jnp.max(marked) > float(PEER_TOPK))
    def _():
        _peer_select_body(sc_ref, ra_ref, ea_ref, cb_ref, eb_ref, exact_ties=True)


def _peer_select(sc, tt=256):
    w, t = sc.shape
    rows = PEER_HEADS * N_KEYS
    out_spec_a = pl.BlockSpec((tt // LANES, rows, LANES), lambda i: (i, 0, 0))
    shape_a = jax.ShapeDtypeStruct((t // LANES, rows, LANES), F32)
    out_spec_b = pl.BlockSpec((rows // BF16_ROWS, BF16_ROWS, tt), lambda i: (0, 0, i))
    shape_b = jax.ShapeDtypeStruct((rows // BF16_ROWS, BF16_ROWS, t), BF16)
    return pl.pallas_call(
        _peer_select_kernel,
        grid=(t // tt,),
        in_specs=[pl.BlockSpec((w, tt), lambda i: (0, i))],
        out_specs=[out_spec_a, out_spec_a, out_spec_b, out_spec_b],
        out_shape=[shape_a, shape_a, shape_b, shape_b],
        compiler_params=_cparams(("parallel",)),
        name="peer_select",
    )(sc)


def _row_on_sublanes(ref, h, i, lane0, lanes):
    tiles = range(lane0 // LANES, (lane0 + lanes) // LANES)
    return jnp.concatenate([ref[c, h, pl.ds(i, BF16_ROWS, stride=0), :] for c in tiles], axis=1)


def _masked_acts(hmat, ra_ref, ea_ref, cb_ref, eb_ref, key0):
    tt = hmat.shape[1]
    kg = N_KEYS // BF16_ROWS
    ys = []
    for k in range(hmat.shape[0] // N_KEYS):
        rows = slice(k * N_KEYS, (k + 1) * N_KEYS)
        ws = []
        for l0 in range(0, tt, PEER_MASK_LANES):
            ls = slice(l0, l0 + PEER_MASK_LANES)
            w = jnp.zeros((kg, BF16_ROWS, PEER_MASK_LANES), BF16)
            for h in range(PEER_HEADS):
                hs = slice(h * kg, (h + 1) * kg)
                ra = _row_on_sublanes(ra_ref, h, key0 + k, l0, PEER_MASK_LANES).astype(BF16)
                ea = _row_on_sublanes(ea_ref, h, key0 + k, l0, PEER_MASK_LANES).astype(BF16)
                w = w + eb_ref[hs, :, ls] * jnp.where(cb_ref[hs, :, ls] > ra[None], ea[None],
                                                      jnp.zeros_like(w))
            ws.append(w.reshape(N_KEYS, PEER_MASK_LANES))
        ys.append(jnp.concatenate(ws, axis=1) * _gelu_erf(hmat[rows, :]).astype(BF16))
    return jnp.concatenate(ys, axis=0)


def _peer_mix_kernel(x1b_ref, x1_ref, u_ref, vt_ref, ra_ref, ea_ref, cb_ref, eb_ref,
                     g_ref, b_ref, o_ref, acc_ref, *, ib):
    e = pl.program_id(1)

    @pl.when(e == 0)
    def _():
        acc_ref[...] = jnp.zeros_like(acc_ref)

    x1b = x1b_ref[...]
    acc = acc_ref[...]
    assert sum(PEER_CHUNK_KEYS) == ib
    starts = [sum(PEER_CHUNK_KEYS[:c]) for c in range(len(PEER_CHUNK_KEYS) + 1)]
    chunk_rows = lambda c: slice(starts[c] * N_KEYS, starts[c + 1] * N_KEYS)
    n_chunks = len(PEER_CHUNK_KEYS)
    h_next = _dot_nt(u_ref[chunk_rows(0), :], x1b)
    for c in range(n_chunks):
        hc = h_next
        if c + 1 < n_chunks:
            h_next = _dot_nt(u_ref[chunk_rows(c + 1), :], x1b)
        y = _masked_acts(hc, ra_ref, ea_ref, cb_ref, eb_ref, starts[c])
        acc = acc + lax.dot_general(vt_ref[chunk_rows(c), :], y, (((0,), (0,)), ((), ())),
                                    preferred_element_type=F32)
    acc_ref[...] = acc

    @pl.when(e == pl.num_programs(1) - 1)
    def _():
        o_ref[...] = _layernorm(ALPHA * x1_ref[...] + acc_ref[...].T, g_ref[...], b_ref[...])


def _peer_mix(x1b, x1, u, vt, ra, ea, cb, eb, g, bta, tt=512, te=sum(PEER_CHUNK_KEYS) * N_KEYS):
    t = x1.shape[0]
    ib = te // N_KEYS
    ne = N_EXPERTS // te
    rows = PEER_HEADS * N_KEYS
    return pl.pallas_call(
        functools.partial(_peer_mix_kernel, ib=ib),
        grid=(t // tt, ne),
        in_specs=[pl.BlockSpec((tt, D_MODEL), lambda i, e: (i, 0)),
                  pl.BlockSpec((tt, D_MODEL), lambda i, e: (i, 0)),
                  pl.BlockSpec((te, D_MODEL), lambda i, e: (e, 0)),
                  pl.BlockSpec((te, D_MODEL), lambda i, e: (e, 0)),
                  pl.BlockSpec((tt // LANES, PEER_HEADS, ib, LANES), lambda i, e: (i, 0, e, 0)),
                  pl.BlockSpec((tt // LANES, PEER_HEADS, ib, LANES), lambda i, e: (i, 0, e, 0)),
                  pl.BlockSpec((rows // BF16_ROWS, BF16_ROWS, tt), lambda i, e: (0, 0, i)),
                  pl.BlockSpec((rows // BF16_ROWS, BF16_ROWS, tt), lambda i, e: (0, 0, i)),
                  pl.BlockSpec((1, D_MODEL), lambda i, e: (0, 0)),
                  pl.BlockSpec((1, D_MODEL), lambda i, e: (0, 0))],
        out_specs=pl.BlockSpec((tt, D_MODEL), lambda i, e: (i, 0)),
        out_shape=jax.ShapeDtypeStruct((t, D_MODEL), F32),
        scratch_shapes=[pltpu.VMEM((D_MODEL, tt), F32)],
        compiler_params=_cparams(("parallel", "arbitrary")),
        name="peer_mix_ln2",
    )(x1b, x1, u, vt, ra, ea, cb, eb, g, bta)


def _block_diag(w, per_tile):
    n, k, _ = w.shape
    w = w.reshape(n // per_tile, per_tile, k, k)
    eye = jnp.eye(per_tile, dtype=w.dtype)
    out = jnp.einsum('tpij,pq->tpiqj', w, eye)
    return out.reshape(n // per_tile, per_tile * k, per_tile * k)


def _layer(x, mem, w_in, b_gate, conv_w, conv_b, lru_wa, lru_ba, lru_wx, lru_bx, lru_lambda,
           w_mem_kv, w_br_attn, w_br_lru, w_br_mem, w_out, ln1_g, ln1_b,
           peer_wq, peer_keys, peer_u, peer_v, ln2_g, ln2_b):
    b, s, d = x.shape
    t = b * s
    xf = x.reshape(t, d)
    xb = xf.astype(BF16)
    w_in_b = w_in.astype(BF16)
    row = lambda a: a.reshape(1, -1).astype(F32)

    off_rnn = 3 * DSWA_WIDTH
    off_mq = off_rnn + 2 * D_RNN
    off_gl = off_mq + MEM_WIDTH
    mq = _matmul(xb, w_in_b, off_mq, MEM_WIDTH, BF16, tn=512).reshape(b, s, MEM_WIDTH)

    os, lses = [], []
    for gi, dil in enumerate(DSWA_DILATIONS):
        xp = xb if dil == 1 else _permute_rows(x, dil).reshape(t, d)
        w_g = jnp.concatenate([w_in_b[:, part * DSWA_WIDTH + gi * GROUP_W:
                                      part * DSWA_WIDTH + (gi + 1) * GROUP_W] for part in range(3)], axis=1)
        qkv = _matmul(xp, w_g, 0, 3 * GROUP_W, BF16, tn=3 * GROUP_W).reshape(b, dil, s // dil, 3 * GROUP_W)
        o_g, lse_g = _dswa_group(qkv, gi)
        os.append(o_g.reshape(t, GROUP_W))
        lses.append(lse_g.reshape(t, HEAD_DIM))

    per_tile = MXU_TILE // LRU_BW
    wa_bd = _block_diag(lru_wa, per_tile).astype(BF16)
    wx_bd = _block_diag(lru_wx, per_tile).astype(BF16)
    rec = _rglru(xb.reshape(b, s, d), w_in_b[:, off_rnn:off_mq], conv_w.astype(F32), row(conv_b),
                 wa_bd, row(lru_ba), wx_bd, row(lru_bx), row(lru_lambda)).reshape(t, D_RNN)

    mem_len = mem.shape[1]
    kv = _matmul(mem.reshape(b * mem_len, d).astype(BF16), w_mem_kv.astype(BF16), 0,
                 2 * MEM_WIDTH, BF16, tn=1024).reshape(b, mem_len, 2 * MEM_WIDTH)
    memo = _memattn(mq, kv).reshape(t, MEM_WIDTH)

    x1, x1b = _mix(os, lses, rec, memo, xf, w_in_b[:, off_gl:], b_gate.astype(F32), w_br_attn.astype(BF16),
                   w_br_lru.astype(BF16), w_br_mem.astype(BF16), w_out.astype(BF16),
                   row(ln1_g), row(ln1_b))

    kbd = _block_diag(peer_keys.reshape(PEER_HEADS * 2, N_KEYS, PEER_KEY_DIM // 2), 2).astype(BF16)
    scores = _peer_scores(x1b, peer_wq.astype(BF16), kbd)

    ra, ea, cb, eb = _peer_select(scores)
    ra = ra.reshape(t // LANES, PEER_HEADS, N_KEYS, LANES)
    ea = ea.reshape(t // LANES, PEER_HEADS, N_KEYS, LANES)

    out = _peer_mix(x1b, x1, peer_u.astype(BF16), peer_v.astype(BF16), ra, ea, cb, eb,
                    row(ln2_g), row(ln2_b))
    return out.reshape(b, s, d)


def kernel(x, mem, w_in, b_gate, conv_w, conv_b, lru_wa, lru_ba, lru_wx, lru_bx, lru_lambda, w_mem_kv, w_br_attn, w_br_lru, w_br_mem, w_out, ln1_g, ln1_b, peer_wq, peer_keys, peer_u, peer_v, ln2_g, ln2_b):
    h = x.astype(F32)
    depth = w_in.shape[0]
    for l in range(depth):
        h = _layer(h, mem, w_in[l], b_gate[l], conv_w[l], conv_b[l], lru_wa[l], lru_ba[l],
                   lru_wx[l], lru_bx[l], lru_lambda[l], w_mem_kv[l], w_br_attn[l], w_br_lru[l],
                   w_br_mem[l], w_out[l], ln1_g[l], ln1_b[l], peer_wq[l], peer_keys[l],
                   peer_u[l], peer_v[l], ln2_g[l], ln2_b[l])
    return h.astype(x.dtype)
```

```python
import functools
import math

import jax
import jax.numpy as jnp
from jax import lax
from jax.experimental import pallas as pl
from jax.experimental.pallas import tpu as pltpu

F32 = jnp.float32
BF16 = jnp.bfloat16

D_MODEL = 1024
N_GROUPS = 3
DSWA_DILATIONS = (1, 4, 16)
DSWA_HEADS = 4
HEAD_DIM = 128
DSWA_BLK = 128
DSWA_BATCH_BLOCKS = 4
DSWA_TILE = 2048
GROUP_W = DSWA_HEADS * HEAD_DIM
DSWA_WIDTH = N_GROUPS * GROUP_W
D_RNN = 1024
LRU_BLOCKS = 16
LRU_BW = D_RNN // LRU_BLOCKS
CONV_W = 4
LRU_C = 8.0
MEM_HEADS = 4
MEM_WIDTH = MEM_HEADS * HEAD_DIM
PEER_HEADS = 8
PEER_KEY_DIM = 256
N_KEYS = 128
N_EXPERTS = N_KEYS * N_KEYS
PEER_TOPK = 16
PEER_CHUNK_KEYS = (8,) * 2
BF16_ROWS = 16
PEER_MASK_LANES = 256
ALPHA = 2.0 ** 0.25
LN_EPS = 1e-5
NEG_INF = -1e30

LANES = 128
SUBLANES = 8
MXU_TILE = 256
VMEM_LIMIT = 56 * 1024 * 1024


def _cparams(sem):
    return pltpu.CompilerParams(dimension_semantics=sem, vmem_limit_bytes=VMEM_LIMIT)


def _dot(a, b):
    return jnp.dot(a, b, preferred_element_type=F32)


def _dot_nt(a, b):
    return lax.dot_general(a, b, (((1,), (1,)), ((), ())), preferred_element_type=F32)


def _gelu_erf(x):
    return 0.5 * x * (1.0 + lax.erf(x * (1.0 / math.sqrt(2.0))))


def _layernorm(h, g, b):
    mu = jnp.mean(h, axis=-1, keepdims=True)
    c = h - mu
    var = jnp.mean(c * c, axis=-1, keepdims=True)
    return c * lax.rsqrt(var + LN_EPS) * g + b


def _matmul_kernel(x_ref, w_ref, o_ref):
    o_ref[...] = _dot(x_ref[...], w_ref[...]).astype(o_ref.dtype)


def _matmul(x, w, col_off, n_cols, out_dtype, tm=1024, tn=1024):
    m, k = x.shape
    tn = min(tn, n_cols)
    tm = min(tm, m)
    assert m % tm == 0 and n_cols % tn == 0 and col_off % tn == 0
    off = col_off // tn
    return pl.pallas_call(
        _matmul_kernel,
        grid=(m // tm, n_cols // tn),
        in_specs=[pl.BlockSpec((tm, k), lambda i, j: (i, 0)),
                  pl.BlockSpec((k, tn), lambda i, j: (0, off + j))],
        out_specs=pl.BlockSpec((tm, tn), lambda i, j: (i, j)),
        out_shape=jax.ShapeDtypeStruct((m, n_cols), out_dtype),
        compiler_params=_cparams(("parallel", "arbitrary")),
        name="matmul",
    )(x, w)


def _permute_kernel(x_ref, o_ref, *, dilation):
    rows = x_ref.shape[1] // dilation
    for bi in range(x_ref.shape[0]):
        for r in range(dilation):
            o_ref[bi, r] = x_ref[bi, pl.ds(r, rows, stride=dilation), :].astype(o_ref.dtype)


def _permute_rows(x, dilation):
    b, s, d = x.shape
    m_len = s // dilation
    return pl.pallas_call(
        functools.partial(_permute_kernel, dilation=dilation),
        grid=(s // DSWA_TILE, d // LANES),
        in_specs=[pl.BlockSpec((b, DSWA_TILE, LANES), lambda n, c: (0, n, c))],
        out_specs=pl.BlockSpec((b, dilation, DSWA_TILE // dilation, LANES),
                               lambda n, c: (0, 0, n, c)),
        out_shape=jax.ShapeDtypeStruct((b, dilation, m_len, d), BF16),
        compiler_params=_cparams(("parallel", "parallel")),
        name=f"permute_d{dilation}",
    )(x)


def _dswa_kernel(q_ref, kc_ref, kp_ref, vc_ref, vp_ref, o_ref, lse_ref, o_scr, lse_scr, *, dilation):
    nq = q_ref.shape[2] // DSWA_BLK
    row = lax.broadcasted_iota(jnp.int32, (DSWA_BLK, DSWA_BLK), 0)
    col = lax.broadcasted_iota(jnp.int32, (DSWA_BLK, DSWA_BLK), 1)
    mask_cur = col <= row
    mask_band = col >= row
    mask_first = col >= row + jnp.where(pl.program_id(1) == 0, DSWA_BLK, 0)
    scale = 1.0 / math.sqrt(HEAD_DIM)
    bdot_nt = lambda a, b: jnp.einsum('hqd,hkd->hqk', a, b, preferred_element_type=F32)
    bdot = lambda a, b: jnp.einsum('hqk,hkd->hqd', a, b, preferred_element_type=F32)

    def heads(blocks):
        return jnp.stack([a[:, h * HEAD_DIM:(h + 1) * HEAD_DIM]
                          for a in blocks for h in range(DSWA_HEADS)], axis=0)

    def attend(blocks):
        qs, kps, kcs, vps, vcs = [], [], [], [], []
        for r, j in blocks:
            rows = slice(j * DSWA_BLK, (j + 1) * DSWA_BLK)
            prows = slice((j - 1) * DSWA_BLK, j * DSWA_BLK)
            qs.append(q_ref[0, r, rows, :])
            kcs.append(kc_ref[0, r, rows, :])
            vcs.append(vc_ref[0, r, rows, :])
            kps.append(kp_ref[0, r] if j == 0 else kc_ref[0, r, prows, :])
            vps.append(vp_ref[0, r] if j == 0 else vc_ref[0, r, prows, :])
        mask_prev = mask_first if blocks[0][1] == 0 else mask_band
        q4 = heads(qs)
        s_p = jnp.where(mask_prev[None], bdot_nt(q4, heads(kps)) * scale, NEG_INF)
        s_c = jnp.where(mask_cur[None], bdot_nt(q4, heads(kcs)) * scale, NEG_INF)
        m = jnp.maximum(jnp.max(s_p, axis=-1, keepdims=True), jnp.max(s_c, axis=-1, keepdims=True))
        p_p = jnp.exp(s_p - m)
        p_c = jnp.exp(s_c - m)
        l = jnp.sum(p_p, axis=-1, keepdims=True) + jnp.sum(p_c, axis=-1, keepdims=True)
        o = (bdot(p_p.astype(BF16), heads(vps)) + bdot(p_c.astype(BF16), heads(vcs))) / l
        lse = m + jnp.log(l)
        for bi, (r, j) in enumerate(blocks):
            dst = pl.ds(j * DSWA_BLK * dilation + r, DSWA_BLK, stride=dilation)
            lse_tile = jnp.zeros((DSWA_BLK, HEAD_DIM), F32)
            for h in range(DSWA_HEADS):
                o_scr[h, dst, :] = o[bi * DSWA_HEADS + h]
                lse_tile = jnp.where(col == h, lse[bi * DSWA_HEADS + h], lse_tile)
            lse_scr[dst, :] = lse_tile

    for first in (True, False):
        group = [(r, j) for j in range(nq) for r in range(dilation) if (j == 0) == first]
        for i in range(0, len(group), DSWA_BATCH_BLOCKS):
            attend(group[i:i + DSWA_BATCH_BLOCKS])
    for h in range(DSWA_HEADS):
        o_ref[0, :, h * HEAD_DIM:(h + 1) * HEAD_DIM] = o_scr[h].astype(o_ref.dtype)
    lse_ref[0] = lse_scr[...]


def _dswa_group(qkv, group):
    b, dilation, m_len, _ = qkv.shape
    s = dilation * m_len
    assert s % DSWA_TILE == 0 and DSWA_TILE % (dilation * DSWA_BLK) == 0
    rows = DSWA_TILE // dilation
    nq = rows // DSWA_BLK

    def cur(c):
        return pl.BlockSpec((1, dilation, rows, GROUP_W), lambda bi, n: (bi, 0, n, c))

    def prev(c):
        return pl.BlockSpec((1, dilation, DSWA_BLK, GROUP_W),
                            lambda bi, n: (bi, 0, jnp.maximum(n * nq - 1, 0), c))

    return pl.pallas_call(
        functools.partial(_dswa_kernel, dilation=dilation),
        grid=(b, s // DSWA_TILE),
        in_specs=[cur(0), cur(1), prev(1), cur(2), prev(2)],
        out_specs=[pl.BlockSpec((1, DSWA_TILE, GROUP_W), lambda bi, n: (bi, n, 0)),
                   pl.BlockSpec((1, DSWA_TILE, HEAD_DIM), lambda bi, n: (bi, n, 0))],
        out_shape=[jax.ShapeDtypeStruct((b, s, GROUP_W), BF16),
                   jax.ShapeDtypeStruct((b, s, HEAD_DIM), F32)],
        scratch_shapes=[pltpu.VMEM((DSWA_HEADS, DSWA_TILE, HEAD_DIM), F32),
                        pltpu.VMEM((DSWA_TILE, HEAD_DIM), F32)],
        compiler_params=_cparams(("parallel", "arbitrary")),
        name=f"dswa_g{group}",
    )(qkv, qkv, qkv, qkv, qkv)


def _linear_scan(a, b, h0):
    ts, d = a.shape
    a = a.reshape(ts // SUBLANES, SUBLANES, d)
    b = b.reshape(ts // SUBLANES, SUBLANES, d)
    row = lax.broadcasted_iota(jnp.int32, a.shape, 1)
    step = 1
    while step < SUBLANES:
        keep = row >= step
        a_s = jnp.where(keep, pltpu.roll(a, step, 1), 1.0)
        b_s = jnp.where(keep, pltpu.roll(b, step, 1), 0.0)
        b = a * b_s + b
        a = a * a_s
        step *= 2
    hs, carry = [], h0
    for g in range(ts // SUBLANES):
        hg = a[g] * carry + b[g]
        hs.append(hg)
        carry = hg[SUBLANES - 1:SUBLANES, :]
    return jnp.concatenate(hs, axis=0), carry


def _rglru_kernel(xb_ref, halo_ref, w_ref, cw_ref, cb_ref, wa_ref, ba_ref, wx_ref, bx_ref,
                  lam_ref, o_ref, h_ref, *, ts):
    si = pl.program_id(1)

    @pl.when(si == 0)
    def _():
        h_ref[...] = jnp.zeros_like(h_ref)

    z = _dot(xb_ref[0], w_ref[...])
    x = z[:, :D_RNN]
    yg = z[:, D_RNN:]
    halo = _dot(halo_ref[0], w_ref[:, :D_RNN])[BF16_ROWS - SUBLANES:, :]
    halo = jnp.where(si > 0, halo, 0.0)
    xfull = jnp.concatenate([halo, x], axis=0)
    xc = cb_ref[...] + cw_ref[CONV_W - 1:CONV_W, :] * x
    for k in range(1, CONV_W):
        xc = xc + cw_ref[CONV_W - 1 - k:CONV_W - k, :] * pltpu.roll(xfull, k, 0)[SUBLANES:]

    xb = xc.astype(BF16)
    nblk = D_RNN // MXU_TILE
    ra = jnp.concatenate(
        [_dot(xb[:, c * MXU_TILE:(c + 1) * MXU_TILE], wa_ref[c]) for c in range(nblk)], axis=1)
    rx = jnp.concatenate(
        [_dot(xb[:, c * MXU_TILE:(c + 1) * MXU_TILE], wx_ref[c]) for c in range(nblk)], axis=1)
    r = jax.nn.sigmoid(ra + ba_ref[...])
    gate_i = jax.nn.sigmoid(rx + bx_ref[...])
    neg_lam = -lam_ref[...]
    softplus = jnp.maximum(neg_lam, 0.0) + jnp.log1p(jnp.exp(-jnp.abs(neg_lam)))
    log_a = (-LRU_C) * r * softplus
    a = jnp.exp(log_a)
    bterm = jnp.sqrt(1.0 - a * a) * gate_i * xc

    h, carry = _linear_scan(a, bterm, h_ref[0:1, :])
    h_ref[0:1, :] = carry
    o_ref[0] = (h * jax.nn.gelu(yg, approximate=True)).astype(o_ref.dtype)


def _rglru(xb, w_rnn, conv_w, conv_b, wa_bd, ba, wx_bd, bx, lam, ts=512):
    b, s, d = xb.shape
    assert s % ts == 0
    full = lambda shape: pl.BlockSpec(shape, lambda bi, si: (0,) * len(shape))
    return pl.pallas_call(
        functools.partial(_rglru_kernel, ts=ts),
        grid=(b, s // ts),
        in_specs=[pl.BlockSpec((1, ts, d), lambda bi, si: (bi, si, 0)),
                  pl.BlockSpec((1, BF16_ROWS, d),
                               lambda bi, si: (bi, jnp.maximum(si * (ts // BF16_ROWS) - 1, 0), 0)),
                  full(w_rnn.shape),
                  full((CONV_W, D_RNN)), full((1, D_RNN)),
                  full(wa_bd.shape), full((1, D_RNN)),
                  full(wx_bd.shape), full((1, D_RNN)), full((1, D_RNN))],
        out_specs=pl.BlockSpec((1, ts, D_RNN), lambda bi, si: (bi, si, 0)),
        out_shape=jax.ShapeDtypeStruct((b, s, D_RNN), BF16),
        scratch_shapes=[pltpu.VMEM((SUBLANES, D_RNN), F32)],
        compiler_params=_cparams(("parallel", "arbitrary")),
        name="rglru",
    )(xb, xb, w_rnn, conv_w, conv_b, wa_bd, ba, wx_bd, bx, lam)


def _memattn_kernel(q_ref, kv_ref, o_ref):
    scale = 1.0 / math.sqrt(HEAD_DIM)
    for h in range(MEM_HEADS):
        hs = slice(h * HEAD_DIM, (h + 1) * HEAD_DIM)
        vs = slice(MEM_WIDTH + h * HEAD_DIM, MEM_WIDTH + (h + 1) * HEAD_DIM)
        s = _dot_nt(q_ref[0, :, hs], kv_ref[0, :, hs]) * scale
        m = jnp.max(s, axis=-1, keepdims=True)
        p = jnp.exp(s - m)
        l = jnp.sum(p, axis=-1, keepdims=True)
        o_ref[0, :, hs] = (_dot(p.astype(BF16), kv_ref[0, :, vs]) / l).astype(o_ref.dtype)


def _memattn(mq, kv, ts=1024):
    b, s, _ = mq.shape
    mem_len = kv.shape[1]
    return pl.pallas_call(
        _memattn_kernel,
        grid=(b, s // ts),
        in_specs=[pl.BlockSpec((1, ts, MEM_WIDTH), lambda bi, si: (bi, si, 0)),
                  pl.BlockSpec((1, mem_len, 2 * MEM_WIDTH), lambda bi, si: (bi, 0, 0))],
        out_specs=pl.BlockSpec((1, ts, MEM_WIDTH), lambda bi, si: (bi, si, 0)),
        out_shape=jax.ShapeDtypeStruct((b, s, MEM_WIDTH), BF16),
        compiler_params=_cparams(("parallel", "arbitrary")),
        name="memattn",
    )(mq, kv)


def _mix_kernel(o0_ref, o1_ref, o2_ref, l0_ref, l1_ref, l2_ref, rec_ref, memo_ref, x_ref,
                wg_ref, bg_ref, wa_ref, wl_ref, wm_ref, wo_ref, g_ref, b_ref, x1_ref, x1b_ref):
    l0, l1, l2 = l0_ref[...], l1_ref[...], l2_ref[...]
    mx = jnp.maximum(jnp.maximum(l0, l1), l2)
    e0, e1, e2 = jnp.exp(l0 - mx), jnp.exp(l1 - mx), jnp.exp(l2 - mx)
    inv = 1.0 / (e0 + e1 + e2)
    w0, w1, w2 = e0 * inv, e1 * inv, e2 * inv
    parts = []
    for h in range(DSWA_HEADS):
        hs = slice(h * HEAD_DIM, (h + 1) * HEAD_DIM)
        parts.append((w0[:, h:h + 1] * o0_ref[:, hs].astype(F32)
                      + w1[:, h:h + 1] * o1_ref[:, hs].astype(F32)
                      + w2[:, h:h + 1] * o2_ref[:, hs].astype(F32)).astype(BF16))
    attn = jnp.concatenate(parts, axis=1)
    d = D_MODEL
    gl = _dot(x_ref[...].astype(BF16), wg_ref[...])
    gate = lambda j: jax.nn.sigmoid(gl[:, j * d:(j + 1) * d] + bg_ref[j:j + 1, :])
    merged = (gate(0) * _dot(attn, wa_ref[...])
              + gate(1) * _dot(rec_ref[...], wl_ref[...])
              + gate(2) * _dot(memo_ref[...], wm_ref[...]))
    mix = _dot(merged.astype(BF16), wo_ref[...])
    x1 = _layernorm(ALPHA * x_ref[...] + mix, g_ref[...], b_ref[...])
    x1_ref[...] = x1
    x1b_ref[...] = x1.astype(BF16)


def _mix(os, lses, rec, memo, x, w_gate, b_gate, wa, wl, wm, wo, g, bta, tt=512):
    t = x.shape[0]
    rows = lambda w: pl.BlockSpec((tt, w), lambda i: (i, 0))
    full = lambda a: pl.BlockSpec(a.shape, lambda i: (0,) * a.ndim, pipeline_mode=pl.Buffered(1))
    return pl.pallas_call(
        _mix_kernel,
        grid=(t // tt,),
        in_specs=[rows(GROUP_W)] * 3 + [rows(HEAD_DIM)] * 3 + [
            rows(D_RNN), rows(MEM_WIDTH), rows(D_MODEL), full(w_gate),
            full(b_gate), full(wa), full(wl), full(wm), full(wo), full(g), full(bta)],
        out_specs=[rows(D_MODEL), rows(D_MODEL)],
        out_shape=[jax.ShapeDtypeStruct((t, D_MODEL), F32), jax.ShapeDtypeStruct((t, D_MODEL), BF16)],
        compiler_params=_cparams(("parallel",)),
        name="mix_ln1",
    )(*os, *lses, rec, memo, x, w_gate, b_gate, wa, wl, wm, wo, g, bta)


def _peer_scores_kernel(x_ref, wq_ref, kbd_ref, o_ref):
    q = _dot(x_ref[...], wq_ref[...]).astype(BF16)
    for h in range(PEER_HEADS):
        hs = slice(h * PEER_KEY_DIM, (h + 1) * PEER_KEY_DIM)
        o_ref[hs, :] = _dot_nt(kbd_ref[h], q[:, hs])


def _peer_scores(x1b, wq, kbd, tt=1024):
    t = x1b.shape[0]
    w = PEER_HEADS * PEER_KEY_DIM
    return pl.pallas_call(
        _peer_scores_kernel,
        grid=(t // tt,),
        in_specs=[pl.BlockSpec((tt, D_MODEL), lambda i: (i, 0)),
                  pl.BlockSpec(wq.shape, lambda i: (0, 0)),
                  pl.BlockSpec(kbd.shape, lambda i: (0, 0, 0))],
        out_specs=pl.BlockSpec((w, tt), lambda i: (0, i)),
        out_shape=jax.ShapeDtypeStruct((w, t), F32),
        compiler_params=_cparams(("parallel",)),
        name="peer_scores",
    )(x1b, wq, kbd)


_PEER_CANDS = tuple((p, q) for p in range(PEER_TOPK) for q in range(PEER_TOPK)
                    if (p + 1) * (q + 1) <= PEER_TOPK)


def _top_positions(s, exact_ties):
    key = lax.broadcasted_iota(jnp.int32, s.shape, 0).astype(F32)
    pos = jnp.full(s.shape, float(PEER_TOPK), F32)
    work = s
    tops = []
    for p in range(PEER_TOPK):
        mx = jnp.max(work, axis=0, keepdims=True)
        if exact_ties:
            first = jnp.min(jnp.where(work == mx, key, float(N_KEYS)), axis=0, keepdims=True)
            hit = key == first
        else:
            hit = work == mx
        pos = jnp.where(hit, float(p), pos)
        work = jnp.where(hit, -jnp.inf, work)
        tops.append(mx)
    return pos, tops


def _store_lane_tiles(ref, h, val):
    for c in range(ref.shape[0]):
        ref[c, h * N_KEYS:(h + 1) * N_KEYS, :] = val[:, c * LANES:(c + 1) * LANES]


def _peer_select_body(sc_ref, ra_ref, ea_ref, cb_ref, eb_ref, exact_ties):
    tt = sc_ref.shape[1]
    kg = N_KEYS // BF16_ROWS
    pos_b, tops_a, tops_b = [], [], []
    marked = jnp.zeros((1, tt), F32)
    for h in range(PEER_HEADS):
        base = h * PEER_KEY_DIM
        pa, ta = _top_positions(sc_ref[base:base + N_KEYS, :], exact_ties)
        pb, tb = _top_positions(sc_ref[base + N_KEYS:base + 2 * N_KEYS, :], exact_ties)
        _store_lane_tiles(ra_ref, h, pa)
        pos_b.append(pb)
        tops_a.append(ta)
        tops_b.append(tb)
        if not exact_ties:
            for pos in (pa, pb):
                n_marked = jnp.sum(jnp.where(pos < float(PEER_TOPK), 1.0, 0.0), axis=0, keepdims=True)
                marked = jnp.maximum(marked, n_marked)
    a_top = [jnp.concatenate([tops_a[h][p] for h in range(PEER_HEADS)], axis=0)
             for p in range(PEER_TOPK)]
    b_top = [jnp.concatenate([tops_b[h][q] for h in range(PEER_HEADS)], axis=0)
             for q in range(PEER_TOPK)]
    sums = [a_top[p] + b_top[q] for p, q in _PEER_CANDS]
    n = len(sums)
    rank = [jnp.full((PEER_HEADS, tt), float(n - 1 - c), F32) for c in range(n)]
    for c in range(n):
        for c2 in range(c):
            c2_wins = jnp.where(sums[c2] >= sums[c], 1.0, 0.0)
            rank[c] = rank[c] + c2_wins
            rank[c2] = rank[c2] - c2_wins
    cnt = [jnp.zeros((PEER_HEADS, tt), F32) for _ in range(PEER_TOPK)]
    zsum = jnp.zeros((PEER_HEADS, tt), F32)
    for c, (p, q) in enumerate(_PEER_CANDS):
        sel = rank[c] < float(PEER_TOPK)
        cnt[q] = cnt[q] + jnp.where(sel, 1.0, 0.0)
        zsum = zsum + jnp.where(sel, jnp.exp(sums[c] - sums[0]), 0.0)
    inv_z = 1.0 / zsum
    for h in range(PEER_HEADS):
        base = h * PEER_KEY_DIM
        rows = slice(h * N_KEYS, (h + 1) * N_KEYS)
        posb = pos_b[h].astype(BF16).reshape(kg, BF16_ROWS, tt)
        cb = jnp.zeros((kg, BF16_ROWS, tt), BF16)
        for q in range(PEER_TOPK):
            cnt_q = jnp.broadcast_to(cnt[q][h:h + 1, :], (BF16_ROWS, tt)).astype(BF16)
            cb = jnp.where(posb == float(q), cnt_q[None], cb)
        cb_ref[h * kg:(h + 1) * kg] = cb
        _store_lane_tiles(ea_ref, h, jnp.exp(sc_ref[base:base + N_KEYS, :] - a_top[0][h:h + 1, :])
                          * inv_z[h:h + 1, :])
        eb = jnp.exp(sc_ref[base + N_KEYS:base + 2 * N_KEYS, :] - b_top[0][h:h + 1, :])
        eb_ref[h * kg:(h + 1) * kg] = eb.astype(BF16).reshape(kg, BF16_ROWS, tt)
    return marked


def _peer_select_kernel(sc_ref, ra_ref, ea_ref, cb_ref, eb_ref):
    marked = _peer_select_body(sc_ref, ra_ref, ea_ref, cb_ref, eb_ref, exact_ties=False)

    @pl.when(jnp.max(marked) > float(PEER_TOPK))
    def _():
        _peer_select_body(sc_ref, ra_ref, ea_ref, cb_ref, eb_ref, exact_ties=True)


def _peer_select(sc, tt=256):
    w, t = sc.shape
    rows = PEER_HEADS * N_KEYS
    out_spec_a = pl.BlockSpec((tt // LANES, rows, LANES), lambda i: (i, 0, 0))
    shape_a = jax.ShapeDtypeStruct((t // LANES, rows, LANES), F32)
    out_spec_b = pl.BlockSpec((rows // BF16_ROWS, BF16_ROWS, tt), lambda i: (0, 0, i))
    shape_b = jax.ShapeDtypeStruct((rows // BF16_ROWS, BF16_ROWS, t), BF16)
    return pl.pallas_call(
        _peer_select_kernel,
        grid=(t // tt,),
        in_specs=[pl.BlockSpec((w, tt), lambda i: (0, i))],
        out_specs=[out_spec_a, out_spec_a, out_spec_b, out_spec_b],
        out_shape=[shape_a, shape_a, shape_b, shape_b],
        compiler_params=_cparams(("parallel",)),
        name="peer_select",
    )(sc)


def _row_on_sublanes(ref, h, i, lane0, lanes):
    tiles = range(lane0 // LANES, (lane0 + lanes) // LANES)
    return jnp.concatenate([ref[c, h, pl.ds(i, BF16_ROWS, stride=0), :] for c in tiles], axis=1)


def _masked_acts(hmat, ra_ref, ea_ref, cb_ref, eb_ref, key0):
    tt = hmat.shape[1]
    kg = N_KEYS // BF16_ROWS
    ys = []
    for k in range(hmat.shape[0] // N_KEYS):
        rows = slice(k * N_KEYS, (k + 1) * N_KEYS)
        ws = []
        for l0 in range(0, tt, PEER_MASK_LANES):
            ls = slice(l0, l0 + PEER_MASK_LANES)
            w = jnp.zeros((kg, BF16_ROWS, PEER_MASK_LANES), BF16)
            for h in range(PEER_HEADS):
                hs = slice(h * kg, (h + 1) * kg)
                ra = _row_on_sublanes(ra_ref, h, key0 + k, l0, PEER_MASK_LANES).astype(BF16)
                ea = _row_on_sublanes(ea_ref, h, key0 + k, l0, PEER_MASK_LANES).astype(BF16)
                w = w + eb_ref[hs, :, ls] * jnp.where(cb_ref[hs, :, ls] > ra[None], ea[None],
                                                      jnp.zeros_like(w))
            ws.append(w.reshape(N_KEYS, PEER_MASK_LANES))
        ys.append(jnp.concatenate(ws, axis=1) * _gelu_erf(hmat[rows, :]).astype(BF16))
    return jnp.concatenate(ys, axis=0)


def _peer_mix_kernel(x1b_ref, x1_ref, u_ref, vt_ref, ra_ref, ea_ref, cb_ref, eb_ref,
                     g_ref, b_ref, o_ref, acc_ref, *, ib):
    e = pl.program_id(1)

    @pl.when(e == 0)
    def _():
        acc_ref[...] = jnp.zeros_like(acc_ref)

    x1b = x1b_ref[...]
    acc = acc_ref[...]
    assert sum(PEER_CHUNK_KEYS) == ib
    starts = [sum(PEER_CHUNK_KEYS[:c]) for c in range(len(PEER_CHUNK_KEYS) + 1)]
    chunk_rows = lambda c: slice(starts[c] * N_KEYS, starts[c + 1] * N_KEYS)
    n_chunks = len(PEER_CHUNK_KEYS)
    h_next = _dot_nt(u_ref[chunk_rows(0), :], x1b)
    for c in range(n_chunks):
        hc = h_next
        if c + 1 < n_chunks:
            h_next = _dot_nt(u_ref[chunk_rows(c + 1), :], x1b)
        y = _masked_acts(hc, ra_ref, ea_ref, cb_ref, eb_ref, starts[c])
        acc = acc + _dot(vt_ref[:, chunk_rows(c)], y)
    acc_ref[...] = acc

    @pl.when(e == pl.num_programs(1) - 1)
    def _():
        o_ref[...] = _layernorm(ALPHA * x1_ref[...] + acc_ref[...].T, g_ref[...], b_ref[...])


def _peer_mix(x1b, x1, u, vt, ra, ea, cb, eb, g, bta, tt=512, te=sum(PEER_CHUNK_KEYS) * N_KEYS):
    t = x1.shape[0]
    ib = te // N_KEYS
    ne = N_EXPERTS // te
    rows = PEER_HEADS * N_KEYS
    return pl.pallas_call(
        functools.partial(_peer_mix_kernel, ib=ib),
        grid=(t // tt, ne),
        in_specs=[pl.BlockSpec((tt, D_MODEL), lambda i, e: (i, 0)),
                  pl.BlockSpec((tt, D_MODEL), lambda i, e: (i, 0)),
                  pl.BlockSpec((te, D_MODEL), lambda i, e: (e, 0)),
                  pl.BlockSpec((D_MODEL, te), lambda i, e: (0, e)),
                  pl.BlockSpec((tt // LANES, PEER_HEADS, ib, LANES), lambda i, e: (i, 0, e, 0)),
                  pl.BlockSpec((tt // LANES, PEER_HEADS, ib, LANES), lambda i, e: (i, 0, e, 0)),
                  pl.BlockSpec((rows // BF16_ROWS, BF16_ROWS, tt), lambda i, e: (0, 0, i)),
                  pl.BlockSpec((rows // BF16_ROWS, BF16_ROWS, tt), lambda i, e: (0, 0, i)),
                  pl.BlockSpec((1, D_MODEL), lambda i, e: (0, 0)),
                  pl.BlockSpec((1, D_MODEL), lambda i, e: (0, 0))],
        out_specs=pl.BlockSpec((tt, D_MODEL), lambda i, e: (i, 0)),
        out_shape=jax.ShapeDtypeStruct((t, D_MODEL), F32),
        scratch_shapes=[pltpu.VMEM((D_MODEL, tt), F32)],
        compiler_params=_cparams(("parallel", "arbitrary")),
        name="peer_mix_ln2",
    )(x1b, x1, u, vt, ra, ea, cb, eb, g, bta)


def _block_diag(w, per_tile):
    n, k, _ = w.shape
    w = w.reshape(n // per_tile, per_tile, k, k)
    eye = jnp.eye(per_tile, dtype=w.dtype)
    out = jnp.einsum('tpij,pq->tpiqj', w, eye)
    return out.reshape(n // per_tile, per_tile * k, per_tile * k)


def _layer(x, mem, w_in, b_gate, conv_w, conv_b, lru_wa, lru_ba, lru_wx, lru_bx, lru_lambda,
           w_mem_kv, w_br_attn, w_br_lru, w_br_mem, w_out, ln1_g, ln1_b,
           peer_wq, peer_keys, peer_u, peer_v, ln2_g, ln2_b):
    b, s, d = x.shape
    t = b * s
    xf = x.reshape(t, d)
    xb = xf.astype(BF16)
    w_in_b = w_in.astype(BF16)
    row = lambda a: a.reshape(1, -1).astype(F32)

    off_rnn = 3 * DSWA_WIDTH
    off_mq = off_rnn + 2 * D_RNN
    off_gl = off_mq + MEM_WIDTH
    mq = _matmul(xb, w_in_b, off_mq, MEM_WIDTH, BF16, tn=512).reshape(b, s, MEM_WIDTH)

    os, lses = [], []
    for gi, dil in enumerate(DSWA_DILATIONS):
        xp = xb if dil == 1 else _permute_rows(x, dil).reshape(t, d)
        w_g = jnp.concatenate([w_in_b[:, part * DSWA_WIDTH + gi * GROUP_W:
                                      part * DSWA_WIDTH + (gi + 1) * GROUP_W] for part in range(3)], axis=1)
        qkv = _matmul(xp, w_g, 0, 3 * GROUP_W, BF16, tn=3 * GROUP_W).reshape(b, dil, s // dil, 3 * GROUP_W)
        o_g, lse_g = _dswa_group(qkv, gi)
        os.append(o_g.reshape(t, GROUP_W))
        lses.append(lse_g.reshape(t, HEAD_DIM))

    per_tile = MXU_TILE // LRU_BW
    wa_bd = _block_diag(lru_wa, per_tile).astype(BF16)
    wx_bd = _block_diag(lru_wx, per_tile).astype(BF16)
    rec = _rglru(xb.reshape(b, s, d), w_in_b[:, off_rnn:off_mq], conv_w.astype(F32), row(conv_b),
                 wa_bd, row(lru_ba), wx_bd, row(lru_bx), row(lru_lambda)).reshape(t, D_RNN)

    mem_len = mem.shape[1]
    kv = _matmul(mem.reshape(b * mem_len, d).astype(BF16), w_mem_kv.astype(BF16), 0,
                 2 * MEM_WIDTH, BF16, tn=1024).reshape(b, mem_len, 2 * MEM_WIDTH)
    memo = _memattn(mq, kv).reshape(t, MEM_WIDTH)

    x1, x1b = _mix(os, lses, rec, memo, xf, w_in_b[:, off_gl:], b_gate.astype(F32), w_br_attn.astype(BF16),
                   w_br_lru.astype(BF16), w_br_mem.astype(BF16), w_out.astype(BF16),
                   row(ln1_g), row(ln1_b))

    kbd = _block_diag(peer_keys.reshape(PEER_HEADS * 2, N_KEYS, PEER_KEY_DIM // 2), 2).astype(BF16)
    scores = _peer_scores(x1b, peer_wq.astype(BF16), kbd)

    ra, ea, cb, eb = _peer_select(scores)
    ra = ra.reshape(t // LANES, PEER_HEADS, N_KEYS, LANES)
    ea = ea.reshape(t // LANES, PEER_HEADS, N_KEYS, LANES)

    out = _peer_mix(x1b, x1, peer_u.astype(BF16), peer_v.T.astype(BF16), ra, ea, cb, eb,
                    row(ln2_g), row(ln2_b))
    return out.reshape(b, s, d)


def kernel(x, mem, w_in, b_gate, conv_w, conv_b, lru_wa, lru_ba, lru_wx, lru_bx, lru_lambda, w_mem_kv, w_br_attn, w_br_lru, w_br_mem, w_out, ln1_g, ln1_b, peer_wq, peer_keys, peer_u, peer_v, ln2_g, ln2_b):
    h = x.astype(F32)
    depth = w_in.shape[0]
    for l in range(depth):
        h = _layer(h, mem, w_in[l], b_gate[l], conv_w[l], conv_b[l], lru_wa[l], lru_ba[l],
                   lru_wx[l], lru_bx[l], lru_lambda[l], w_mem_kv[l], w_br_attn[l], w_br_lru[l],
                   w_br_mem[l], w_out[l], ln1_g[l], ln1_b[l], peer_wq[l], peer_keys[l],
                   peer_u[l], peer_v[l], ln2_g[l], ln2_b[l])
    return h.astype(x.dtype)
```

```python
import functools
import math

import jax
import jax.numpy as jnp
from jax import lax
from jax.experimental import pallas as pl
from jax.experimental.pallas import tpu as pltpu

F32 = jnp.float32
BF16 = jnp.bfloat16

D_MODEL = 1024
N_GROUPS = 3
DSWA_DILATIONS = (1, 4, 16)
DSWA_HEADS = 4
HEAD_DIM = 128
DSWA_BLK = 128
DSWA_BATCH_BLOCKS = 4
DSWA_TILE = 2048
GROUP_W = DSWA_HEADS * HEAD_DIM
DSWA_WIDTH = N_GROUPS * GROUP_W
D_RNN = 1024
LRU_BLOCKS = 16
LRU_BW = D_RNN // LRU_BLOCKS
CONV_W = 4
LRU_C = 8.0
MEM_HEADS = 4
MEM_WIDTH = MEM_HEADS * HEAD_DIM
PEER_HEADS = 8
PEER_KEY_DIM = 256
N_KEYS = 128
N_EXPERTS = N_KEYS * N_KEYS
PEER_TOPK = 16
PEER_CHUNK_KEYS = (8,) * 2
BF16_ROWS = 16
PEER_MASK_LANES = 256
PEER_MASK_GROUPS = 4
ALPHA = 2.0 ** 0.25
LN_EPS = 1e-5
NEG_INF = -1e30

LANES = 128
SUBLANES = 8
MXU_TILE = 256
VMEM_LIMIT = 56 * 1024 * 1024


def _cparams(sem):
    return pltpu.CompilerParams(dimension_semantics=sem, vmem_limit_bytes=VMEM_LIMIT)


def _dot(a, b):
    return jnp.dot(a, b, preferred_element_type=F32)


def _dot_nt(a, b):
    return lax.dot_general(a, b, (((1,), (1,)), ((), ())), preferred_element_type=F32)


def _gelu_erf(x):
    return 0.5 * x * (1.0 + lax.erf(x * (1.0 / math.sqrt(2.0))))


def _layernorm(h, g, b):
    mu = jnp.mean(h, axis=-1, keepdims=True)
    c = h - mu
    var = jnp.mean(c * c, axis=-1, keepdims=True)
    return c * lax.rsqrt(var + LN_EPS) * g + b


def _matmul_kernel(x_ref, w_ref, o_ref):
    o_ref[...] = _dot(x_ref[...], w_ref[...]).astype(o_ref.dtype)


def _matmul(x, w, col_off, n_cols, out_dtype, tm=1024, tn=1024):
    m, k = x.shape
    tn = min(tn, n_cols)
    tm = min(tm, m)
    assert m % tm == 0 and n_cols % tn == 0 and col_off % tn == 0
    off = col_off // tn
    return pl.pallas_call(
        _matmul_kernel,
        grid=(m // tm, n_cols // tn),
        in_specs=[pl.BlockSpec((tm, k), lambda i, j: (i, 0)),
                  pl.BlockSpec((k, tn), lambda i, j: (0, off + j))],
        out_specs=pl.BlockSpec((tm, tn), lambda i, j: (i, j)),
        out_shape=jax.ShapeDtypeStruct((m, n_cols), out_dtype),
        compiler_params=_cparams(("parallel", "arbitrary")),
        name="matmul",
    )(x, w)


def _permute_kernel(x_ref, o_ref, *, dilation):
    rows = x_ref.shape[1] // dilation
    for bi in range(x_ref.shape[0]):
        for r in range(dilation):
            o_ref[bi, r] = x_ref[bi, pl.ds(r, rows, stride=dilation), :].astype(o_ref.dtype)


def _permute_rows(x, dilation):
    b, s, d = x.shape
    m_len = s // dilation
    return pl.pallas_call(
        functools.partial(_permute_kernel, dilation=dilation),
        grid=(s // DSWA_TILE, d // LANES),
        in_specs=[pl.BlockSpec((b, DSWA_TILE, LANES), lambda n, c: (0, n, c))],
        out_specs=pl.BlockSpec((b, dilation, DSWA_TILE // dilation, LANES),
                               lambda n, c: (0, 0, n, c)),
        out_shape=jax.ShapeDtypeStruct((b, dilation, m_len, d), BF16),
        compiler_params=_cparams(("parallel", "parallel")),
        name=f"permute_d{dilation}",
    )(x)


def _dswa_kernel(q_ref, kc_ref, kp_ref, vc_ref, vp_ref, o_ref, lse_ref, o_scr, lse_scr, *, dilation):
    nq = q_ref.shape[2] // DSWA_BLK
    row = lax.broadcasted_iota(jnp.int32, (DSWA_BLK, DSWA_BLK), 0)
    col = lax.broadcasted_iota(jnp.int32, (DSWA_BLK, DSWA_BLK), 1)
    mask_cur = col <= row
    mask_band = col >= row
    mask_first = col >= row + jnp.where(pl.program_id(1) == 0, DSWA_BLK, 0)
    scale = 1.0 / math.sqrt(HEAD_DIM)
    bdot_nt = lambda a, b: jnp.einsum('hqd,hkd->hqk', a, b, preferred_element_type=F32)
    bdot = lambda a, b: jnp.einsum('hqk,hkd->hqd', a, b, preferred_element_type=F32)

    def heads(blocks):
        return jnp.stack([a[:, h * HEAD_DIM:(h + 1) * HEAD_DIM]
                          for a in blocks for h in range(DSWA_HEADS)], axis=0)

    def attend(blocks):
        qs, kps, kcs, vps, vcs = [], [], [], [], []
        for r, j in blocks:
            rows = slice(j * DSWA_BLK, (j + 1) * DSWA_BLK)
            prows = slice((j - 1) * DSWA_BLK, j * DSWA_BLK)
            qs.append(q_ref[0, r, rows, :])
            kcs.append(kc_ref[0, r, rows, :])
            vcs.append(vc_ref[0, r, rows, :])
            kps.append(kp_ref[0, r] if j == 0 else kc_ref[0, r, prows, :])
            vps.append(vp_ref[0, r] if j == 0 else vc_ref[0, r, prows, :])
        mask_prev = mask_first if blocks[0][1] == 0 else mask_band
        q4 = heads(qs)
        s_p = jnp.where(mask_prev[None], bdot_nt(q4, heads(kps)) * scale, NEG_INF)
        s_c = jnp.where(mask_cur[None], bdot_nt(q4, heads(kcs)) * scale, NEG_INF)
        m = jnp.maximum(jnp.max(s_p, axis=-1, keepdims=True), jnp.max(s_c, axis=-1, keepdims=True))
        p_p = jnp.exp(s_p - m)
        p_c = jnp.exp(s_c - m)
        l = jnp.sum(p_p, axis=-1, keepdims=True) + jnp.sum(p_c, axis=-1, keepdims=True)
        o = (bdot(p_p.astype(BF16), heads(vps)) + bdot(p_c.astype(BF16), heads(vcs))) / l
        lse = m + jnp.log(l)
        for bi, (r, j) in enumerate(blocks):
            dst = pl.ds(j * DSWA_BLK * dilation + r, DSWA_BLK, stride=dilation)
            lse_tile = jnp.zeros((DSWA_BLK, HEAD_DIM), F32)
            for h in range(DSWA_HEADS):
                o_scr[h, dst, :] = o[bi * DSWA_HEADS + h]
                lse_tile = jnp.where(col == h, lse[bi * DSWA_HEADS + h], lse_tile)
            lse_scr[dst, :] = lse_tile

    for first in (True, False):
        group = [(r, j) for j in range(nq) for r in range(dilation) if (j == 0) == first]
        for i in range(0, len(group), DSWA_BATCH_BLOCKS):
            attend(group[i:i + DSWA_BATCH_BLOCKS])
    for h in range(DSWA_HEADS):
        o_ref[0, :, h * HEAD_DIM:(h + 1) * HEAD_DIM] = o_scr[h].astype(o_ref.dtype)
    lse_ref[0] = lse_scr[...]


def _dswa_group(qkv, group):
    b, dilation, m_len, _ = qkv.shape
    s = dilation * m_len
    assert s % DSWA_TILE == 0 and DSWA_TILE % (dilation * DSWA_BLK) == 0
    rows = DSWA_TILE // dilation
    nq = rows // DSWA_BLK

    def cur(c):
        return pl.BlockSpec((1, dilation, rows, GROUP_W), lambda bi, n: (bi, 0, n, c))

    def prev(c):
        return pl.BlockSpec((1, dilation, DSWA_BLK, GROUP_W),
                            lambda bi, n: (bi, 0, jnp.maximum(n * nq - 1, 0), c))

    return pl.pallas_call(
        functools.partial(_dswa_kernel, dilation=dilation),
        grid=(b, s // DSWA_TILE),
        in_specs=[cur(0), cur(1), prev(1), cur(2), prev(2)],
        out_specs=[pl.BlockSpec((1, DSWA_TILE, GROUP_W), lambda bi, n: (bi, n, 0)),
                   pl.BlockSpec((1, DSWA_TILE, HEAD_DIM), lambda bi, n: (bi, n, 0))],
        out_shape=[jax.ShapeDtypeStruct((b, s, GROUP_W), BF16),
                   jax.ShapeDtypeStruct((b, s, HEAD_DIM), F32)],
        scratch_shapes=[pltpu.VMEM((DSWA_HEADS, DSWA_TILE, HEAD_DIM), F32),
                        pltpu.VMEM((DSWA_TILE, HEAD_DIM), F32)],
        compiler_params=_cparams(("parallel", "arbitrary")),
        name=f"dswa_g{group}",
    )(qkv, qkv, qkv, qkv, qkv)


def _linear_scan(a, b, h0):
    ts, d = a.shape
    a = a.reshape(ts // SUBLANES, SUBLANES, d)
    b = b.reshape(ts // SUBLANES, SUBLANES, d)
    row = lax.broadcasted_iota(jnp.int32, a.shape, 1)
    step = 1
    while step < SUBLANES:
        keep = row >= step
        a_s = jnp.where(keep, pltpu.roll(a, step, 1), 1.0)
        b_s = jnp.where(keep, pltpu.roll(b, step, 1), 0.0)
        b = a * b_s + b
        a = a * a_s
        step *= 2
    hs, carry = [], h0
    for g in range(ts // SUBLANES):
        hg = a[g] * carry + b[g]
        hs.append(hg)
        carry = hg[SUBLANES - 1:SUBLANES, :]
    return jnp.concatenate(hs, axis=0), carry


def _rglru_kernel(xb_ref, halo_ref, w_ref, cw_ref, cb_ref, wa_ref, ba_ref, wx_ref, bx_ref,
                  lam_ref, o_ref, h_ref, *, ts):
    si = pl.program_id(1)

    @pl.when(si == 0)
    def _():
        h_ref[...] = jnp.zeros_like(h_ref)

    z = _dot(xb_ref[0], w_ref[...])
    x = z[:, :D_RNN]
    yg = z[:, D_RNN:]
    halo = _dot(halo_ref[0], w_ref[:, :D_RNN])[BF16_ROWS - SUBLANES:, :]
    halo = jnp.where(si > 0, halo, 0.0)
    xfull = jnp.concatenate([halo, x], axis=0)
    xc = cb_ref[...] + cw_ref[CONV_W - 1:CONV_W, :] * x
    for k in range(1, CONV_W):
        xc = xc + cw_ref[CONV_W - 1 - k:CONV_W - k, :] * pltpu.roll(xfull, k, 0)[SUBLANES:]

    xb = xc.astype(BF16)
    nblk = D_RNN // MXU_TILE
    ra = jnp.concatenate(
        [_dot(xb[:, c * MXU_TILE:(c + 1) * MXU_TILE], wa_ref[c]) for c in range(nblk)], axis=1)
    rx = jnp.concatenate(
        [_dot(xb[:, c * MXU_TILE:(c + 1) * MXU_TILE], wx_ref[c]) for c in range(nblk)], axis=1)
    r = jax.nn.sigmoid(ra + ba_ref[...])
    gate_i = jax.nn.sigmoid(rx + bx_ref[...])
    neg_lam = -lam_ref[...]
    softplus = jnp.maximum(neg_lam, 0.0) + jnp.log1p(jnp.exp(-jnp.abs(neg_lam)))
    log_a = (-LRU_C) * r * softplus
    a = jnp.exp(log_a)
    bterm = jnp.sqrt(1.0 - a * a) * gate_i * xc

    h, carry = _linear_scan(a, bterm, h_ref[0:1, :])
    h_ref[0:1, :] = carry
    o_ref[0] = (h * jax.nn.gelu(yg, approximate=True)).astype(o_ref.dtype)


def _rglru(xb, w_rnn, conv_w, conv_b, wa_bd, ba, wx_bd, bx, lam, ts=512):
    b, s, d = xb.shape
    assert s % ts == 0
    full = lambda shape: pl.BlockSpec(shape, lambda bi, si: (0,) * len(shape))
    return pl.pallas_call(
        functools.partial(_rglru_kernel, ts=ts),
        grid=(b, s // ts),
        in_specs=[pl.BlockSpec((1, ts, d), lambda bi, si: (bi, si, 0)),
                  pl.BlockSpec((1, BF16_ROWS, d),
                               lambda bi, si: (bi, jnp.maximum(si * (ts // BF16_ROWS) - 1, 0), 0)),
                  full(w_rnn.shape),
                  full((CONV_W, D_RNN)), full((1, D_RNN)),
                  full(wa_bd.shape), full((1, D_RNN)),
                  full(wx_bd.shape), full((1, D_RNN)), full((1, D_RNN))],
        out_specs=pl.BlockSpec((1, ts, D_RNN), lambda bi, si: (bi, si, 0)),
        out_shape=jax.ShapeDtypeStruct((b, s, D_RNN), BF16),
        scratch_shapes=[pltpu.VMEM((SUBLANES, D_RNN), F32)],
        compiler_params=_cparams(("parallel", "arbitrary")),
        name="rglru",
    )(xb, xb, w_rnn, conv_w, conv_b, wa_bd, ba, wx_bd, bx, lam)


def _memattn_kernel(q_ref, kv_ref, o_ref):
    scale = 1.0 / math.sqrt(HEAD_DIM)
    for h in range(MEM_HEADS):
        hs = slice(h * HEAD_DIM, (h + 1) * HEAD_DIM)
        vs = slice(MEM_WIDTH + h * HEAD_DIM, MEM_WIDTH + (h + 1) * HEAD_DIM)
        s = _dot_nt(q_ref[0, :, hs], kv_ref[0, :, hs]) * scale
        m = jnp.max(s, axis=-1, keepdims=True)
        p = jnp.exp(s - m)
        l = jnp.sum(p, axis=-1, keepdims=True)
        o_ref[0, :, hs] = (_dot(p.astype(BF16), kv_ref[0, :, vs]) / l).astype(o_ref.dtype)


def _memattn(mq, kv, ts=1024):
    b, s, _ = mq.shape
    mem_len = kv.shape[1]
    return pl.pallas_call(
        _memattn_kernel,
        grid=(b, s // ts),
        in_specs=[pl.BlockSpec((1, ts, MEM_WIDTH), lambda bi, si: (bi, si, 0)),
                  pl.BlockSpec((1, mem_len, 2 * MEM_WIDTH), lambda bi, si: (bi, 0, 0))],
        out_specs=pl.BlockSpec((1, ts, MEM_WIDTH), lambda bi, si: (bi, si, 0)),
        out_shape=jax.ShapeDtypeStruct((b, s, MEM_WIDTH), BF16),
        compiler_params=_cparams(("parallel", "arbitrary")),
        name="memattn",
    )(mq, kv)


def _mix_kernel(o0_ref, o1_ref, o2_ref, l0_ref, l1_ref, l2_ref, rec_ref, memo_ref, x_ref,
                wg_ref, bg_ref, wa_ref, wl_ref, wm_ref, wo_ref, g_ref, b_ref, x1_ref, x1b_ref):
    l0, l1, l2 = l0_ref[...], l1_ref[...], l2_ref[...]
    mx = jnp.maximum(jnp.maximum(l0, l1), l2)
    e0, e1, e2 = jnp.exp(l0 - mx), jnp.exp(l1 - mx), jnp.exp(l2 - mx)
    inv = 1.0 / (e0 + e1 + e2)
    w0, w1, w2 = e0 * inv, e1 * inv, e2 * inv
    parts = []
    for h in range(DSWA_HEADS):
        hs = slice(h * HEAD_DIM, (h + 1) * HEAD_DIM)
        parts.append((w0[:, h:h + 1] * o0_ref[:, hs].astype(F32)
                      + w1[:, h:h + 1] * o1_ref[:, hs].astype(F32)
                      + w2[:, h:h + 1] * o2_ref[:, hs].astype(F32)).astype(BF16))
    attn = jnp.concatenate(parts, axis=1)
    d = D_MODEL
    gl = _dot(x_ref[...].astype(BF16), wg_ref[...])
    gate = lambda j: jax.nn.sigmoid(gl[:, j * d:(j + 1) * d] + bg_ref[j:j + 1, :])
    merged = (gate(0) * _dot(attn, wa_ref[...])
              + gate(1) * _dot(rec_ref[...], wl_ref[...])
              + gate(2) * _dot(memo_ref[...], wm_ref[...]))
    mix = _dot(merged.astype(BF16), wo_ref[...])
    x1 = _layernorm(ALPHA * x_ref[...] + mix, g_ref[...], b_ref[...])
    x1_ref[...] = x1
    x1b_ref[...] = x1.astype(BF16)


def _mix(os, lses, rec, memo, x, w_gate, b_gate, wa, wl, wm, wo, g, bta, tt=512):
    t = x.shape[0]
    rows = lambda w: pl.BlockSpec((tt, w), lambda i: (i, 0))
    full = lambda a: pl.BlockSpec(a.shape, lambda i: (0,) * a.ndim, pipeline_mode=pl.Buffered(1))
    return pl.pallas_call(
        _mix_kernel,
        grid=(t // tt,),
        in_specs=[rows(GROUP_W)] * 3 + [rows(HEAD_DIM)] * 3 + [
            rows(D_RNN), rows(MEM_WIDTH), rows(D_MODEL), full(w_gate),
            full(b_gate), full(wa), full(wl), full(wm), full(wo), full(g), full(bta)],
        out_specs=[rows(D_MODEL), rows(D_MODEL)],
        out_shape=[jax.ShapeDtypeStruct((t, D_MODEL), F32), jax.ShapeDtypeStruct((t, D_MODEL), BF16)],
        compiler_params=_cparams(("parallel",)),
        name="mix_ln1",
    )(*os, *lses, rec, memo, x, w_gate, b_gate, wa, wl, wm, wo, g, bta)


def _peer_scores_kernel(x_ref, wq_ref, kbd_ref, o_ref):
    q = _dot(x_ref[...], wq_ref[...]).astype(BF16)
    for h in range(PEER_HEADS):
        hs = slice(h * PEER_KEY_DIM, (h + 1) * PEER_KEY_DIM)
        o_ref[hs, :] = _dot_nt(kbd_ref[h], q[:, hs])


def _peer_scores(x1b, wq, kbd, tt=1024):
    t = x1b.shape[0]
    w = PEER_HEADS * PEER_KEY_DIM
    return pl.pallas_call(
        _peer_scores_kernel,
        grid=(t // tt,),
        in_specs=[pl.BlockSpec((tt, D_MODEL), lambda i: (i, 0)),
                  pl.BlockSpec(wq.shape, lambda i: (0, 0)),
                  pl.BlockSpec(kbd.shape, lambda i: (0, 0, 0))],
        out_specs=pl.BlockSpec((w, tt), lambda i: (0, i)),
        out_shape=jax.ShapeDtypeStruct((w, t), F32),
        compiler_params=_cparams(("parallel",)),
        name="peer_scores",
    )(x1b, wq, kbd)


_PEER_CANDS = tuple((p, q) for p in range(PEER_TOPK) for q in range(PEER_TOPK)
                    if (p + 1) * (q + 1) <= PEER_TOPK)


def _top_positions(s, exact_ties):
    key = lax.broadcasted_iota(jnp.int32, s.shape, 0).astype(F32)
    pos = jnp.full(s.shape, float(PEER_TOPK), F32)
    work = s
    tops = []
    for p in range(PEER_TOPK):
        mx = jnp.max(work, axis=0, keepdims=True)
        if exact_ties:
            first = jnp.min(jnp.where(work == mx, key, float(N_KEYS)), axis=0, keepdims=True)
            hit = key == first
        else:
            hit = work == mx
        pos = jnp.where(hit, float(p), pos)
        work = jnp.where(hit, -jnp.inf, work)
        tops.append(mx)
    return pos, tops


def _store_lane_tiles(ref, h, val):
    for c in range(ref.shape[0]):
        ref[c, h * N_KEYS:(h + 1) * N_KEYS, :] = val[:, c * LANES:(c + 1) * LANES]


def _peer_select_body(sc_ref, ra_ref, ea_ref, cb_ref, eb_ref, exact_ties):
    tt = sc_ref.shape[1]
    kg = N_KEYS // BF16_ROWS
    pos_b, tops_a, tops_b = [], [], []
    marked = jnp.zeros((1, tt), F32)
    for h in range(PEER_HEADS):
        base = h * PEER_KEY_DIM
        pa, ta = _top_positions(sc_ref[base:base + N_KEYS, :], exact_ties)
        pb, tb = _top_positions(sc_ref[base + N_KEYS:base + 2 * N_KEYS, :], exact_ties)
        _store_lane_tiles(ra_ref, h, pa)
        pos_b.append(pb)
        tops_a.append(ta)
        tops_b.append(tb)
        if not exact_ties:
            for pos in (pa, pb):
                n_marked = jnp.sum(jnp.where(pos < float(PEER_TOPK), 1.0, 0.0), axis=0, keepdims=True)
                marked = jnp.maximum(marked, n_marked)
    a_top = [jnp.concatenate([tops_a[h][p] for h in range(PEER_HEADS)], axis=0)
             for p in range(PEER_TOPK)]
    b_top = [jnp.concatenate([tops_b[h][q] for h in range(PEER_HEADS)], axis=0)
             for q in range(PEER_TOPK)]
    sums = [a_top[p] + b_top[q] for p, q in _PEER_CANDS]
    n = len(sums)
    rank = [jnp.full((PEER_HEADS, tt), float(n - 1 - c), F32) for c in range(n)]
    for c in range(n):
        for c2 in range(c):
            c2_wins = jnp.where(sums[c2] >= sums[c], 1.0, 0.0)
            rank[c] = rank[c] + c2_wins
            rank[c2] = rank[c2] - c2_wins
    cnt = [jnp.zeros((PEER_HEADS, tt), F32) for _ in range(PEER_TOPK)]
    zsum = jnp.zeros((PEER_HEADS, tt), F32)
    for c, (p, q) in enumerate(_PEER_CANDS):
        sel = rank[c] < float(PEER_TOPK)
        cnt[q] = cnt[q] + jnp.where(sel, 1.0, 0.0)
        zsum = zsum + jnp.where(sel, jnp.exp(sums[c] - sums[0]), 0.0)
    inv_z = 1.0 / zsum
    for h in range(PEER_HEADS):
        base = h * PEER_KEY_DIM
        rows = slice(h * N_KEYS, (h + 1) * N_KEYS)
        posb = pos_b[h].astype(BF16).reshape(kg, BF16_ROWS, tt)
        cb = jnp.zeros((kg, BF16_ROWS, tt), BF16)
        for q in range(PEER_TOPK):
            cnt_q = jnp.broadcast_to(cnt[q][h:h + 1, :], (BF16_ROWS, tt)).astype(BF16)
            cb = jnp.where(posb == float(q), cnt_q[None], cb)
        cb_ref[h * kg:(h + 1) * kg] = cb
        _store_lane_tiles(ea_ref, h, jnp.exp(sc_ref[base:base + N_KEYS, :] - a_top[0][h:h + 1, :])
                          * inv_z[h:h + 1, :])
        eb = jnp.exp(sc_ref[base + N_KEYS:base + 2 * N_KEYS, :] - b_top[0][h:h + 1, :])
        eb_ref[h * kg:(h + 1) * kg] = eb.astype(BF16).reshape(kg, BF16_ROWS, tt)
    return marked


def _peer_select_kernel(sc_ref, ra_ref, ea_ref, cb_ref, eb_ref):
    marked = _peer_select_body(sc_ref, ra_ref, ea_ref, cb_ref, eb_ref, exact_ties=False)

    @pl.when(jnp.max(marked) > float(PEER_TOPK))
    def _():
        _peer_select_body(sc_ref, ra_ref, ea_ref, cb_ref, eb_ref, exact_ties=True)


def _peer_select(sc, tt=256):
    w, t = sc.shape
    rows = PEER_HEADS * N_KEYS
    out_spec_a = pl.BlockSpec((tt // LANES, rows, LANES), lambda i: (i, 0, 0))
    shape_a = jax.ShapeDtypeStruct((t // LANES, rows, LANES), F32)
    out_spec_b = pl.BlockSpec((rows // BF16_ROWS, BF16_ROWS, tt), lambda i: (0, 0, i))
    shape_b = jax.ShapeDtypeStruct((rows // BF16_ROWS, BF16_ROWS, t), BF16)
    return pl.pallas_call(
        _peer_select_kernel,
        grid=(t // tt,),
        in_specs=[pl.BlockSpec((w, tt), lambda i: (0, i))],
        out_specs=[out_spec_a, out_spec_a, out_spec_b, out_spec_b],
        out_shape=[shape_a, shape_a, shape_b, shape_b],
        compiler_params=_cparams(("parallel",)),
        name="peer_select",
    )(sc)


def _row_on_sublanes(ref, h, i, lane0, lanes):
    tiles = range(lane0 // LANES, (lane0 + lanes) // LANES)
    return jnp.concatenate([ref[c, h, pl.ds(i, BF16_ROWS, stride=0), :] for c in tiles], axis=1)


def _masked_acts(hmat, ra_ref, ea_ref, cb_ref, eb_ref, key0):
    tt = hmat.shape[1]
    kg = N_KEYS // BF16_ROWS
    ys = []
    for k in range(hmat.shape[0] // N_KEYS):
        rows = slice(k * N_KEYS, (k + 1) * N_KEYS)
        ws = []
        for l0 in range(0, tt, PEER_MASK_LANES):
            ls = slice(l0, l0 + PEER_MASK_LANES)
            wg = []
            for g0 in range(0, kg, PEER_MASK_GROUPS):
                w = jnp.zeros((PEER_MASK_GROUPS, BF16_ROWS, PEER_MASK_LANES), BF16)
                for h in range(PEER_HEADS):
                    hs = slice(h * kg + g0, h * kg + g0 + PEER_MASK_GROUPS)
                    ra = _row_on_sublanes(ra_ref, h, key0 + k, l0, PEER_MASK_LANES).astype(BF16)
                    ea = _row_on_sublanes(ea_ref, h, key0 + k, l0, PEER_MASK_LANES).astype(BF16)
                    w = w + eb_ref[hs, :, ls] * jnp.where(cb_ref[hs, :, ls] > ra[None], ea[None],
                                                          jnp.zeros_like(w))
                wg.append(w.reshape(PEER_MASK_GROUPS * BF16_ROWS, PEER_MASK_LANES))
            ws.append(jnp.concatenate(wg, axis=0))
        ys.append(jnp.concatenate(ws, axis=1) * _gelu_erf(hmat[rows, :]).astype(BF16))
    return jnp.concatenate(ys, axis=0)


def _peer_mix_kernel(x1b_ref, x1_ref, u_ref, vt_ref, ra_ref, ea_ref, cb_ref, eb_ref,
                     un_ref, x1bn_ref, g_ref, b_ref, o_ref, acc_ref, hfirst_ref, *, ib):
    i, e = pl.program_id(0), pl.program_id(1)
    assert sum(PEER_CHUNK_KEYS) == ib
    starts = [sum(PEER_CHUNK_KEYS[:c]) for c in range(len(PEER_CHUNK_KEYS) + 1)]
    chunk_rows = lambda c: slice(starts[c] * N_KEYS, starts[c + 1] * N_KEYS)
    n_chunks = len(PEER_CHUNK_KEYS)

    @pl.when((i == 0) & (e == 0))
    def _():
        hfirst_ref[...] = _dot_nt(u_ref[chunk_rows(0), :], x1b_ref[...])

    @pl.when(e == 0)
    def _():
        acc_ref[...] = jnp.zeros_like(acc_ref)

    x1b = x1b_ref[...]
    acc = acc_ref[...]
    h_next = hfirst_ref
    for c in range(n_chunks):
        y = _masked_acts(h_next, ra_ref, ea_ref, cb_ref, eb_ref, starts[c])
        if c + 1 < n_chunks:
            h_next = _dot_nt(u_ref[chunk_rows(c + 1), :], x1b)
        else:
            h_next = _dot_nt(un_ref[...], x1bn_ref[...])
        acc = acc + _dot(vt_ref[:, chunk_rows(c)], y)
    acc_ref[...] = acc
    hfirst_ref[...] = h_next

    @pl.when(e == pl.num_programs(1) - 1)
    def _():
        o_ref[...] = _layernorm(ALPHA * x1_ref[...] + acc_ref[...].T, g_ref[...], b_ref[...])


def _peer_mix(x1b, x1, u, vt, ra, ea, cb, eb, g, bta, tt=512, te=sum(PEER_CHUNK_KEYS) * N_KEYS):
    t = x1.shape[0]
    ib = te // N_KEYS
    ne = N_EXPERTS // te
    rows = PEER_HEADS * N_KEYS
    first = PEER_CHUNK_KEYS[0] * N_KEYS
    assert te % first == 0
    n_steps = (t // tt) * ne
    nxt = lambda i, e: jnp.minimum(i * ne + e + 1, n_steps - 1)
    return pl.pallas_call(
        functools.partial(_peer_mix_kernel, ib=ib),
        grid=(t // tt, ne),
        in_specs=[pl.BlockSpec((tt, D_MODEL), lambda i, e: (i, 0)),
                  pl.BlockSpec((tt, D_MODEL), lambda i, e: (i, 0)),
                  pl.BlockSpec((te, D_MODEL), lambda i, e: (e, 0)),
                  pl.BlockSpec((D_MODEL, te), lambda i, e: (0, e)),
                  pl.BlockSpec((tt // LANES, PEER_HEADS, ib, LANES), lambda i, e: (i, 0, e, 0)),
                  pl.BlockSpec((tt // LANES, PEER_HEADS, ib, LANES), lambda i, e: (i, 0, e, 0)),
                  pl.BlockSpec((rows // BF16_ROWS, BF16_ROWS, tt), lambda i, e: (0, 0, i)),
                  pl.BlockSpec((rows // BF16_ROWS, BF16_ROWS, tt), lambda i, e: (0, 0, i)),
                  pl.BlockSpec((first, D_MODEL), lambda i, e: ((nxt(i, e) % ne) * (te // first), 0)),
                  pl.BlockSpec((tt, D_MODEL), lambda i, e: (nxt(i, e) // ne, 0)),
                  pl.BlockSpec((1, D_MODEL), lambda i, e: (0, 0)),
                  pl.BlockSpec((1, D_MODEL), lambda i, e: (0, 0))],
        out_specs=pl.BlockSpec((tt, D_MODEL), lambda i, e: (i, 0)),
        out_shape=jax.ShapeDtypeStruct((t, D_MODEL), F32),
        scratch_shapes=[pltpu.VMEM((D_MODEL, tt), F32), pltpu.VMEM((first, tt), F32)],
        compiler_params=_cparams(("arbitrary", "arbitrary")),
        name="peer_mix_ln2",
    )(x1b, x1, u, vt, ra, ea, cb, eb, u, x1b, g, bta)


def _block_diag(w, per_tile):
    n, k, _ = w.shape
    w = w.reshape(n // per_tile, per_tile, k, k)
    eye = jnp.eye(per_tile, dtype=w.dtype)
    out = jnp.einsum('tpij,pq->tpiqj', w, eye)
    return out.reshape(n // per_tile, per_tile * k, per_tile * k)


def _layer(x, mem, w_in, b_gate, conv_w, conv_b, lru_wa, lru_ba, lru_wx, lru_bx, lru_lambda,
           w_mem_kv, w_br_attn, w_br_lru, w_br_mem, w_out, ln1_g, ln1_b,
           peer_wq, peer_keys, peer_u, peer_v, ln2_g, ln2_b):
    b, s, d = x.shape
    t = b * s
    xf = x.reshape(t, d)
    xb = xf.astype(BF16)
    w_in_b = w_in.astype(BF16)
    row = lambda a: a.reshape(1, -1).astype(F32)

    off_rnn = 3 * DSWA_WIDTH
    off_mq = off_rnn + 2 * D_RNN
    off_gl = off_mq + MEM_WIDTH
    mq = _matmul(xb, w_in_b, off_mq, MEM_WIDTH, BF16, tn=512).reshape(b, s, MEM_WIDTH)

    os, lses = [], []
    for gi, dil in enumerate(DSWA_DILATIONS):
        xp = xb if dil == 1 else _permute_rows(x, dil).reshape(t, d)
        w_g = jnp.concatenate([w_in_b[:, part * DSWA_WIDTH + gi * GROUP_W:
                                      part * DSWA_WIDTH + (gi + 1) * GROUP_W] for part in range(3)], axis=1)
        qkv = _matmul(xp, w_g, 0, 3 * GROUP_W, BF16, tn=3 * GROUP_W).reshape(b, dil, s // dil, 3 * GROUP_W)
        o_g, lse_g = _dswa_group(qkv, gi)
        os.append(o_g.reshape(t, GROUP_W))
        lses.append(lse_g.reshape(t, HEAD_DIM))

    per_tile = MXU_TILE // LRU_BW
    wa_bd = _block_diag(lru_wa, per_tile).astype(BF16)
    wx_bd = _block_diag(lru_wx, per_tile).astype(BF16)
    rec = _rglru(xb.reshape(b, s, d), w_in_b[:, off_rnn:off_mq], conv_w.astype(F32), row(conv_b),
                 wa_bd, row(lru_ba), wx_bd, row(lru_bx), row(lru_lambda)).reshape(t, D_RNN)

    mem_len = mem.shape[1]
    kv = _matmul(mem.reshape(b * mem_len, d).astype(BF16), w_mem_kv.astype(BF16), 0,
                 2 * MEM_WIDTH, BF16, tn=1024).reshape(b, mem_len, 2 * MEM_WIDTH)
    memo = _memattn(mq, kv).reshape(t, MEM_WIDTH)

    x1, x1b = _mix(os, lses, rec, memo, xf, w_in_b[:, off_gl:], b_gate.astype(F32), w_br_attn.astype(BF16),
                   w_br_lru.astype(BF16), w_br_mem.astype(BF16), w_out.astype(BF16),
                   row(ln1_g), row(ln1_b))

    kbd = _block_diag(peer_keys.reshape(PEER_HEADS * 2, N_KEYS, PEER_KEY_DIM // 2), 2).astype(BF16)
    scores = _peer_scores(x1b, peer_wq.astype(BF16), kbd)

    ra, ea, cb, eb = _peer_select(scores)
    ra = ra.reshape(t // LANES, PEER_HEADS, N_KEYS, LANES)
    ea = ea.reshape(t // LANES, PEER_HEADS, N_KEYS, LANES)

    out = _peer_mix(x1b, x1, peer_u.astype(BF16), peer_v.T.astype(BF16), ra, ea, cb, eb,
                    row(ln2_g), row(ln2_b))
    return out.reshape(b, s, d)


def kernel(x, mem, w_in, b_gate, conv_w, conv_b, lru_wa, lru_ba, lru_wx, lru_bx, lru_lambda, w_mem_kv, w_br_attn, w_br_lru, w_br_mem, w_out, ln1_g, ln1_b, peer_wq, peer_keys, peer_u, peer_v, ln2_g, ln2_b):
    h = x.astype(F32)
    depth = w_in.shape[0]
    for l in range(depth):
        h = _layer(h, mem, w_in[l], b_gate[l], conv_w[l], conv_b[l], lru_wa[l], lru_ba[l],
                   lru_wx[l], lru_bx[l], lru_lambda[l], w_mem_kv[l], w_br_attn[l], w_br_lru[l],
                   w_br_mem[l], w_out[l], ln1_g[l], ln1_b[l], peer_wq[l], peer_keys[l],
                   peer_u[l], peer_v[l], ln2_g[l], ln2_b[l])
    return h.astype(x.dtype)
```

```python
import functools
import math

import jax
import jax.numpy as jnp
from jax import lax
from jax.experimental import pallas as pl
from jax.experimental.pallas import tpu as pltpu

F32 = jnp.float32
BF16 = jnp.bfloat16

D_MODEL = 1024
N_GROUPS = 3
DSWA_DILATIONS = (1, 4, 16)
DSWA_HEADS = 4
HEAD_DIM = 128
DSWA_BLK = 128
DSWA_BATCH_BLOCKS = 4
DSWA_TILE = 2048
GROUP_W = DSWA_HEADS * HEAD_DIM
DSWA_WIDTH = N_GROUPS * GROUP_W
D_RNN = 1024
LRU_BLOCKS = 16
LRU_BW = D_RNN // LRU_BLOCKS
CONV_W = 4
LRU_C = 8.0
MEM_HEADS = 4
MEM_WIDTH = MEM_HEADS * HEAD_DIM
PEER_HEADS = 8
PEER_KEY_DIM = 256
N_KEYS = 128
N_EXPERTS = N_KEYS * N_KEYS
PEER_TOPK = 16
PEER_CHUNK_KEYS = (8,) * 2
BF16_ROWS = 16
PEER_MASK_LANES = 256
ALPHA = 2.0 ** 0.25
LN_EPS = 1e-5
NEG_INF = -1e30

LANES = 128
SUBLANES = 8
MXU_TILE = 256
VMEM_LIMIT = 56 * 1024 * 1024


def _cparams(sem):
    return pltpu.CompilerParams(dimension_semantics=sem, vmem_limit_bytes=VMEM_LIMIT)


def _dot(a, b):
    return jnp.dot(a, b, preferred_element_type=F32)


def _dot_nt(a, b):
    return lax.dot_general(a, b, (((1,), (1,)), ((), ())), preferred_element_type=F32)


def _gelu_erf(x):
    return 0.5 * x * (1.0 + lax.erf(x * (1.0 / math.sqrt(2.0))))


def _layernorm(h, g, b):
    mu = jnp.mean(h, axis=-1, keepdims=True)
    c = h - mu
    var = jnp.mean(c * c, axis=-1, keepdims=True)
    return c * lax.rsqrt(var + LN_EPS) * g + b


def _matmul_kernel(x_ref, w_ref, o_ref):
    o_ref[...] = _dot(x_ref[...], w_ref[...]).astype(o_ref.dtype)


def _matmul(x, w, col_off, n_cols, out_dtype, tm=1024, tn=1024):
    m, k = x.shape
    tn = min(tn, n_cols)
    tm = min(tm, m)
    assert m % tm == 0 and n_cols % tn == 0 and col_off % tn == 0
    off = col_off // tn
    return pl.pallas_call(
        _matmul_kernel,
        grid=(m // tm, n_cols // tn),
        in_specs=[pl.BlockSpec((tm, k), lambda i, j: (i, 0)),
                  pl.BlockSpec((k, tn), lambda i, j: (0, off + j))],
        out_specs=pl.BlockSpec((tm, tn), lambda i, j: (i, j)),
        out_shape=jax.ShapeDtypeStruct((m, n_cols), out_dtype),
        compiler_params=_cparams(("parallel", "arbitrary")),
        name="matmul",
    )(x, w)


def _permute_kernel(x_ref, *o_refs, dilations):
    for o_ref, dilation in zip(o_refs, dilations):
        rows = x_ref.shape[1] // dilation
        for bi in range(x_ref.shape[0]):
            for r in range(dilation):
                o_ref[bi, r] = x_ref[bi, pl.ds(r, rows, stride=dilation), :].astype(o_ref.dtype)


def _permute_rows(x, dilations):
    b, s, d = x.shape
    return pl.pallas_call(
        functools.partial(_permute_kernel, dilations=dilations),
        grid=(s // DSWA_TILE, d // LANES),
        in_specs=[pl.BlockSpec((b, DSWA_TILE, LANES), lambda n, c: (0, n, c))],
        out_specs=[pl.BlockSpec((b, dil, DSWA_TILE // dil, LANES), lambda n, c: (0, 0, n, c))
                   for dil in dilations],
        out_shape=[jax.ShapeDtypeStruct((b, dil, s // dil, d), BF16) for dil in dilations],
        compiler_params=_cparams(("parallel", "parallel")),
        name="permute",
    )(x)


def _dswa_kernel(q_ref, kc_ref, kp_ref, vc_ref, vp_ref, o_ref, lse_ref, o_scr, lse_scr, *, dilation):
    nq = q_ref.shape[2] // DSWA_BLK
    row = lax.broadcasted_iota(jnp.int32, (DSWA_BLK, DSWA_BLK), 0)
    col = lax.broadcasted_iota(jnp.int32, (DSWA_BLK, DSWA_BLK), 1)
    mask_cur = col <= row
    mask_band = col >= row
    mask_first = col >= row + jnp.where(pl.program_id(1) == 0, DSWA_BLK, 0)
    scale = 1.0 / math.sqrt(HEAD_DIM)
    bdot_nt = lambda a, b: jnp.einsum('hqd,hkd->hqk', a, b, preferred_element_type=F32)
    bdot = lambda a, b: jnp.einsum('hqk,hkd->hqd', a, b, preferred_element_type=F32)

    def heads(blocks):
        return jnp.stack([a[:, h * HEAD_DIM:(h + 1) * HEAD_DIM]
                          for a in blocks for h in range(DSWA_HEADS)], axis=0)

    def attend(blocks):
        qs, kps, kcs, vps, vcs = [], [], [], [], []
        for r, j in blocks:
            rows = slice(j * DSWA_BLK, (j + 1) * DSWA_BLK)
            prows = slice((j - 1) * DSWA_BLK, j * DSWA_BLK)
            qs.append(q_ref[0, r, rows, :])
            kcs.append(kc_ref[0, r, rows, :])
            vcs.append(vc_ref[0, r, rows, :])
            kps.append(kp_ref[0, r] if j == 0 else kc_ref[0, r, prows, :])
            vps.append(vp_ref[0, r] if j == 0 else vc_ref[0, r, prows, :])
        mask_prev = mask_first if blocks[0][1] == 0 else mask_band
        q4 = heads(qs)
        s_p = jnp.where(mask_prev[None], bdot_nt(q4, heads(kps)) * scale, NEG_INF)
        s_c = jnp.where(mask_cur[None], bdot_nt(q4, heads(kcs)) * scale, NEG_INF)
        m = jnp.maximum(jnp.max(s_p, axis=-1, keepdims=True), jnp.max(s_c, axis=-1, keepdims=True))
        p_p = jnp.exp(s_p - m)
        p_c = jnp.exp(s_c - m)
        l = jnp.sum(p_p, axis=-1, keepdims=True) + jnp.sum(p_c, axis=-1, keepdims=True)
        o = (bdot(p_p.astype(BF16), heads(vps)) + bdot(p_c.astype(BF16), heads(vcs))) / l
        lse = m + jnp.log(l)
        for bi, (r, j) in enumerate(blocks):
            dst = pl.ds(j * DSWA_BLK * dilation + r, DSWA_BLK, stride=dilation)
            lse_tile = jnp.zeros((DSWA_BLK, HEAD_DIM), F32)
            for h in range(DSWA_HEADS):
                o_scr[h, dst, :] = o[bi * DSWA_HEADS + h]
                lse_tile = jnp.where(col == h, lse[bi * DSWA_HEADS + h], lse_tile)
            lse_scr[dst, :] = lse_tile

    for first in (True, False):
        group = [(r, j) for j in range(nq) for r in range(dilation) if (j == 0) == first]
        for i in range(0, len(group), DSWA_BATCH_BLOCKS):
            attend(group[i:i + DSWA_BATCH_BLOCKS])
    for h in range(DSWA_HEADS):
        o_ref[0, :, h * HEAD_DIM:(h + 1) * HEAD_DIM] = o_scr[h].astype(o_ref.dtype)
    lse_ref[0] = lse_scr[...]


def _dswa_group(qkv, group):
    b, dilation, m_len, _ = qkv.shape
    s = dilation * m_len
    assert s % DSWA_TILE == 0 and DSWA_TILE % (dilation * DSWA_BLK) == 0
    rows = DSWA_TILE // dilation
    nq = rows // DSWA_BLK

    def cur(c):
        return pl.BlockSpec((1, dilation, rows, GROUP_W), lambda bi, n: (bi, 0, n, c))

    def prev(c):
        return pl.BlockSpec((1, dilation, DSWA_BLK, GROUP_W),
                            lambda bi, n: (bi, 0, jnp.maximum(n * nq - 1, 0), c))

    return pl.pallas_call(
        functools.partial(_dswa_kernel, dilation=dilation),
        grid=(b, s // DSWA_TILE),
        in_specs=[cur(0), cur(1), prev(1), cur(2), prev(2)],
        out_specs=[pl.BlockSpec((1, DSWA_TILE, GROUP_W), lambda bi, n: (bi, n, 0)),
                   pl.BlockSpec((1, DSWA_TILE, HEAD_DIM), lambda bi, n: (bi, n, 0))],
        out_shape=[jax.ShapeDtypeStruct((b, s, GROUP_W), BF16),
                   jax.ShapeDtypeStruct((b, s, HEAD_DIM), F32)],
        scratch_shapes=[pltpu.VMEM((DSWA_HEADS, DSWA_TILE, HEAD_DIM), F32),
                        pltpu.VMEM((DSWA_TILE, HEAD_DIM), F32)],
        compiler_params=_cparams(("parallel", "arbitrary")),
        name=f"dswa_g{group}",
    )(qkv, qkv, qkv, qkv, qkv)


def _linear_scan(a, b, h0):
    ts, d = a.shape
    a = a.reshape(ts // SUBLANES, SUBLANES, d)
    b = b.reshape(ts // SUBLANES, SUBLANES, d)
    row = lax.broadcasted_iota(jnp.int32, a.shape, 1)
    step = 1
    while step < SUBLANES:
        keep = row >= step
        a_s = jnp.where(keep, pltpu.roll(a, step, 1), 1.0)
        b_s = jnp.where(keep, pltpu.roll(b, step, 1), 0.0)
        b = a * b_s + b
        a = a * a_s
        step *= 2
    hs, carry = [], h0
    for g in range(ts // SUBLANES):
        hg = a[g] * carry + b[g]
        hs.append(hg)
        carry = hg[SUBLANES - 1:SUBLANES, :]
    return jnp.concatenate(hs, axis=0), carry


def _rglru_kernel(xb_ref, halo_ref, w_ref, cw_ref, cb_ref, wa_ref, ba_ref, wx_ref, bx_ref,
                  lam_ref, o_ref, h_ref, *, ts):
    si = pl.program_id(1)

    @pl.when(si == 0)
    def _():
        h_ref[...] = jnp.zeros_like(h_ref)

    z = _dot(xb_ref[0], w_ref[...])
    x = z[:, :D_RNN]
    yg = z[:, D_RNN:]
    halo = _dot(halo_ref[0], w_ref[:, :D_RNN])[BF16_ROWS - SUBLANES:, :]
    halo = jnp.where(si > 0, halo, 0.0)
    xfull = jnp.concatenate([halo, x], axis=0)
    xc = cb_ref[...] + cw_ref[CONV_W - 1:CONV_W, :] * x
    for k in range(1, CONV_W):
        xc = xc + cw_ref[CONV_W - 1 - k:CONV_W - k, :] * pltpu.roll(xfull, k, 0)[SUBLANES:]

    xb = xc.astype(BF16)
    nblk = D_RNN // MXU_TILE
    ra = jnp.concatenate(
        [_dot(xb[:, c * MXU_TILE:(c + 1) * MXU_TILE], wa_ref[c]) for c in range(nblk)], axis=1)
    rx = jnp.concatenate(
        [_dot(xb[:, c * MXU_TILE:(c + 1) * MXU_TILE], wx_ref[c]) for c in range(nblk)], axis=1)
    r = jax.nn.sigmoid(ra + ba_ref[...])
    gate_i = jax.nn.sigmoid(rx + bx_ref[...])
    neg_lam = -lam_ref[...]
    softplus = jnp.maximum(neg_lam, 0.0) + jnp.log1p(jnp.exp(-jnp.abs(neg_lam)))
    log_a = (-LRU_C) * r * softplus
    a = jnp.exp(log_a)
    bterm = jnp.sqrt(1.0 - a * a) * gate_i * xc

    h, carry = _linear_scan(a, bterm, h_ref[0:1, :])
    h_ref[0:1, :] = carry
    o_ref[0] = (h * jax.nn.gelu(yg, approximate=True)).astype(o_ref.dtype)


def _rglru(xb, w_rnn, conv_w, conv_b, wa_bd, ba, wx_bd, bx, lam, ts=1024):
    b, s, d = xb.shape
    assert s % ts == 0
    full = lambda shape: pl.BlockSpec(shape, lambda bi, si: (0,) * len(shape))
    return pl.pallas_call(
        functools.partial(_rglru_kernel, ts=ts),
        grid=(b, s // ts),
        in_specs=[pl.BlockSpec((1, ts, d), lambda bi, si: (bi, si, 0)),
                  pl.BlockSpec((1, BF16_ROWS, d),
                               lambda bi, si: (bi, jnp.maximum(si * (ts // BF16_ROWS) - 1, 0), 0)),
                  full(w_rnn.shape),
                  full((CONV_W, D_RNN)), full((1, D_RNN)),
                  full(wa_bd.shape), full((1, D_RNN)),
                  full(wx_bd.shape), full((1, D_RNN)), full((1, D_RNN))],
        out_specs=pl.BlockSpec((1, ts, D_RNN), lambda bi, si: (bi, si, 0)),
        out_shape=jax.ShapeDtypeStruct((b, s, D_RNN), BF16),
        scratch_shapes=[pltpu.VMEM((SUBLANES, D_RNN), F32)],
        compiler_params=_cparams(("parallel", "arbitrary")),
        name="rglru",
    )(xb, xb, w_rnn, conv_w, conv_b, wa_bd, ba, wx_bd, bx, lam)


def _memattn_kernel(q_ref, kv_ref, o_ref):
    scale = 1.0 / math.sqrt(HEAD_DIM)
    for h in range(MEM_HEADS):
        hs = slice(h * HEAD_DIM, (h + 1) * HEAD_DIM)
        vs = slice(MEM_WIDTH + h * HEAD_DIM, MEM_WIDTH + (h + 1) * HEAD_DIM)
        s = _dot_nt(q_ref[0, :, hs], kv_ref[0, :, hs]) * scale
        m = jnp.max(s, axis=-1, keepdims=True)
        p = jnp.exp(s - m)
        l = jnp.sum(p, axis=-1, keepdims=True)
        o_ref[0, :, hs] = (_dot(p.astype(BF16), kv_ref[0, :, vs]) / l).astype(o_ref.dtype)


def _memattn(mq, kv, ts=1024):
    b, s, _ = mq.shape
    mem_len = kv.shape[1]
    return pl.pallas_call(
        _memattn_kernel,
        grid=(b, s // ts),
        in_specs=[pl.BlockSpec((1, ts, MEM_WIDTH), lambda bi, si: (bi, si, 0)),
                  pl.BlockSpec((1, mem_len, 2 * MEM_WIDTH), lambda bi, si: (bi, 0, 0))],
        out_specs=pl.BlockSpec((1, ts, MEM_WIDTH), lambda bi, si: (bi, si, 0)),
        out_shape=jax.ShapeDtypeStruct((b, s, MEM_WIDTH), BF16),
        compiler_params=_cparams(("parallel", "arbitrary")),
        name="memattn",
    )(mq, kv)


def _mix_kernel(o0_ref, o1_ref, o2_ref, l0_ref, l1_ref, l2_ref, rec_ref, memo_ref, x_ref,
                wg_ref, bg_ref, wa_ref, wl_ref, wm_ref, wo_ref, g_ref, b_ref, x1_ref, x1b_ref):
    l0, l1, l2 = l0_ref[...], l1_ref[...], l2_ref[...]
    mx = jnp.maximum(jnp.maximum(l0, l1), l2)
    e0, e1, e2 = jnp.exp(l0 - mx), jnp.exp(l1 - mx), jnp.exp(l2 - mx)
    inv = 1.0 / (e0 + e1 + e2)
    w0, w1, w2 = e0 * inv, e1 * inv, e2 * inv
    parts = []
    for h in range(DSWA_HEADS):
        hs = slice(h * HEAD_DIM, (h + 1) * HEAD_DIM)
        parts.append((w0[:, h:h + 1] * o0_ref[:, hs].astype(F32)
                      + w1[:, h:h + 1] * o1_ref[:, hs].astype(F32)
                      + w2[:, h:h + 1] * o2_ref[:, hs].astype(F32)).astype(BF16))
    attn = jnp.concatenate(parts, axis=1)
    d = D_MODEL
    gl = _dot(x_ref[...].astype(BF16), wg_ref[...])
    gate = lambda j: jax.nn.sigmoid(gl[:, j * d:(j + 1) * d] + bg_ref[j:j + 1, :])
    merged = (gate(0) * _dot(attn, wa_ref[...])
              + gate(1) * _dot(rec_ref[...], wl_ref[...])
              + gate(2) * _dot(memo_ref[...], wm_ref[...]))
    mix = _dot(merged.astype(BF16), wo_ref[...])
    x1 = _layernorm(ALPHA * x_ref[...] + mix, g_ref[...], b_ref[...])
    x1_ref[...] = x1
    x1b_ref[...] = x1.astype(BF16)


def _mix(os, lses, rec, memo, x, w_gate, b_gate, wa, wl, wm, wo, g, bta, tt=512):
    t = x.shape[0]
    rows = lambda w: pl.BlockSpec((tt, w), lambda i: (i, 0))
    full = lambda a: pl.BlockSpec(a.shape, lambda i: (0,) * a.ndim, pipeline_mode=pl.Buffered(1))
    return pl.pallas_call(
        _mix_kernel,
        grid=(t // tt,),
        in_specs=[rows(GROUP_W)] * 3 + [rows(HEAD_DIM)] * 3 + [
            rows(D_RNN), rows(MEM_WIDTH), rows(D_MODEL), full(w_gate),
            full(b_gate), full(wa), full(wl), full(wm), full(wo), full(g), full(bta)],
        out_specs=[rows(D_MODEL), rows(D_MODEL)],
        out_shape=[jax.ShapeDtypeStruct((t, D_MODEL), F32), jax.ShapeDtypeStruct((t, D_MODEL), BF16)],
        compiler_params=_cparams(("parallel",)),
        name="mix_ln1",
    )(*os, *lses, rec, memo, x, w_gate, b_gate, wa, wl, wm, wo, g, bta)


def _peer_scores_kernel(x_ref, wq_ref, kbd_ref, o_ref):
    q = _dot(x_ref[...], wq_ref[...]).astype(BF16)
    for h in range(PEER_HEADS):
        hs = slice(h * PEER_KEY_DIM, (h + 1) * PEER_KEY_DIM)
        o_ref[hs, :] = _dot_nt(kbd_ref[h], q[:, hs])


def _peer_scores(x1b, wq, kbd, tt=1024):
    t = x1b.shape[0]
    w = PEER_HEADS * PEER_KEY_DIM
    return pl.pallas_call(
        _peer_scores_kernel,
        grid=(t // tt,),
        in_specs=[pl.BlockSpec((tt, D_MODEL), lambda i: (i, 0)),
                  pl.BlockSpec(wq.shape, lambda i: (0, 0)),
                  pl.BlockSpec(kbd.shape, lambda i: (0, 0, 0))],
        out_specs=pl.BlockSpec((w, tt), lambda i: (0, i)),
        out_shape=jax.ShapeDtypeStruct((w, t), F32),
        compiler_params=_cparams(("parallel",)),
        name="peer_scores",
    )(x1b, wq, kbd)


_PEER_CANDS = tuple((p, q) for p in range(PEER_TOPK) for q in range(PEER_TOPK)
                    if (p + 1) * (q + 1) <= PEER_TOPK)


def _top_positions(s, exact_ties):
    key = lax.broadcasted_iota(jnp.int32, s.shape, 0).astype(F32)
    pos = jnp.full(s.shape, float(PEER_TOPK), F32)
    work = s
    tops = []
    for p in range(PEER_TOPK):
        mx = jnp.max(work, axis=0, keepdims=True)
        if exact_ties:
            first = jnp.min(jnp.where(work == mx, key, float(N_KEYS)), axis=0, keepdims=True)
            hit = key == first
        else:
            hit = work == mx
        pos = jnp.where(hit, float(p), pos)
        work = jnp.where(hit, -jnp.inf, work)
        tops.append(mx)
    return pos, tops


def _store_lane_tiles(ref, h, val):
    for c in range(ref.shape[0]):
        ref[c, h * N_KEYS:(h + 1) * N_KEYS, :] = val[:, c * LANES:(c + 1) * LANES]


def _peer_select_body(sc_ref, ra_ref, ea_ref, cb_ref, eb_ref, exact_ties):
    tt = sc_ref.shape[1]
    kg = N_KEYS // BF16_ROWS
    pos_b, tops_a, tops_b = [], [], []
    marked = jnp.zeros((1, tt), F32)
    for h in range(PEER_HEADS):
        base = h * PEER_KEY_DIM
        pa, ta = _top_positions(sc_ref[base:base + N_KEYS, :], exact_ties)
        pb, tb = _top_positions(sc_ref[base + N_KEYS:base + 2 * N_KEYS, :], exact_ties)
        _store_lane_tiles(ra_ref, h, pa)
        pos_b.append(pb)
        tops_a.append(ta)
        tops_b.append(tb)
        if not exact_ties:
            for pos in (pa, pb):
                n_marked = jnp.sum(jnp.where(pos < float(PEER_TOPK), 1.0, 0.0), axis=0, keepdims=True)
                marked = jnp.maximum(marked, n_marked)
    a_top = [jnp.concatenate([tops_a[h][p] for h in range(PEER_HEADS)], axis=0)
             for p in range(PEER_TOPK)]
    b_top = [jnp.concatenate([tops_b[h][q] for h in range(PEER_HEADS)], axis=0)
             for q in range(PEER_TOPK)]
    sums = [a_top[p] + b_top[q] for p, q in _PEER_CANDS]
    n = len(sums)
    rank = [jnp.full((PEER_HEADS, tt), float(n - 1 - c), F32) for c in range(n)]
    for c in range(n):
        for c2 in range(c):
            c2_wins = jnp.where(sums[c2] >= sums[c], 1.0, 0.0)
            rank[c] = rank[c] + c2_wins
            rank[c2] = rank[c2] - c2_wins
    cnt = [jnp.zeros((PEER_HEADS, tt), F32) for _ in range(PEER_TOPK)]
    zsum = jnp.zeros((PEER_HEADS, tt), F32)
    for c, (p, q) in enumerate(_PEER_CANDS):
        sel = rank[c] < float(PEER_TOPK)
        cnt[q] = cnt[q] + jnp.where(sel, 1.0, 0.0)
        zsum = zsum + jnp.where(sel, jnp.exp(sums[c] - sums[0]), 0.0)
    inv_z = 1.0 / zsum
    for h in range(PEER_HEADS):
        base = h * PEER_KEY_DIM
        rows = slice(h * N_KEYS, (h + 1) * N_KEYS)
        posb = pos_b[h].astype(BF16).reshape(kg, BF16_ROWS, tt)
        cb = jnp.zeros((kg, BF16_ROWS, tt), BF16)
        for q in range(PEER_TOPK):
            cnt_q = jnp.broadcast_to(cnt[q][h:h + 1, :], (BF16_ROWS, tt)).astype(BF16)
            cb = jnp.where(posb == float(q), cnt_q[None], cb)
        cb_ref[h * kg:(h + 1) * kg] = cb
        _store_lane_tiles(ea_ref, h, jnp.exp(sc_ref[base:base + N_KEYS, :] - a_top[0][h:h + 1, :])
                          * inv_z[h:h + 1, :])
        eb = jnp.exp(sc_ref[base + N_KEYS:base + 2 * N_KEYS, :] - b_top[0][h:h + 1, :])
        eb_ref[h * kg:(h + 1) * kg] = eb.astype(BF16).reshape(kg, BF16_ROWS, tt)
    return marked


def _peer_select_kernel(sc_ref, ra_ref, ea_ref, cb_ref, eb_ref):
    marked = _peer_select_body(sc_ref, ra_ref, ea_ref, cb_ref, eb_ref, exact_ties=False)

    @pl.when(jnp.max(marked) > float(PEER_TOPK))
    def _():
        _peer_select_body(sc_ref, ra_ref, ea_ref, cb_ref, eb_ref, exact_ties=True)


def _peer_select(sc, tt=256):
    w, t = sc.shape
    rows = PEER_HEADS * N_KEYS
    out_spec_a = pl.BlockSpec((tt // LANES, rows, LANES), lambda i: (i, 0, 0))
    shape_a = jax.ShapeDtypeStruct((t // LANES, rows, LANES), F32)
    out_spec_b = pl.BlockSpec((rows // BF16_ROWS, BF16_ROWS, tt), lambda i: (0, 0, i))
    shape_b = jax.ShapeDtypeStruct((rows // BF16_ROWS, BF16_ROWS, t), BF16)
    return pl.pallas_call(
        _peer_select_kernel,
        grid=(t // tt,),
        in_specs=[pl.BlockSpec((w, tt), lambda i: (0, i))],
        out_specs=[out_spec_a, out_spec_a, out_spec_b, out_spec_b],
        out_shape=[shape_a, shape_a, shape_b, shape_b],
        compiler_params=_cparams(("parallel",)),
        name="peer_select",
    )(sc)


def _row_on_sublanes(ref, h, i, lane0, lanes):
    tiles = range(lane0 // LANES, (lane0 + lanes) // LANES)
    return jnp.concatenate([ref[c, h, pl.ds(i, BF16_ROWS, stride=0), :] for c in tiles], axis=1)


def _masked_acts(hmat, ra_ref, ea_ref, cb_ref, eb_ref, key0):
    tt = hmat.shape[1]
    kg = N_KEYS // BF16_ROWS
    ys = []
    for k in range(hmat.shape[0] // N_KEYS):
        rows = slice(k * N_KEYS, (k + 1) * N_KEYS)
        ws = []
        for l0 in range(0, tt, PEER_MASK_LANES):
            ls = slice(l0, l0 + PEER_MASK_LANES)
            w = jnp.zeros((kg, BF16_ROWS, PEER_MASK_LANES), BF16)
            for h in range(PEER_HEADS):
                hs = slice(h * kg, (h + 1) * kg)
                ra = _row_on_sublanes(ra_ref, h, key0 + k, l0, PEER_MASK_LANES).astype(BF16)
                ea = _row_on_sublanes(ea_ref, h, key0 + k, l0, PEER_MASK_LANES).astype(BF16)
                w = w + eb_ref[hs, :, ls] * jnp.where(cb_ref[hs, :, ls] > ra[None], ea[None],
                                                      jnp.zeros_like(w))
            ws.append(w.reshape(N_KEYS, PEER_MASK_LANES))
        ys.append(jnp.concatenate(ws, axis=1) * _gelu_erf(hmat[rows, :]).astype(BF16))
    return jnp.concatenate(ys, axis=0)


def _peer_mix_kernel(x1b_ref, x1_ref, u_ref, vt_ref, ra_ref, ea_ref, cb_ref, eb_ref,
                     g_ref, b_ref, o_ref, acc_ref, *, ib):
    e = pl.program_id(1)

    @pl.when(e == 0)
    def _():
        acc_ref[...] = jnp.zeros_like(acc_ref)

    x1b = x1b_ref[...]
    acc = acc_ref[...]
    assert sum(PEER_CHUNK_KEYS) == ib
    starts = [sum(PEER_CHUNK_KEYS[:c]) for c in range(len(PEER_CHUNK_KEYS) + 1)]
    chunk_rows = lambda c: slice(starts[c] * N_KEYS, starts[c + 1] * N_KEYS)
    n_chunks = len(PEER_CHUNK_KEYS)
    h_next = _dot_nt(u_ref[chunk_rows(0), :], x1b)
    for c in range(n_chunks):
        hc = h_next
        if c + 1 < n_chunks:
            h_next = _dot_nt(u_ref[chunk_rows(c + 1), :], x1b)
        y = _masked_acts(hc, ra_ref, ea_ref, cb_ref, eb_ref, starts[c])
        acc = acc + _dot(vt_ref[:, chunk_rows(c)], y)
    acc_ref[...] = acc

    @pl.when(e == pl.num_programs(1) - 1)
    def _():
        o_ref[...] = _layernorm(ALPHA * x1_ref[...] + acc_ref[...].T, g_ref[...], b_ref[...])


def _peer_mix(x1b, x1, u, vt, ra, ea, cb, eb, g, bta, tt=512, te=sum(PEER_CHUNK_KEYS) * N_KEYS):
    t = x1.shape[0]
    ib = te // N_KEYS
    ne = N_EXPERTS // te
    rows = PEER_HEADS * N_KEYS
    return pl.pallas_call(
        functools.partial(_peer_mix_kernel, ib=ib),
        grid=(t // tt, ne),
        in_specs=[pl.BlockSpec((tt, D_MODEL), lambda i, e: (i, 0)),
                  pl.BlockSpec((tt, D_MODEL), lambda i, e: (i, 0)),
                  pl.BlockSpec((te, D_MODEL), lambda i, e: (e, 0)),
                  pl.BlockSpec((D_MODEL, te), lambda i, e: (0, e)),
                  pl.BlockSpec((tt // LANES, PEER_HEADS, ib, LANES), lambda i, e: (i, 0, e, 0)),
                  pl.BlockSpec((tt // LANES, PEER_HEADS, ib, LANES), lambda i, e: (i, 0, e, 0)),
                  pl.BlockSpec((rows // BF16_ROWS, BF16_ROWS, tt), lambda i, e: (0, 0, i)),
                  pl.BlockSpec((rows // BF16_ROWS, BF16_ROWS, tt), lambda i, e: (0, 0, i)),
                  pl.BlockSpec((1, D_MODEL), lambda i, e: (0, 0)),
                  pl.BlockSpec((1, D_MODEL), lambda i, e: (0, 0))],
        out_specs=pl.BlockSpec((tt, D_MODEL), lambda i, e: (i, 0)),
        out_shape=jax.ShapeDtypeStruct((t, D_MODEL), F32),
        scratch_shapes=[pltpu.VMEM((D_MODEL, tt), F32)],
        compiler_params=_cparams(("parallel", "arbitrary")),
        name="peer_mix_ln2",
    )(x1b, x1, u, vt, ra, ea, cb, eb, g, bta)


def _block_diag(w, per_tile):
    n, k, _ = w.shape
    w = w.reshape(n // per_tile, per_tile, k, k)
    eye = jnp.eye(per_tile, dtype=w.dtype)
    out = jnp.einsum('tpij,pq->tpiqj', w, eye)
    return out.reshape(n // per_tile, per_tile * k, per_tile * k)


def _layer(x, mem, w_in, b_gate, conv_w, conv_b, lru_wa, lru_ba, lru_wx, lru_bx, lru_lambda,
           w_mem_kv, w_br_attn, w_br_lru, w_br_mem, w_out, ln1_g, ln1_b,
           peer_wq, peer_keys, peer_u, peer_v, ln2_g, ln2_b):
    b, s, d = x.shape
    t = b * s
    xf = x.reshape(t, d)
    x_perm = [xp.reshape(t, d) for xp in _permute_rows(x, DSWA_DILATIONS)]
    xb = x_perm[DSWA_DILATIONS.index(1)]
    w_in_b = w_in.astype(BF16)
    row = lambda a: a.reshape(1, -1).astype(F32)

    off_rnn = 3 * DSWA_WIDTH
    off_mq = off_rnn + 2 * D_RNN
    off_gl = off_mq + MEM_WIDTH
    mq = _matmul(xb, w_in_b, off_mq, MEM_WIDTH, BF16, tn=512).reshape(b, s, MEM_WIDTH)

    os, lses = [], []
    for gi, dil in enumerate(DSWA_DILATIONS):
        xp = x_perm[gi]
        w_g = jnp.concatenate([w_in_b[:, part * DSWA_WIDTH + gi * GROUP_W:
                                      part * DSWA_WIDTH + (gi + 1) * GROUP_W] for part in range(3)], axis=1)
        qkv = _matmul(xp, w_g, 0, 3 * GROUP_W, BF16, tn=3 * GROUP_W).reshape(b, dil, s // dil, 3 * GROUP_W)
        o_g, lse_g = _dswa_group(qkv, gi)
        os.append(o_g.reshape(t, GROUP_W))
        lses.append(lse_g.reshape(t, HEAD_DIM))

    per_tile = MXU_TILE // LRU_BW
    wa_bd = _block_diag(lru_wa, per_tile).astype(BF16)
    wx_bd = _block_diag(lru_wx, per_tile).astype(BF16)
    rec = _rglru(xb.reshape(b, s, d), w_in_b[:, off_rnn:off_mq], conv_w.astype(F32), row(conv_b),
                 wa_bd, row(lru_ba), wx_bd, row(lru_bx), row(lru_lambda)).reshape(t, D_RNN)

    mem_len = mem.shape[1]
    kv = _matmul(mem.reshape(b * mem_len, d).astype(BF16), w_mem_kv.astype(BF16), 0,
                 2 * MEM_WIDTH, BF16, tn=1024).reshape(b, mem_len, 2 * MEM_WIDTH)
    memo = _memattn(mq, kv).reshape(t, MEM_WIDTH)

    x1, x1b = _mix(os, lses, rec, memo, xf, w_in_b[:, off_gl:], b_gate.astype(F32), w_br_attn.astype(BF16),
                   w_br_lru.astype(BF16), w_br_mem.astype(BF16), w_out.astype(BF16),
                   row(ln1_g), row(ln1_b))

    kbd = _block_diag(peer_keys.reshape(PEER_HEADS * 2, N_KEYS, PEER_KEY_DIM // 2), 2).astype(BF16)
    scores = _peer_scores(x1b, peer_wq.astype(BF16), kbd)

    ra, ea, cb, eb = _peer_select(scores)
    ra = ra.reshape(t // LANES, PEER_HEADS, N_KEYS, LANES)
    ea = ea.reshape(t // LANES, PEER_HEADS, N_KEYS, LANES)

    out = _peer_mix(x1b, x1, peer_u.astype(BF16), peer_v.T.astype(BF16), ra, ea, cb, eb,
                    row(ln2_g), row(ln2_b))
    return out.reshape(b, s, d)


def kernel(x, mem, w_in, b_gate, conv_w, conv_b, lru_wa, lru_ba, lru_wx, lru_bx, lru_lambda, w_mem_kv, w_br_attn, w_br_lru, w_br_mem, w_out, ln1_g, ln1_b, peer_wq, peer_keys, peer_u, peer_v, ln2_g, ln2_b):
    h = x.astype(F32)
    depth = w_in.shape[0]
    for l in range(depth):
        h = _layer(h, mem, w_in[l], b_gate[l], conv_w[l], conv_b[l], lru_wa[l], lru_ba[l],
                   lru_wx[l], lru_bx[l], lru_lambda[l], w_mem_kv[l], w_br_attn[l], w_br_lru[l],
                   w_br_mem[l], w_out[l], ln1_g[l], ln1_b[l], peer_wq[l], peer_keys[l],
                   peer_u[l], peer_v[l], ln2_g[l], ln2_b[l])
    return h.astype(x.dtype)
```

```python
import functools
import math

import jax
import jax.numpy as jnp
from jax import lax
from jax.experimental import pallas as pl
from jax.experimental.pallas import tpu as pltpu

F32 = jnp.float32
BF16 = jnp.bfloat16

D_MODEL = 1024
N_GROUPS = 3
DSWA_DILATIONS = (1, 4, 16)
DSWA_HEADS = 4
HEAD_DIM = 128
DSWA_BLK = 128
DSWA_BATCH_BLOCKS = 4
DSWA_TILE = 2048
GROUP_W = DSWA_HEADS * HEAD_DIM
DSWA_WIDTH = N_GROUPS * GROUP_W
D_RNN = 1024
LRU_BLOCKS = 16
LRU_BW = D_RNN // LRU_BLOCKS
CONV_W = 4
LRU_C = 8.0
MEM_HEADS = 4
MEM_WIDTH = MEM_HEADS * HEAD_DIM
PEER_HEADS = 8
PEER_KEY_DIM = 256
N_KEYS = 128
N_EXPERTS = N_KEYS * N_KEYS
PEER_TOPK = 16
PEER_CHUNK_KEYS = (8,) * 2
BF16_ROWS = 16
PEER_MASK_LANES = 256
ALPHA = 2.0 ** 0.25
LN_EPS = 1e-5
NEG_INF = -1e30

LANES = 128
SUBLANES = 8
MXU_TILE = 256
VMEM_LIMIT = 56 * 1024 * 1024


def _cparams(sem):
    return pltpu.CompilerParams(dimension_semantics=sem, vmem_limit_bytes=VMEM_LIMIT)


def _dot(a, b):
    return jnp.dot(a, b, preferred_element_type=F32)


def _dot_nt(a, b):
    return lax.dot_general(a, b, (((1,), (1,)), ((), ())), preferred_element_type=F32)


def _gelu_erf(x):
    return 0.5 * x * (1.0 + lax.erf(x * (1.0 / math.sqrt(2.0))))


def _layernorm(h, g, b):
    mu = jnp.mean(h, axis=-1, keepdims=True)
    c = h - mu
    var = jnp.mean(c * c, axis=-1, keepdims=True)
    return c * lax.rsqrt(var + LN_EPS) * g + b


def _matmul_kernel(x_ref, w_ref, o_ref):
    o_ref[...] = _dot(x_ref[...], w_ref[...]).astype(o_ref.dtype)


def _matmul(x, w, col_off, n_cols, out_dtype, tm=1024, tn=1024):
    m, k = x.shape
    tn = min(tn, n_cols)
    tm = min(tm, m)
    assert m % tm == 0 and n_cols % tn == 0 and col_off % tn == 0
    off = col_off // tn
    return pl.pallas_call(
        _matmul_kernel,
        grid=(m // tm, n_cols // tn),
        in_specs=[pl.BlockSpec((tm, k), lambda i, j: (i, 0)),
                  pl.BlockSpec((k, tn), lambda i, j: (0, off + j))],
        out_specs=pl.BlockSpec((tm, tn), lambda i, j: (i, j)),
        out_shape=jax.ShapeDtypeStruct((m, n_cols), out_dtype),
        compiler_params=_cparams(("parallel", "arbitrary")),
        name="matmul",
    )(x, w)


def _permute_kernel(x_ref, *o_refs, dilations):
    for o_ref, dilation in zip(o_refs, dilations):
        rows = x_ref.shape[1] // dilation
        for bi in range(x_ref.shape[0]):
            for r in range(dilation):
                o_ref[bi, r] = x_ref[bi, pl.ds(r, rows, stride=dilation), :].astype(o_ref.dtype)


def _permute_rows(x, dilations):
    b, s, d = x.shape
    return pl.pallas_call(
        functools.partial(_permute_kernel, dilations=dilations),
        grid=(s // DSWA_TILE, d // LANES),
        in_specs=[pl.BlockSpec((b, DSWA_TILE, LANES), lambda n, c: (0, n, c))],
        out_specs=[pl.BlockSpec((b, dil, DSWA_TILE // dil, LANES), lambda n, c: (0, 0, n, c))
                   for dil in dilations],
        out_shape=[jax.ShapeDtypeStruct((b, dil, s // dil, d), BF16) for dil in dilations],
        compiler_params=_cparams(("parallel", "parallel")),
        name="permute",
    )(x)


def _dswa_kernel(q_ref, kc_ref, kp_ref, vc_ref, vp_ref, o_ref, lse_ref, o_scr, lse_scr, *, dilation):
    nq = q_ref.shape[2] // DSWA_BLK
    row = lax.broadcasted_iota(jnp.int32, (DSWA_BLK, DSWA_BLK), 0)
    col = lax.broadcasted_iota(jnp.int32, (DSWA_BLK, DSWA_BLK), 1)
    mask_cur = col <= row
    mask_band = col >= row
    mask_first = col >= row + jnp.where(pl.program_id(1) == 0, DSWA_BLK, 0)
    scale = 1.0 / math.sqrt(HEAD_DIM)
    bdot_nt = lambda a, b: jnp.einsum('hqd,hkd->hqk', a, b, preferred_element_type=F32)
    bdot = lambda a, b: jnp.einsum('hqk,hkd->hqd', a, b, preferred_element_type=F32)

    def heads(blocks):
        return jnp.stack([a[:, h * HEAD_DIM:(h + 1) * HEAD_DIM]
                          for a in blocks for h in range(DSWA_HEADS)], axis=0)

    def attend(blocks):
        qs, kps, kcs, vps, vcs = [], [], [], [], []
        for r, j in blocks:
            rows = slice(j * DSWA_BLK, (j + 1) * DSWA_BLK)
            prows = slice((j - 1) * DSWA_BLK, j * DSWA_BLK)
            qs.append(q_ref[0, r, rows, :])
            kcs.append(kc_ref[0, r, rows, :])
            vcs.append(vc_ref[0, r, rows, :])
            kps.append(kp_ref[0, r] if j == 0 else kc_ref[0, r, prows, :])
            vps.append(vp_ref[0, r] if j == 0 else vc_ref[0, r, prows, :])
        mask_prev = mask_first if blocks[0][1] == 0 else mask_band
        q4 = heads(qs)
        s_p = jnp.where(mask_prev[None], bdot_nt(q4, heads(kps)) * scale, NEG_INF)
        s_c = jnp.where(mask_cur[None], bdot_nt(q4, heads(kcs)) * scale, NEG_INF)
        m = jnp.maximum(jnp.max(s_p, axis=-1, keepdims=True), jnp.max(s_c, axis=-1, keepdims=True))
        p_p = jnp.exp(s_p - m)
        p_c = jnp.exp(s_c - m)
        l = jnp.sum(p_p, axis=-1, keepdims=True) + jnp.sum(p_c, axis=-1, keepdims=True)
        o = (bdot(p_p.astype(BF16), heads(vps)) + bdot(p_c.astype(BF16), heads(vcs))) / l
        lse = m + jnp.log(l)
        for bi, (r, j) in enumerate(blocks):
            dst = pl.ds(j * DSWA_BLK * dilation + r, DSWA_BLK, stride=dilation)
            lse_tile = jnp.zeros((DSWA_BLK, HEAD_DIM), F32)
            for h in range(DSWA_HEADS):
                o_scr[h, dst, :] = o[bi * DSWA_HEADS + h]
                lse_tile = jnp.where(col == h, lse[bi * DSWA_HEADS + h], lse_tile)
            lse_scr[dst, :] = lse_tile

    for first in (True, False):
        group = [(r, j) for j in range(nq) for r in range(dilation) if (j == 0) == first]
        for i in range(0, len(group), DSWA_BATCH_BLOCKS):
            attend(group[i:i + DSWA_BATCH_BLOCKS])
    for h in range(DSWA_HEADS):
        o_ref[0, :, h * HEAD_DIM:(h + 1) * HEAD_DIM] = o_scr[h].astype(o_ref.dtype)
    lse_ref[0] = lse_scr[...]


def _dswa_group(qkv, group):
    b, dilation, m_len, _ = qkv.shape
    s = dilation * m_len
    assert s % DSWA_TILE == 0 and DSWA_TILE % (dilation * DSWA_BLK) == 0
    rows = DSWA_TILE // dilation
    nq = rows // DSWA_BLK

    def cur(c):
        return pl.BlockSpec((1, dilation, rows, GROUP_W), lambda bi, n: (bi, 0, n, c))

    def prev(c):
        return pl.BlockSpec((1, dilation, DSWA_BLK, GROUP_W),
                            lambda bi, n: (bi, 0, jnp.maximum(n * nq - 1, 0), c))

    return pl.pallas_call(
        functools.partial(_dswa_kernel, dilation=dilation),
        grid=(b, s // DSWA_TILE),
        in_specs=[cur(0), cur(1), prev(1), cur(2), prev(2)],
        out_specs=[pl.BlockSpec((1, DSWA_TILE, GROUP_W), lambda bi, n: (bi, n, 0)),
                   pl.BlockSpec((1, DSWA_TILE, HEAD_DIM), lambda bi, n: (bi, n, 0))],
        out_shape=[jax.ShapeDtypeStruct((b, s, GROUP_W), BF16),
                   jax.ShapeDtypeStruct((b, s, HEAD_DIM), F32)],
        scratch_shapes=[pltpu.VMEM((DSWA_HEADS, DSWA_TILE, HEAD_DIM), F32),
                        pltpu.VMEM((DSWA_TILE, HEAD_DIM), F32)],
        compiler_params=_cparams(("parallel", "arbitrary")),
        name=f"dswa_g{group}",
    )(qkv, qkv, qkv, qkv, qkv)


def _linear_scan(a, b, h0):
    ts, d = a.shape
    a = a.reshape(ts // SUBLANES, SUBLANES, d)
    b = b.reshape(ts // SUBLANES, SUBLANES, d)
    row = lax.broadcasted_iota(jnp.int32, a.shape, 1)
    step = 1
    while step < SUBLANES:
        keep = row >= step
        a_s = jnp.where(keep, pltpu.roll(a, step, 1), 1.0)
        b_s = jnp.where(keep, pltpu.roll(b, step, 1), 0.0)
        b = a * b_s + b
        a = a * a_s
        step *= 2
    hs, carry = [], h0
    for g in range(ts // SUBLANES):
        hg = a[g] * carry + b[g]
        hs.append(hg)
        carry = hg[SUBLANES - 1:SUBLANES, :]
    return jnp.concatenate(hs, axis=0), carry


def _rglru_kernel(xb_ref, halo_ref, w_ref, cw_ref, cb_ref, wa_ref, ba_ref, wx_ref, bx_ref,
                  lam_ref, o_ref, h_ref, *, ts):
    si = pl.program_id(1)

    @pl.when(si == 0)
    def _():
        h_ref[...] = jnp.zeros_like(h_ref)

    z = _dot(xb_ref[0], w_ref[...])
    x = z[:, :D_RNN]
    yg = z[:, D_RNN:]
    halo = _dot(halo_ref[0], w_ref[:, :D_RNN])[BF16_ROWS - SUBLANES:, :]
    halo = jnp.where(si > 0, halo, 0.0)
    xfull = jnp.concatenate([halo, x], axis=0)
    xc = cb_ref[...] + cw_ref[CONV_W - 1:CONV_W, :] * x
    for k in range(1, CONV_W):
        xc = xc + cw_ref[CONV_W - 1 - k:CONV_W - k, :] * pltpu.roll(xfull, k, 0)[SUBLANES:]

    xb = xc.astype(BF16)
    nblk = D_RNN // MXU_TILE
    ra = jnp.concatenate(
        [_dot(xb[:, c * MXU_TILE:(c + 1) * MXU_TILE], wa_ref[c]) for c in range(nblk)], axis=1)
    rx = jnp.concatenate(
        [_dot(xb[:, c * MXU_TILE:(c + 1) * MXU_TILE], wx_ref[c]) for c in range(nblk)], axis=1)
    r = jax.nn.sigmoid(ra + ba_ref[...])
    gate_i = jax.nn.sigmoid(rx + bx_ref[...])
    neg_lam = -lam_ref[...]
    softplus = jnp.maximum(neg_lam, 0.0) + jnp.log1p(jnp.exp(-jnp.abs(neg_lam)))
    log_a = (-LRU_C) * r * softplus
    a = jnp.exp(log_a)
    bterm = jnp.sqrt(1.0 - a * a) * gate_i * xc

    h, carry = _linear_scan(a, bterm, h_ref[0:1, :])
    h_ref[0:1, :] = carry
    o_ref[0] = (h * jax.nn.gelu(yg, approximate=True)).astype(o_ref.dtype)


def _rglru(xb, w_rnn, conv_w, conv_b, wa_bd, ba, wx_bd, bx, lam, ts=1024):
    b, s, d = xb.shape
    assert s % ts == 0
    full = lambda shape: pl.BlockSpec(shape, lambda bi, si: (0,) * len(shape))
    return pl.pallas_call(
        functools.partial(_rglru_kernel, ts=ts),
        grid=(b, s // ts),
        in_specs=[pl.BlockSpec((1, ts, d), lambda bi, si: (bi, si, 0)),
                  pl.BlockSpec((1, BF16_ROWS, d),
                               lambda bi, si: (bi, jnp.maximum(si * (ts // BF16_ROWS) - 1, 0), 0)),
                  full(w_rnn.shape),
                  full((CONV_W, D_RNN)), full((1, D_RNN)),
                  full(wa_bd.shape), full((1, D_RNN)),
                  full(wx_bd.shape), full((1, D_RNN)), full((1, D_RNN))],
        out_specs=pl.BlockSpec((1, ts, D_RNN), lambda bi, si: (bi, si, 0)),
        out_shape=jax.ShapeDtypeStruct((b, s, D_RNN), BF16),
        scratch_shapes=[pltpu.VMEM((SUBLANES, D_RNN), F32)],
        compiler_params=_cparams(("parallel", "arbitrary")),
        name="rglru",
    )(xb, xb, w_rnn, conv_w, conv_b, wa_bd, ba, wx_bd, bx, lam)


def _memattn_kernel(xb_ref, wq_ref, kv_ref, o_ref):
    scale = 1.0 / math.sqrt(HEAD_DIM)
    q = _dot(xb_ref[0], wq_ref[...]).astype(BF16)
    for h in range(MEM_HEADS):
        hs = slice(h * HEAD_DIM, (h + 1) * HEAD_DIM)
        vs = slice(MEM_WIDTH + h * HEAD_DIM, MEM_WIDTH + (h + 1) * HEAD_DIM)
        s = _dot_nt(q[:, hs], kv_ref[0, :, hs]) * scale
        m = jnp.max(s, axis=-1, keepdims=True)
        p = jnp.exp(s - m)
        l = jnp.sum(p, axis=-1, keepdims=True)
        o_ref[0, :, hs] = (_dot(p.astype(BF16), kv_ref[0, :, vs]) / l).astype(o_ref.dtype)


def _memattn(xb, w_mq, kv, ts=1024):
    b, s, d = xb.shape
    mem_len = kv.shape[1]
    return pl.pallas_call(
        _memattn_kernel,
        grid=(b, s // ts),
        in_specs=[pl.BlockSpec((1, ts, d), lambda bi, si: (bi, si, 0)),
                  pl.BlockSpec(w_mq.shape, lambda bi, si: (0, 0), pipeline_mode=pl.Buffered(1)),
                  pl.BlockSpec((1, mem_len, 2 * MEM_WIDTH), lambda bi, si: (bi, 0, 0))],
        out_specs=pl.BlockSpec((1, ts, MEM_WIDTH), lambda bi, si: (bi, si, 0)),
        out_shape=jax.ShapeDtypeStruct((b, s, MEM_WIDTH), BF16),
        compiler_params=_cparams(("parallel", "arbitrary")),
        name="memattn",
    )(xb, w_mq, kv)


def _mix_kernel(o0_ref, o1_ref, o2_ref, l0_ref, l1_ref, l2_ref, rec_ref, memo_ref, x_ref,
                wg_ref, bg_ref, wa_ref, wl_ref, wm_ref, wo_ref, g_ref, b_ref, x1_ref, x1b_ref):
    l0, l1, l2 = l0_ref[...], l1_ref[...], l2_ref[...]
    mx = jnp.maximum(jnp.maximum(l0, l1), l2)
    e0, e1, e2 = jnp.exp(l0 - mx), jnp.exp(l1 - mx), jnp.exp(l2 - mx)
    inv = 1.0 / (e0 + e1 + e2)
    w0, w1, w2 = e0 * inv, e1 * inv, e2 * inv
    parts = []
    for h in range(DSWA_HEADS):
        hs = slice(h * HEAD_DIM, (h + 1) * HEAD_DIM)
        parts.append((w0[:, h:h + 1] * o0_ref[:, hs].astype(F32)
                      + w1[:, h:h + 1] * o1_ref[:, hs].astype(F32)
                      + w2[:, h:h + 1] * o2_ref[:, hs].astype(F32)).astype(BF16))
    attn = jnp.concatenate(parts, axis=1)
    d = D_MODEL
    gl = _dot(x_ref[...].astype(BF16), wg_ref[...])
    gate = lambda j: jax.nn.sigmoid(gl[:, j * d:(j + 1) * d] + bg_ref[j:j + 1, :])
    merged = (gate(0) * _dot(attn, wa_ref[...])
              + gate(1) * _dot(rec_ref[...], wl_ref[...])
              + gate(2) * _dot(memo_ref[...], wm_ref[...]))
    mix = _dot(merged.astype(BF16), wo_ref[...])
    x1 = _layernorm(ALPHA * x_ref[...] + mix, g_ref[...], b_ref[...])
    x1_ref[...] = x1
    x1b_ref[...] = x1.astype(BF16)


def _mix(os, lses, rec, memo, x, w_gate, b_gate, wa, wl, wm, wo, g, bta, tt=512):
    t = x.shape[0]
    rows = lambda w: pl.BlockSpec((tt, w), lambda i: (i, 0))
    full = lambda a: pl.BlockSpec(a.shape, lambda i: (0,) * a.ndim, pipeline_mode=pl.Buffered(1))
    return pl.pallas_call(
        _mix_kernel,
        grid=(t // tt,),
        in_specs=[rows(GROUP_W)] * 3 + [rows(HEAD_DIM)] * 3 + [
            rows(D_RNN), rows(MEM_WIDTH), rows(D_MODEL), full(w_gate),
            full(b_gate), full(wa), full(wl), full(wm), full(wo), full(g), full(bta)],
        out_specs=[rows(D_MODEL), rows(D_MODEL)],
        out_shape=[jax.ShapeDtypeStruct((t, D_MODEL), F32), jax.ShapeDtypeStruct((t, D_MODEL), BF16)],
        compiler_params=_cparams(("parallel",)),
        name="mix_ln1",
    )(*os, *lses, rec, memo, x, w_gate, b_gate, wa, wl, wm, wo, g, bta)


def _peer_scores_kernel(x_ref, wq_ref, kbd_ref, o_ref):
    q = _dot(x_ref[...], wq_ref[...]).astype(BF16)
    for h in range(PEER_HEADS):
        hs = slice(h * PEER_KEY_DIM, (h + 1) * PEER_KEY_DIM)
        o_ref[hs, :] = _dot_nt(kbd_ref[h], q[:, hs])


def _peer_scores(x1b, wq, kbd, tt=1024):
    t = x1b.shape[0]
    w = PEER_HEADS * PEER_KEY_DIM
    return pl.pallas_call(
        _peer_scores_kernel,
        grid=(t // tt,),
        in_specs=[pl.BlockSpec((tt, D_MODEL), lambda i: (i, 0)),
                  pl.BlockSpec(wq.shape, lambda i: (0, 0)),
                  pl.BlockSpec(kbd.shape, lambda i: (0, 0, 0))],
        out_specs=pl.BlockSpec((w, tt), lambda i: (0, i)),
        out_shape=jax.ShapeDtypeStruct((w, t), F32),
        compiler_params=_cparams(("parallel",)),
        name="peer_scores",
    )(x1b, wq, kbd)


_PEER_CANDS = tuple((p, q) for p in range(PEER_TOPK) for q in range(PEER_TOPK)
                    if (p + 1) * (q + 1) <= PEER_TOPK)


def _top_positions(s, exact_ties):
    key = lax.broadcasted_iota(jnp.int32, s.shape, 0).astype(F32)
    pos = jnp.full(s.shape, float(PEER_TOPK), F32)
    work = s
    tops = []
    for p in range(PEER_TOPK):
        mx = jnp.max(work, axis=0, keepdims=True)
        if exact_ties:
            first = jnp.min(jnp.where(work == mx, key, float(N_KEYS)), axis=0, keepdims=True)
            hit = key == first
        else:
            hit = work == mx
        pos = jnp.where(hit, float(p), pos)
        work = jnp.where(hit, -jnp.inf, work)
        tops.append(mx)
    return pos, tops


def _store_lane_tiles(ref, h, val):
    for c in range(ref.shape[0]):
        ref[c, h * N_KEYS:(h + 1) * N_KEYS, :] = val[:, c * LANES:(c + 1) * LANES]


def _peer_select_body(sc_ref, ra_ref, ea_ref, cb_ref, eb_ref, exact_ties):
    tt = sc_ref.shape[1]
    kg = N_KEYS // BF16_ROWS
    pos_b, tops_a, tops_b = [], [], []
    marked = jnp.zeros((1, tt), F32)
    for h in range(PEER_HEADS):
        base = h * PEER_KEY_DIM
        pa, ta = _top_positions(sc_ref[base:base + N_KEYS, :], exact_ties)
        pb, tb = _top_positions(sc_ref[base + N_KEYS:base + 2 * N_KEYS, :], exact_ties)
        _store_lane_tiles(ra_ref, h, pa)
        pos_b.append(pb)
        tops_a.append(ta)
        tops_b.append(tb)
        if not exact_ties:
            for pos in (pa, pb):
                n_marked = jnp.sum(jnp.where(pos < float(PEER_TOPK), 1.0, 0.0), axis=0, keepdims=True)
                marked = jnp.maximum(marked, n_marked)
    a_top = [jnp.concatenate([tops_a[h][p] for h in range(PEER_HEADS)], axis=0)
             for p in range(PEER_TOPK)]
    b_top = [jnp.concatenate([tops_b[h][q] for h in range(PEER_HEADS)], axis=0)
             for q in range(PEER_TOPK)]
    sums = [a_top[p] + b_top[q] for p, q in _PEER_CANDS]
    n = len(sums)
    rank = [jnp.full((PEER_HEADS, tt), float(n - 1 - c), F32) for c in range(n)]
    for c in range(n):
        for c2 in range(c):
            c2_wins = jnp.where(sums[c2] >= sums[c], 1.0, 0.0)
            rank[c] = rank[c] + c2_wins
            rank[c2] = rank[c2] - c2_wins
    cnt = [jnp.zeros((PEER_HEADS, tt), F32) for _ in range(PEER_TOPK)]
    zsum = jnp.zeros((PEER_HEADS, tt), F32)
    for c, (p, q) in enumerate(_PEER_CANDS):
        sel = rank[c] < float(PEER_TOPK)
        cnt[q] = cnt[q] + jnp.where(sel, 1.0, 0.0)
        zsum = zsum + jnp.where(sel, jnp.exp(sums[c] - sums[0]), 0.0)
    inv_z = 1.0 / zsum
    for h in range(PEER_HEADS):
        base = h * PEER_KEY_DIM
        rows = slice(h * N_KEYS, (h + 1) * N_KEYS)
        posb = pos_b[h].astype(BF16).reshape(kg, BF16_ROWS, tt)
        cb = jnp.zeros((kg, BF16_ROWS, tt), BF16)
        for q in range(PEER_TOPK):
            cnt_q = jnp.broadcast_to(cnt[q][h:h + 1, :], (BF16_ROWS, tt)).astype(BF16)
            cb = jnp.where(posb == float(q), cnt_q[None], cb)
        cb_ref[h * kg:(h + 1) * kg] = cb
        _store_lane_tiles(ea_ref, h, jnp.exp(sc_ref[base:base + N_KEYS, :] - a_top[0][h:h + 1, :])
                          * inv_z[h:h + 1, :])
        eb = jnp.exp(sc_ref[base + N_KEYS:base + 2 * N_KEYS, :] - b_top[0][h:h + 1, :])
        eb_ref[h * kg:(h + 1) * kg] = eb.astype(BF16).reshape(kg, BF16_ROWS, tt)
    return marked


def _peer_select_kernel(sc_ref, ra_ref, ea_ref, cb_ref, eb_ref):
    marked = _peer_select_body(sc_ref, ra_ref, ea_ref, cb_ref, eb_ref, exact_ties=False)

    @pl.when(jnp.max(marked) > float(PEER_TOPK))
    def _():
        _peer_select_body(sc_ref, ra_ref, ea_ref, cb_ref, eb_ref, exact_ties=True)


def _peer_select(sc, tt=256):
    w, t = sc.shape
    rows = PEER_HEADS * N_KEYS
    out_spec_a = pl.BlockSpec((tt // LANES, rows, LANES), lambda i: (i, 0, 0))
    shape_a = jax.ShapeDtypeStruct((t // LANES, rows, LANES), F32)
    out_spec_b = pl.BlockSpec((rows // BF16_ROWS, BF16_ROWS, tt), lambda i: (0, 0, i))
    shape_b = jax.ShapeDtypeStruct((rows // BF16_ROWS, BF16_ROWS, t), BF16)
    return pl.pallas_call(
        _peer_select_kernel,
        grid=(t // tt,),
        in_specs=[pl.BlockSpec((w, tt), lambda i: (0, i))],
        out_specs=[out_spec_a, out_spec_a, out_spec_b, out_spec_b],
        out_shape=[shape_a, shape_a, shape_b, shape_b],
        compiler_params=_cparams(("parallel",)),
        name="peer_select",
    )(sc)


def _row_on_sublanes(ref, h, i, lane0, lanes):
    tiles = range(lane0 // LANES, (lane0 + lanes) // LANES)
    return jnp.concatenate([ref[c, h, pl.ds(i, BF16_ROWS, stride=0), :] for c in tiles], axis=1)


def _masked_acts(hmat, ra_ref, ea_ref, cb_ref, eb_ref, key0):
    tt = hmat.shape[1]
    kg = N_KEYS // BF16_ROWS
    ys = []
    for k in range(hmat.shape[0] // N_KEYS):
        rows = slice(k * N_KEYS, (k + 1) * N_KEYS)
        ws = []
        for l0 in range(0, tt, PEER_MASK_LANES):
            ls = slice(l0, l0 + PEER_MASK_LANES)
            w = jnp.zeros((kg, BF16_ROWS, PEER_MASK_LANES), BF16)
            for h in range(PEER_HEADS):
                hs = slice(h * kg, (h + 1) * kg)
                ra = _row_on_sublanes(ra_ref, h, key0 + k, l0, PEER_MASK_LANES).astype(BF16)
                ea = _row_on_sublanes(ea_ref, h, key0 + k, l0, PEER_MASK_LANES).astype(BF16)
                w = w + eb_ref[hs, :, ls] * jnp.where(cb_ref[hs, :, ls] > ra[None], ea[None],
                                                      jnp.zeros_like(w))
            ws.append(w.reshape(N_KEYS, PEER_MASK_LANES))
        ys.append(jnp.concatenate(ws, axis=1) * _gelu_erf(hmat[rows, :]).astype(BF16))
    return jnp.concatenate(ys, axis=0)


def _peer_mix_kernel(x1b_ref, x1_ref, u_ref, vt_ref, ra_ref, ea_ref, cb_ref, eb_ref,
                     g_ref, b_ref, o_ref, acc_ref, *, ib):
    e = pl.program_id(1)

    @pl.when(e == 0)
    def _():
        acc_ref[...] = jnp.zeros_like(acc_ref)

    x1b = x1b_ref[...]
    acc = acc_ref[...]
    assert sum(PEER_CHUNK_KEYS) == ib
    starts = [sum(PEER_CHUNK_KEYS[:c]) for c in range(len(PEER_CHUNK_KEYS) + 1)]
    chunk_rows = lambda c: slice(starts[c] * N_KEYS, starts[c + 1] * N_KEYS)
    n_chunks = len(PEER_CHUNK_KEYS)
    h_next = _dot_nt(u_ref[chunk_rows(0), :], x1b)
    for c in range(n_chunks):
        hc = h_next
        if c + 1 < n_chunks:
            h_next = _dot_nt(u_ref[chunk_rows(c + 1), :], x1b)
        y = _masked_acts(hc, ra_ref, ea_ref, cb_ref, eb_ref, starts[c])
        acc = acc + _dot(vt_ref[:, chunk_rows(c)], y)
    acc_ref[...] = acc

    @pl.when(e == pl.num_programs(1) - 1)
    def _():
        o_ref[...] = _layernorm(ALPHA * x1_ref[...] + acc_ref[...].T, g_ref[...], b_ref[...])


def _peer_mix(x1b, x1, u, vt, ra, ea, cb, eb, g, bta, tt=512, te=sum(PEER_CHUNK_KEYS) * N_KEYS):
    t = x1.shape[0]
    ib = te // N_KEYS
    ne = N_EXPERTS // te
    rows = PEER_HEADS * N_KEYS
    return pl.pallas_call(
        functools.partial(_peer_mix_kernel, ib=ib),
        grid=(t // tt, ne),
        in_specs=[pl.BlockSpec((tt, D_MODEL), lambda i, e: (i, 0)),
                  pl.BlockSpec((tt, D_MODEL), lambda i, e: (i, 0)),
                  pl.BlockSpec((te, D_MODEL), lambda i, e: (e, 0)),
                  pl.BlockSpec((D_MODEL, te), lambda i, e: (0, e)),
                  pl.BlockSpec((tt // LANES, PEER_HEADS, ib, LANES), lambda i, e: (i, 0, e, 0)),
                  pl.BlockSpec((tt // LANES, PEER_HEADS, ib, LANES), lambda i, e: (i, 0, e, 0)),
                  pl.BlockSpec((rows // BF16_ROWS, BF16_ROWS, tt), lambda i, e: (0, 0, i)),
                  pl.BlockSpec((rows // BF16_ROWS, BF16_ROWS, tt), lambda i, e: (0, 0, i)),
                  pl.BlockSpec((1, D_MODEL), lambda i, e: (0, 0)),
                  pl.BlockSpec((1, D_MODEL), lambda i, e: (0, 0))],
        out_specs=pl.BlockSpec((tt, D_MODEL), lambda i, e: (i, 0)),
        out_shape=jax.ShapeDtypeStruct((t, D_MODEL), F32),
        scratch_shapes=[pltpu.VMEM((D_MODEL, tt), F32)],
        compiler_params=_cparams(("parallel", "arbitrary")),
        name="peer_mix_ln2",
    )(x1b, x1, u, vt, ra, ea, cb, eb, g, bta)


def _block_diag(w, per_tile):
    n, k, _ = w.shape
    w = w.reshape(n // per_tile, per_tile, k, k)
    eye = jnp.eye(per_tile, dtype=w.dtype)
    out = jnp.einsum('tpij,pq->tpiqj', w, eye)
    return out.reshape(n // per_tile, per_tile * k, per_tile * k)


def _layer(x, mem, w_in, b_gate, conv_w, conv_b, lru_wa, lru_ba, lru_wx, lru_bx, lru_lambda,
           w_mem_kv, w_br_attn, w_br_lru, w_br_mem, w_out, ln1_g, ln1_b,
           peer_wq, peer_keys, peer_u, peer_v, ln2_g, ln2_b):
    b, s, d = x.shape
    t = b * s
    xf = x.reshape(t, d)
    x_perm = [xp.reshape(t, d) for xp in _permute_rows(x, DSWA_DILATIONS)]
    xb = x_perm[DSWA_DILATIONS.index(1)]
    w_in_b = w_in.astype(BF16)
    row = lambda a: a.reshape(1, -1).astype(F32)

    off_rnn = 3 * DSWA_WIDTH
    off_mq = off_rnn + 2 * D_RNN
    off_gl = off_mq + MEM_WIDTH

    os, lses = [], []
    for gi, dil in enumerate(DSWA_DILATIONS):
        xp = x_perm[gi]
        w_g = jnp.concatenate([w_in_b[:, part * DSWA_WIDTH + gi * GROUP_W:
                                      part * DSWA_WIDTH + (gi + 1) * GROUP_W] for part in range(3)], axis=1)
        qkv = _matmul(xp, w_g, 0, 3 * GROUP_W, BF16, tn=3 * GROUP_W).reshape(b, dil, s // dil, 3 * GROUP_W)
        o_g, lse_g = _dswa_group(qkv, gi)
        os.append(o_g.reshape(t, GROUP_W))
        lses.append(lse_g.reshape(t, HEAD_DIM))

    per_tile = MXU_TILE // LRU_BW
    wa_bd = _block_diag(lru_wa, per_tile).astype(BF16)
    wx_bd = _block_diag(lru_wx, per_tile).astype(BF16)
    rec = _rglru(xb.reshape(b, s, d), w_in_b[:, off_rnn:off_mq], conv_w.astype(F32), row(conv_b),
                 wa_bd, row(lru_ba), wx_bd, row(lru_bx), row(lru_lambda)).reshape(t, D_RNN)

    mem_len = mem.shape[1]
    kv = _matmul(mem.reshape(b * mem_len, d).astype(BF16), w_mem_kv.astype(BF16), 0,
                 2 * MEM_WIDTH, BF16, tn=1024).reshape(b, mem_len, 2 * MEM_WIDTH)
    memo = _memattn(xb.reshape(b, s, d), w_in_b[:, off_mq:off_gl], kv).reshape(t, MEM_WIDTH)

    x1, x1b = _mix(os, lses, rec, memo, xf, w_in_b[:, off_gl:], b_gate.astype(F32), w_br_attn.astype(BF16),
                   w_br_lru.astype(BF16), w_br_mem.astype(BF16), w_out.astype(BF16),
                   row(ln1_g), row(ln1_b))

    kbd = _block_diag(peer_keys.reshape(PEER_HEADS * 2, N_KEYS, PEER_KEY_DIM // 2), 2).astype(BF16)
    scores = _peer_scores(x1b, peer_wq.astype(BF16), kbd)

    ra, ea, cb, eb = _peer_select(scores)
    ra = ra.reshape(t // LANES, PEER_HEADS, N_KEYS, LANES)
    ea = ea.reshape(t // LANES, PEER_HEADS, N_KEYS, LANES)

    out = _peer_mix(x1b, x1, peer_u.astype(BF16), peer_v.T.astype(BF16), ra, ea, cb, eb,
                    row(ln2_g), row(ln2_b))
    return out.reshape(b, s, d)


def kernel(x, mem, w_in, b_gate, conv_w, conv_b, lru_wa, lru_ba, lru_wx, lru_bx, lru_lambda, w_mem_kv, w_br_attn, w_br_lru, w_br_mem, w_out, ln1_g, ln1_b, peer_wq, peer_keys, peer_u, peer_v, ln2_g, ln2_b):
    h = x.astype(F32)
    depth = w_in.shape[0]
    for l in range(depth):
        h = _layer(h, mem, w_in[l], b_gate[l], conv_w[l], conv_b[l], lru_wa[l], lru_ba[l],
                   lru_wx[l], lru_bx[l], lru_lambda[l], w_mem_kv[l], w_br_attn[l], w_br_lru[l],
                   w_br_mem[l], w_out[l], ln1_g[l], ln1_b[l], peer_wq[l], peer_keys[l],
                   peer_u[l], peer_v[l], ln2_g[l], ln2_b[l])
    return h.astype(x.dtype)
```

```python
import functools
import math

import jax
import jax.numpy as jnp
from jax import lax
from jax.experimental import pallas as pl
from jax.experimental.pallas import tpu as pltpu

F32 = jnp.float32
BF16 = jnp.bfloat16

D_MODEL = 1024
N_GROUPS = 3
DSWA_DILATIONS = (1, 4, 16)
DSWA_HEADS = 4
HEAD_DIM = 128
DSWA_BLK = 128
DSWA_BATCH_BLOCKS = 4
DSWA_TILE = 2048
GROUP_W = DSWA_HEADS * HEAD_DIM
DSWA_WIDTH = N_GROUPS * GROUP_W
D_RNN = 1024
LRU_BLOCKS = 16
LRU_BW = D_RNN // LRU_BLOCKS
CONV_W = 4
LRU_C = 8.0
MEM_HEADS = 4
MEM_WIDTH = MEM_HEADS * HEAD_DIM
PEER_HEADS = 8
PEER_KEY_DIM = 256
N_KEYS = 128
N_EXPERTS = N_KEYS * N_KEYS
PEER_TOPK = 16
PEER_CHUNK_KEYS = (8,) * 2
BF16_ROWS = 16
PEER_MASK_LANES = 256
ALPHA = 2.0 ** 0.25
LN_EPS = 1e-5
NEG_INF = -1e30

LANES = 128
SUBLANES = 8
MXU_TILE = 256
VMEM_LIMIT = 56 * 1024 * 1024


def _cparams(sem):
    return pltpu.CompilerParams(dimension_semantics=sem, vmem_limit_bytes=VMEM_LIMIT)


def _dot(a, b):
    return jnp.dot(a, b, preferred_element_type=F32)


def _dot_nt(a, b):
    return lax.dot_general(a, b, (((1,), (1,)), ((), ())), preferred_element_type=F32)


def _gelu_erf(x):
    return 0.5 * x * (1.0 + lax.erf(x * (1.0 / math.sqrt(2.0))))


def _layernorm(h, g, b):
    mu = jnp.mean(h, axis=-1, keepdims=True)
    c = h - mu
    var = jnp.mean(c * c, axis=-1, keepdims=True)
    return c * lax.rsqrt(var + LN_EPS) * g + b


def _matmul_kernel(x_ref, w_ref, o_ref):
    o_ref[...] = _dot(x_ref[...], w_ref[...]).astype(o_ref.dtype)


def _matmul(x, w, col_off, n_cols, out_dtype, tm=1024, tn=1024):
    m, k = x.shape
    tn = min(tn, n_cols)
    tm = min(tm, m)
    assert m % tm == 0 and n_cols % tn == 0 and col_off % tn == 0
    off = col_off // tn
    return pl.pallas_call(
        _matmul_kernel,
        grid=(m // tm, n_cols // tn),
        in_specs=[pl.BlockSpec((tm, k), lambda i, j: (i, 0)),
                  pl.BlockSpec((k, tn), lambda i, j: (0, off + j))],
        out_specs=pl.BlockSpec((tm, tn), lambda i, j: (i, j)),
        out_shape=jax.ShapeDtypeStruct((m, n_cols), out_dtype),
        compiler_params=_cparams(("parallel", "arbitrary")),
        name="matmul",
    )(x, w)


def _permute_kernel(x_ref, *o_refs, dilations):
    for o_ref, dilation in zip(o_refs, dilations):
        rows = x_ref.shape[1] // dilation
        for bi in range(x_ref.shape[0]):
            for r in range(dilation):
                o_ref[bi, r] = x_ref[bi, pl.ds(r, rows, stride=dilation), :].astype(o_ref.dtype)


def _permute_rows(x, dilations):
    b, s, d = x.shape
    return pl.pallas_call(
        functools.partial(_permute_kernel, dilations=dilations),
        grid=(s // DSWA_TILE, d // LANES),
        in_specs=[pl.BlockSpec((b, DSWA_TILE, LANES), lambda n, c: (0, n, c))],
        out_specs=[pl.BlockSpec((b, dil, DSWA_TILE // dil, LANES), lambda n, c: (0, 0, n, c))
                   for dil in dilations],
        out_shape=[jax.ShapeDtypeStruct((b, dil, s // dil, d), BF16) for dil in dilations],
        compiler_params=_cparams(("parallel", "parallel")),
        name="permute",
    )(x)


def _dswa_kernel(q_ref, kc_ref, kp_ref, vc_ref, vp_ref, o_ref, lse_ref, o_scr, lse_scr, *, dilation):
    nq = q_ref.shape[2] // DSWA_BLK
    row = lax.broadcasted_iota(jnp.int32, (DSWA_BLK, DSWA_BLK), 0)
    col = lax.broadcasted_iota(jnp.int32, (DSWA_BLK, DSWA_BLK), 1)
    mask_cur = col <= row
    mask_band = col >= row
    mask_first = col >= row + jnp.where(pl.program_id(1) == 0, DSWA_BLK, 0)
    scale = 1.0 / math.sqrt(HEAD_DIM)
    bdot_nt = lambda a, b: jnp.einsum('hqd,hkd->hqk', a, b, preferred_element_type=F32)
    bdot = lambda a, b: jnp.einsum('hqk,hkd->hqd', a, b, preferred_element_type=F32)

    def heads(blocks):
        return jnp.stack([a[:, h * HEAD_DIM:(h + 1) * HEAD_DIM]
                          for a in blocks for h in range(DSWA_HEADS)], axis=0)

    def attend(blocks):
        qs, kps, kcs, vps, vcs = [], [], [], [], []
        for r, j in blocks:
            rows = slice(j * DSWA_BLK, (j + 1) * DSWA_BLK)
            prows = slice((j - 1) * DSWA_BLK, j * DSWA_BLK)
            qs.append(q_ref[0, r, rows, :])
            kcs.append(kc_ref[0, r, rows, :])
            vcs.append(vc_ref[0, r, rows, :])
            kps.append(kp_ref[0, r] if j == 0 else kc_ref[0, r, prows, :])
            vps.append(vp_ref[0, r] if j == 0 else vc_ref[0, r, prows, :])
        mask_prev = mask_first if blocks[0][1] == 0 else mask_band
        q4 = heads(qs)
        s_p = jnp.where(mask_prev[None], bdot_nt(q4, heads(kps)) * scale, NEG_INF)
        s_c = jnp.where(mask_cur[None], bdot_nt(q4, heads(kcs)) * scale, NEG_INF)
        m = jnp.maximum(jnp.max(s_p, axis=-1, keepdims=True), jnp.max(s_c, axis=-1, keepdims=True))
        p_p = jnp.exp(s_p - m)
        p_c = jnp.exp(s_c - m)
        l = jnp.sum(p_p, axis=-1, keepdims=True) + jnp.sum(p_c, axis=-1, keepdims=True)
        o = (bdot(p_p.astype(BF16), heads(vps)) + bdot(p_c.astype(BF16), heads(vcs))) / l
        lse = m + jnp.log(l)
        for bi, (r, j) in enumerate(blocks):
            dst = pl.ds(j * DSWA_BLK * dilation + r, DSWA_BLK, stride=dilation)
            lse_tile = jnp.zeros((DSWA_BLK, HEAD_DIM), F32)
            for h in range(DSWA_HEADS):
                o_scr[h, dst, :] = o[bi * DSWA_HEADS + h]
                lse_tile = jnp.where(col == h, lse[bi * DSWA_HEADS + h], lse_tile)
            lse_scr[dst, :] = lse_tile

    for first in (True, False):
        group = [(r, j) for j in range(nq) for r in range(dilation) if (j == 0) == first]
        for i in range(0, len(group), DSWA_BATCH_BLOCKS):
            attend(group[i:i + DSWA_BATCH_BLOCKS])
    for h in range(DSWA_HEADS):
        o_ref[0, :, h * HEAD_DIM:(h + 1) * HEAD_DIM] = o_scr[h].astype(o_ref.dtype)
    lse_ref[0] = lse_scr[...]


def _dswa_group(qkv, group):
    b, dilation, m_len, _ = qkv.shape
    s = dilation * m_len
    assert s % DSWA_TILE == 0 and DSWA_TILE % (dilation * DSWA_BLK) == 0
    rows = DSWA_TILE // dilation
    nq = rows // DSWA_BLK

    def cur(c):
        return pl.BlockSpec((1, dilation, rows, GROUP_W), lambda bi, n: (bi, 0, n, c))

    def prev(c):
        return pl.BlockSpec((1, dilation, DSWA_BLK, GROUP_W),
                            lambda bi, n: (bi, 0, jnp.maximum(n * nq - 1, 0), c))

    return pl.pallas_call(
        functools.partial(_dswa_kernel, dilation=dilation),
        grid=(b, s // DSWA_TILE),
        in_specs=[cur(0), cur(1), prev(1), cur(2), prev(2)],
        out_specs=[pl.BlockSpec((1, DSWA_TILE, GROUP_W), lambda bi, n: (bi, n, 0)),
                   pl.BlockSpec((1, DSWA_TILE, HEAD_DIM), lambda bi, n: (bi, n, 0))],
        out_shape=[jax.ShapeDtypeStruct((b, s, GROUP_W), BF16),
                   jax.ShapeDtypeStruct((b, s, HEAD_DIM), F32)],
        scratch_shapes=[pltpu.VMEM((DSWA_HEADS, DSWA_TILE, HEAD_DIM), F32),
                        pltpu.VMEM((DSWA_TILE, HEAD_DIM), F32)],
        compiler_params=_cparams(("parallel", "arbitrary")),
        name=f"dswa_g{group}",
    )(qkv, qkv, qkv, qkv, qkv)


def _linear_scan(a, b, h0):
    ts, d = a.shape
    a = a.reshape(ts // SUBLANES, SUBLANES, d)
    b = b.reshape(ts // SUBLANES, SUBLANES, d)
    row = lax.broadcasted_iota(jnp.int32, a.shape, 1)
    step = 1
    while step < SUBLANES:
        keep = row >= step
        a_s = jnp.where(keep, pltpu.roll(a, step, 1), 1.0)
        b_s = jnp.where(keep, pltpu.roll(b, step, 1), 0.0)
        b = a * b_s + b
        a = a * a_s
        step *= 2
    hs, carry = [], h0
    for g in range(ts // SUBLANES):
        hg = a[g] * carry + b[g]
        hs.append(hg)
        carry = hg[SUBLANES - 1:SUBLANES, :]
    return jnp.concatenate(hs, axis=0), carry


def _rglru_kernel(xb_ref, halo_ref, w_ref, cw_ref, cb_ref, wa_ref, ba_ref, wx_ref, bx_ref,
                  lam_ref, o_ref, h_ref, *, ts):
    si = pl.program_id(1)

    @pl.when(si == 0)
    def _():
        h_ref[...] = jnp.zeros_like(h_ref)

    z = _dot(xb_ref[0], w_ref[...])
    x = z[:, :D_RNN]
    yg = z[:, D_RNN:]
    halo = _dot(halo_ref[0], w_ref[:, :D_RNN])[BF16_ROWS - SUBLANES:, :]
    halo = jnp.where(si > 0, halo, 0.0)
    xfull = jnp.concatenate([halo, x], axis=0)
    xc = cb_ref[...] + cw_ref[CONV_W - 1:CONV_W, :] * x
    for k in range(1, CONV_W):
        xc = xc + cw_ref[CONV_W - 1 - k:CONV_W - k, :] * pltpu.roll(xfull, k, 0)[SUBLANES:]

    xb = xc.astype(BF16)
    nblk = D_RNN // MXU_TILE
    ra = jnp.concatenate(
        [_dot(xb[:, c * MXU_TILE:(c + 1) * MXU_TILE], wa_ref[c]) for c in range(nblk)], axis=1)
    rx = jnp.concatenate(
        [_dot(xb[:, c * MXU_TILE:(c + 1) * MXU_TILE], wx_ref[c]) for c in range(nblk)], axis=1)
    r = jax.nn.sigmoid(ra + ba_ref[...])
    gate_i = jax.nn.sigmoid(rx + bx_ref[...])
    neg_lam = -lam_ref[...]
    softplus = jnp.maximum(neg_lam, 0.0) + jnp.log1p(jnp.exp(-jnp.abs(neg_lam)))
    log_a = (-LRU_C) * r * softplus
    a = jnp.exp(log_a)
    bterm = jnp.sqrt(1.0 - a * a) * gate_i * xc

    h, carry = _linear_scan(a, bterm, h_ref[0:1, :])
    h_ref[0:1, :] = carry
    o_ref[0] = (h * jax.nn.gelu(yg, approximate=True)).astype(o_ref.dtype)


def _rglru(xb, w_rnn, conv_w, conv_b, wa_bd, ba, wx_bd, bx, lam, ts=1024):
    b, s, d = xb.shape
    assert s % ts == 0
    full = lambda shape: pl.BlockSpec(shape, lambda bi, si: (0,) * len(shape))
    return pl.pallas_call(
        functools.partial(_rglru_kernel, ts=ts),
        grid=(b, s // ts),
        in_specs=[pl.BlockSpec((1, ts, d), lambda bi, si: (bi, si, 0)),
                  pl.BlockSpec((1, BF16_ROWS, d),
                               lambda bi, si: (bi, jnp.maximum(si * (ts // BF16_ROWS) - 1, 0), 0)),
                  full(w_rnn.shape),
                  full((CONV_W, D_RNN)), full((1, D_RNN)),
                  full(wa_bd.shape), full((1, D_RNN)),
                  full(wx_bd.shape), full((1, D_RNN)), full((1, D_RNN))],
        out_specs=pl.BlockSpec((1, ts, D_RNN), lambda bi, si: (bi, si, 0)),
        out_shape=jax.ShapeDtypeStruct((b, s, D_RNN), BF16),
        scratch_shapes=[pltpu.VMEM((SUBLANES, D_RNN), F32)],
        compiler_params=_cparams(("parallel", "arbitrary")),
        name="rglru",
    )(xb, xb, w_rnn, conv_w, conv_b, wa_bd, ba, wx_bd, bx, lam)


def _memattn_kernel(xb_ref, wq_ref, kv_ref, o_ref):
    scale = 1.0 / math.sqrt(HEAD_DIM)
    q = _dot(xb_ref[0], wq_ref[...]).astype(BF16)
    for h in range(MEM_HEADS):
        hs = slice(h * HEAD_DIM, (h + 1) * HEAD_DIM)
        vs = slice(MEM_WIDTH + h * HEAD_DIM, MEM_WIDTH + (h + 1) * HEAD_DIM)
        s = _dot_nt(q[:, hs], kv_ref[0, :, hs]) * scale
        m = jnp.max(s, axis=-1, keepdims=True)
        p = jnp.exp(s - m)
        l = jnp.sum(p, axis=-1, keepdims=True)
        o_ref[0, :, hs] = (_dot(p.astype(BF16), kv_ref[0, :, vs]) / l).astype(o_ref.dtype)


def _memattn(xb, w_mq, kv, ts=1024):
    b, s, d = xb.shape
    mem_len = kv.shape[1]
    return pl.pallas_call(
        _memattn_kernel,
        grid=(b, s // ts),
        in_specs=[pl.BlockSpec((1, ts, d), lambda bi, si: (bi, si, 0)),
                  pl.BlockSpec(w_mq.shape, lambda bi, si: (0, 0), pipeline_mode=pl.Buffered(1)),
                  pl.BlockSpec((1, mem_len, 2 * MEM_WIDTH), lambda bi, si: (bi, 0, 0))],
        out_specs=pl.BlockSpec((1, ts, MEM_WIDTH), lambda bi, si: (bi, si, 0)),
        out_shape=jax.ShapeDtypeStruct((b, s, MEM_WIDTH), BF16),
        compiler_params=_cparams(("parallel", "arbitrary")),
        name="memattn",
    )(xb, w_mq, kv)


def _mix_kernel(o0_ref, o1_ref, o2_ref, l0_ref, l1_ref, l2_ref, rec_ref, memo_ref, x_ref,
                wg_ref, bg_ref, wa_ref, wl_ref, wm_ref, wo_ref, g_ref, b_ref, x1_ref, x1b_ref):
    l0, l1, l2 = l0_ref[...], l1_ref[...], l2_ref[...]
    mx = jnp.maximum(jnp.maximum(l0, l1), l2)
    e0, e1, e2 = jnp.exp(l0 - mx), jnp.exp(l1 - mx), jnp.exp(l2 - mx)
    inv = 1.0 / (e0 + e1 + e2)
    w0, w1, w2 = e0 * inv, e1 * inv, e2 * inv
    parts = []
    for h in range(DSWA_HEADS):
        hs = slice(h * HEAD_DIM, (h + 1) * HEAD_DIM)
        parts.append((w0[:, h:h + 1] * o0_ref[:, hs].astype(F32)
                      + w1[:, h:h + 1] * o1_ref[:, hs].astype(F32)
                      + w2[:, h:h + 1] * o2_ref[:, hs].astype(F32)).astype(BF16))
    attn = jnp.concatenate(parts, axis=1)
    d = D_MODEL
    gl = _dot(x_ref[...].astype(BF16), wg_ref[...])
    gate = lambda j: jax.nn.sigmoid(gl[:, j * d:(j + 1) * d] + bg_ref[j:j + 1, :])
    merged = (gate(0) * _dot(attn, wa_ref[...])
              + gate(1) * _dot(rec_ref[...], wl_ref[...])
              + gate(2) * _dot(memo_ref[...], wm_ref[...]))
    mix = _dot(merged.astype(BF16), wo_ref[...])
    x1 = _layernorm(ALPHA * x_ref[...] + mix, g_ref[...], b_ref[...])
    x1_ref[...] = x1
    x1b_ref[...] = x1.astype(BF16)


def _mix(os, lses, rec, memo, x, w_gate, b_gate, wa, wl, wm, wo, g, bta, tt=512):
    t = x.shape[0]
    rows = lambda w: pl.BlockSpec((tt, w), lambda i: (i, 0))
    full = lambda a: pl.BlockSpec(a.shape, lambda i: (0,) * a.ndim, pipeline_mode=pl.Buffered(1))
    return pl.pallas_call(
        _mix_kernel,
        grid=(t // tt,),
        in_specs=[rows(GROUP_W)] * 3 + [rows(HEAD_DIM)] * 3 + [
            rows(D_RNN), rows(MEM_WIDTH), rows(D_MODEL), full(w_gate),
            full(b_gate), full(wa), full(wl), full(wm), full(wo), full(g), full(bta)],
        out_specs=[rows(D_MODEL), rows(D_MODEL)],
        out_shape=[jax.ShapeDtypeStruct((t, D_MODEL), F32), jax.ShapeDtypeStruct((t, D_MODEL), BF16)],
        compiler_params=_cparams(("parallel",)),
        name="mix_ln1",
    )(*os, *lses, rec, memo, x, w_gate, b_gate, wa, wl, wm, wo, g, bta)


def _peer_scores_kernel(x_ref, wq_ref, kbd_ref, o_ref):
    q = _dot(x_ref[...], wq_ref[...]).astype(BF16)
    for h in range(PEER_HEADS):
        hs = slice(h * PEER_KEY_DIM, (h + 1) * PEER_KEY_DIM)
        o_ref[hs, :] = _dot_nt(kbd_ref[h], q[:, hs])


def _peer_scores(x1b, wq, kbd, tt=1024):
    t = x1b.shape[0]
    w = PEER_HEADS * PEER_KEY_DIM
    return pl.pallas_call(
        _peer_scores_kernel,
        grid=(t // tt,),
        in_specs=[pl.BlockSpec((tt, D_MODEL), lambda i: (i, 0)),
                  pl.BlockSpec(wq.shape, lambda i: (0, 0)),
                  pl.BlockSpec(kbd.shape, lambda i: (0, 0, 0))],
        out_specs=pl.BlockSpec((w, tt), lambda i: (0, i)),
        out_shape=jax.ShapeDtypeStruct((w, t), F32),
        compiler_params=_cparams(("parallel",)),
        name="peer_scores",
    )(x1b, wq, kbd)


_PEER_CANDS = tuple((p, q) for p in range(PEER_TOPK) for q in range(PEER_TOPK)
                    if (p + 1) * (q + 1) <= PEER_TOPK)


def _top_positions(s, exact_ties):
    key = lax.broadcasted_iota(jnp.int32, s.shape, 0).astype(F32)
    pos = jnp.full(s.shape, float(PEER_TOPK), F32)
    work = s
    tops = []
    for p in range(PEER_TOPK):
        mx = jnp.max(work, axis=0, keepdims=True)
        if exact_ties:
            first = jnp.min(jnp.where(work == mx, key, float(N_KEYS)), axis=0, keepdims=True)
            hit = key == first
        else:
            hit = work == mx
        pos = jnp.where(hit, float(p), pos)
        work = jnp.where(hit, -jnp.inf, work)
        tops.append(mx)
    return pos, tops


def _store_lane_tiles(ref, h, val):
    for c in range(ref.shape[0]):
        ref[c, h * N_KEYS:(h + 1) * N_KEYS, :] = val[:, c * LANES:(c + 1) * LANES]


def _peer_select_body(sc_ref, ra_ref, ea_ref, cb_ref, eb_ref, exact_ties):
    tt = sc_ref.shape[1]
    kg = N_KEYS // BF16_ROWS
    pos_b, tops_a, tops_b = [], [], []
    marked = jnp.zeros((1, tt), F32)
    for h in range(PEER_HEADS):
        base = h * PEER_KEY_DIM
        pa, ta = _top_positions(sc_ref[base:base + N_KEYS, :], exact_ties)
        pb, tb = _top_positions(sc_ref[base + N_KEYS:base + 2 * N_KEYS, :], exact_ties)
        _store_lane_tiles(ra_ref, h, pa)
        pos_b.append(pb)
        tops_a.append(ta)
        tops_b.append(tb)
        if not exact_ties:
            for pos in (pa, pb):
                n_marked = jnp.sum(jnp.where(pos < float(PEER_TOPK), 1.0, 0.0), axis=0, keepdims=True)
                marked = jnp.maximum(marked, n_marked)
    a_top = [jnp.concatenate([tops_a[h][p] for h in range(PEER_HEADS)], axis=0)
             for p in range(PEER_TOPK)]
    b_top = [jnp.concatenate([tops_b[h][q] for h in range(PEER_HEADS)], axis=0)
             for q in range(PEER_TOPK)]
    sums = [a_top[p] + b_top[q] for p, q in _PEER_CANDS]
    n = len(sums)
    rank = [jnp.full((PEER_HEADS, tt), float(n - 1 - c), F32) for c in range(n)]
    for c in range(n):
        for c2 in range(c):
            c2_wins = jnp.where(sums[c2] >= sums[c], 1.0, 0.0)
            rank[c] = rank[c] + c2_wins
            rank[c2] = rank[c2] - c2_wins
    cnt = [jnp.zeros((PEER_HEADS, tt), F32) for _ in range(PEER_TOPK)]
    zsum = jnp.zeros((PEER_HEADS, tt), F32)
    for c, (p, q) in enumerate(_PEER_CANDS):
        sel = rank[c] < float(PEER_TOPK)
        cnt[q] = cnt[q] + jnp.where(sel, 1.0, 0.0)
        zsum = zsum + jnp.where(sel, jnp.exp(sums[c] - sums[0]), 0.0)
    inv_z = 1.0 / zsum
    for h in range(PEER_HEADS):
        base = h * PEER_KEY_DIM
        rows = slice(h * N_KEYS, (h + 1) * N_KEYS)
        posb = pos_b[h].astype(BF16).reshape(kg, BF16_ROWS, tt)
        cb = jnp.zeros((kg, BF16_ROWS, tt), BF16)
        for q in range(PEER_TOPK):
            cnt_q = jnp.broadcast_to(cnt[q][h:h + 1, :], (BF16_ROWS, tt)).astype(BF16)
            cb = jnp.where(posb == float(q), cnt_q[None], cb)
        cb_ref[h * kg:(h + 1) * kg] = cb
        _store_lane_tiles(ea_ref, h, jnp.exp(sc_ref[base:base + N_KEYS, :] - a_top[0][h:h + 1, :])
                          * inv_z[h:h + 1, :])
        eb = jnp.exp(sc_ref[base + N_KEYS:base + 2 * N_KEYS, :] - b_top[0][h:h + 1, :])
        eb_ref[h * kg:(h + 1) * kg] = eb.astype(BF16).reshape(kg, BF16_ROWS, tt)
    return marked


def _peer_select_kernel(sc_ref, ra_ref, ea_ref, cb_ref, eb_ref):
    marked = _peer_select_body(sc_ref, ra_ref, ea_ref, cb_ref, eb_ref, exact_ties=False)

    @pl.when(jnp.max(marked) > float(PEER_TOPK))
    def _():
        _peer_select_body(sc_ref, ra_ref, ea_ref, cb_ref, eb_ref, exact_ties=True)


def _peer_select(sc, tt=256):
    w, t = sc.shape
    rows = PEER_HEADS * N_KEYS
    out_spec_a = pl.BlockSpec((tt // LANES, rows, LANES), lambda i: (i, 0, 0))
    shape_a = jax.ShapeDtypeStruct((t // LANES, rows, LANES), F32)
    out_spec_b = pl.BlockSpec((rows // BF16_ROWS, BF16_ROWS, tt), lambda i: (0, 0, i))
    shape_b = jax.ShapeDtypeStruct((rows // BF16_ROWS, BF16_ROWS, t), BF16)
    return pl.pallas_call(
        _peer_select_kernel,
        grid=(t // tt,),
        in_specs=[pl.BlockSpec((w, tt), lambda i: (0, i))],
        out_specs=[out_spec_a, out_spec_a, out_spec_b, out_spec_b],
        out_shape=[shape_a, shape_a, shape_b, shape_b],
        compiler_params=_cparams(("parallel",)),
        name="peer_select",
    )(sc)


def _row_on_sublanes(ref, h, i, lane0, lanes):
    tiles = range(lane0 // LANES, (lane0 + lanes) // LANES)
    return jnp.concatenate([ref[c, h, pl.ds(i, BF16_ROWS, stride=0), :] for c in tiles], axis=1)


def _masked_acts(hmat, ra_ref, ea_ref, cb_ref, eb_ref, key0):
    tt = hmat.shape[1]
    kg = N_KEYS // BF16_ROWS
    ys = []
    for k in range(hmat.shape[0] // N_KEYS):
        rows = slice(k * N_KEYS, (k + 1) * N_KEYS)
        ws = []
        for l0 in range(0, tt, PEER_MASK_LANES):
            ls = slice(l0, l0 + PEER_MASK_LANES)
            w = jnp.zeros((kg, BF16_ROWS, PEER_MASK_LANES), BF16)
            for h in range(PEER_HEADS):
                hs = slice(h * kg, (h + 1) * kg)
                ra = _row_on_sublanes(ra_ref, h, key0 + k, l0, PEER_MASK_LANES).astype(BF16)
                ea = _row_on_sublanes(ea_ref, h, key0 + k, l0, PEER_MASK_LANES).astype(BF16)
                w = w + eb_ref[hs, :, ls] * jnp.where(cb_ref[hs, :, ls] > ra[None], ea[None],
                                                      jnp.zeros_like(w))
            ws.append(w.reshape(N_KEYS, PEER_MASK_LANES))
        ys.append(jnp.concatenate(ws, axis=1) * _gelu_erf(hmat[rows, :]).astype(BF16))
    return jnp.concatenate(ys, axis=0)


def _peer_mix_kernel(x1b_ref, x1_ref, u_ref, vt_ref, ra_ref, ea_ref, cb_ref, eb_ref,
                     g_ref, b_ref, o_ref, acc_ref, *, ib):
    e = pl.program_id(1)

    @pl.when(e == 0)
    def _():
        acc_ref[...] = jnp.zeros_like(acc_ref)

    x1b = x1b_ref[...]
    acc = acc_ref[...]
    assert sum(PEER_CHUNK_KEYS) == ib
    starts = [sum(PEER_CHUNK_KEYS[:c]) for c in range(len(PEER_CHUNK_KEYS) + 1)]
    chunk_rows = lambda c: slice(starts[c] * N_KEYS, starts[c + 1] * N_KEYS)
    n_chunks = len(PEER_CHUNK_KEYS)
    h_next = _dot_nt(u_ref[chunk_rows(0), :], x1b)
    for c in range(n_chunks):
        hc = h_next
        if c + 1 < n_chunks:
            h_next = _dot_nt(u_ref[chunk_rows(c + 1), :], x1b)
        y = _masked_acts(hc, ra_ref, ea_ref, cb_ref, eb_ref, starts[c])
        acc = acc + _dot(vt_ref[:, chunk_rows(c)], y)
    acc_ref[...] = acc

    @pl.when(e == pl.num_programs(1) - 1)
    def _():
        o_ref[...] = _layernorm(ALPHA * x1_ref[...] + acc_ref[...].T, g_ref[...], b_ref[...])


def _peer_mix(x1b, x1, u, vt, ra, ea, cb, eb, g, bta, tt=512, te=sum(PEER_CHUNK_KEYS) * N_KEYS):
    t = x1.shape[0]
    ib = te // N_KEYS
    ne = N_EXPERTS // te
    rows = PEER_HEADS * N_KEYS
    return pl.pallas_call(
        functools.partial(_peer_mix_kernel, ib=ib),
        grid=(t // tt, ne),
        in_specs=[pl.BlockSpec((tt, D_MODEL), lambda i, e: (i, 0)),
                  pl.BlockSpec((tt, D_MODEL), lambda i, e: (i, 0)),
                  pl.BlockSpec((te, D_MODEL), lambda i, e: (e, 0)),
                  pl.BlockSpec((D_MODEL, te), lambda i, e: (0, e)),
                  pl.BlockSpec((tt // LANES, PEER_HEADS, ib, LANES), lambda i, e: (i, 0, e, 0)),
                  pl.BlockSpec((tt // LANES, PEER_HEADS, ib, LANES), lambda i, e: (i, 0, e, 0)),
                  pl.BlockSpec((rows // BF16_ROWS, BF16_ROWS, tt), lambda i, e: (0, 0, i)),
                  pl.BlockSpec((rows // BF16_ROWS, BF16_ROWS, tt), lambda i, e: (0, 0, i)),
                  pl.BlockSpec((1, D_MODEL), lambda i, e: (0, 0)),
                  pl.BlockSpec((1, D_MODEL), lambda i, e: (0, 0))],
        out_specs=pl.BlockSpec((tt, D_MODEL), lambda i, e: (i, 0)),
        out_shape=jax.ShapeDtypeStruct((t, D_MODEL), F32),
        scratch_shapes=[pltpu.VMEM((D_MODEL, tt), F32)],
        compiler_params=_cparams(("parallel", "arbitrary")),
        name="peer_mix_ln2",
    )(x1b, x1, u, vt, ra, ea, cb, eb, g, bta)


def _block_diag(w, per_tile):
    n, k, _ = w.shape
    w = w.reshape(n // per_tile, per_tile, k, k)
    eye = jnp.eye(per_tile, dtype=w.dtype)
    out = jnp.einsum('tpij,pq->tpiqj', w, eye)
    return out.reshape(n // per_tile, per_tile * k, per_tile * k)


def _layer(x, mem, w_in, b_gate, conv_w, conv_b, lru_wa, lru_ba, lru_wx, lru_bx, lru_lambda,
           w_mem_kv, w_br_attn, w_br_lru, w_br_mem, w_out, ln1_g, ln1_b,
           peer_wq, peer_keys, peer_u, peer_v, ln2_g, ln2_b):
    b, s, d = x.shape
    t = b * s
    xf = x.reshape(t, d)
    x_perm = [xp.reshape(t, d) for xp in _permute_rows(x, DSWA_DILATIONS)]
    xb = x_perm[DSWA_DILATIONS.index(1)]
    w_in_b = w_in.astype(BF16)
    row = lambda a: a.reshape(1, -1).astype(F32)

    off_rnn = 3 * DSWA_WIDTH
    off_mq = off_rnn + 2 * D_RNN
    off_gl = off_mq + MEM_WIDTH

    os, lses = [], []
    for gi, dil in enumerate(DSWA_DILATIONS):
        xp = x_perm[gi]
        w_g = jnp.concatenate([w_in_b[:, part * DSWA_WIDTH + gi * GROUP_W:
                                      part * DSWA_WIDTH + (gi + 1) * GROUP_W] for part in range(3)], axis=1)
        qkv = _matmul(xp, w_g, 0, 3 * GROUP_W, BF16, tm=2048, tn=3 * GROUP_W)
        qkv = qkv.reshape(b, dil, s // dil, 3 * GROUP_W)
        o_g, lse_g = _dswa_group(qkv, gi)
        os.append(o_g.reshape(t, GROUP_W))
        lses.append(lse_g.reshape(t, HEAD_DIM))

    per_tile = MXU_TILE // LRU_BW
    wa_bd = _block_diag(lru_wa, per_tile).astype(BF16)
    wx_bd = _block_diag(lru_wx, per_tile).astype(BF16)
    rec = _rglru(xb.reshape(b, s, d), w_in_b[:, off_rnn:off_mq], conv_w.astype(F32), row(conv_b),
                 wa_bd, row(lru_ba), wx_bd, row(lru_bx), row(lru_lambda)).reshape(t, D_RNN)

    mem_len = mem.shape[1]
    kv = _matmul(mem.reshape(b * mem_len, d).astype(BF16), w_mem_kv.astype(BF16), 0,
                 2 * MEM_WIDTH, BF16, tn=1024).reshape(b, mem_len, 2 * MEM_WIDTH)
    memo = _memattn(xb.reshape(b, s, d), w_in_b[:, off_mq:off_gl], kv).reshape(t, MEM_WIDTH)

    x1, x1b = _mix(os, lses, rec, memo, xf, w_in_b[:, off_gl:], b_gate.astype(F32), w_br_attn.astype(BF16),
                   w_br_lru.astype(BF16), w_br_mem.astype(BF16), w_out.astype(BF16),
                   row(ln1_g), row(ln1_b))

    kbd = _block_diag(peer_keys.reshape(PEER_HEADS * 2, N_KEYS, PEER_KEY_DIM // 2), 2).astype(BF16)
    scores = _peer_scores(x1b, peer_wq.astype(BF16), kbd)

    ra, ea, cb, eb = _peer_select(scores)
    ra = ra.reshape(t // LANES, PEER_HEADS, N_KEYS, LANES)
    ea = ea.reshape(t // LANES, PEER_HEADS, N_KEYS, LANES)

    out = _peer_mix(x1b, x1, peer_u.astype(BF16), peer_v.T.astype(BF16), ra, ea, cb, eb,
                    row(ln2_g), row(ln2_b))
    return out.reshape(b, s, d)


def kernel(x, mem, w_in, b_gate, conv_w, conv_b, lru_wa, lru_ba, lru_wx, lru_bx, lru_lambda, w_mem_kv, w_br_attn, w_br_lru, w_br_mem, w_out, ln1_g, ln1_b, peer_wq, peer_keys, peer_u, peer_v, ln2_g, ln2_b):
    h = x.astype(F32)
    depth = w_in.shape[0]
    for l in range(depth):
        h = _layer(h, mem, w_in[l], b_gate[l], conv_w[l], conv_b[l], lru_wa[l], lru_ba[l],
                   lru_wx[l], lru_bx[l], lru_lambda[l], w_mem_kv[l], w_br_attn[l], w_br_lru[l],
                   w_br_mem[l], w_out[l], ln1_g[l], ln1_b[l], peer_wq[l], peer_keys[l],
                   peer_u[l], peer_v[l], ln2_g[l], ln2_b[l])
    return h.astype(x.dtype)
```

```python
import functools
import math

import jax
import jax.numpy as jnp
from jax import lax
from jax.experimental import pallas as pl
from jax.experimental.pallas import tpu as pltpu

F32 = jnp.float32
BF16 = jnp.bfloat16

D_MODEL = 1024
N_GROUPS = 3
DSWA_DILATIONS = (1, 4, 16)
DSWA_HEADS = 4
HEAD_DIM = 128
DSWA_BLK = 128
DSWA_BATCH_BLOCKS = 4
DSWA_TILE = 2048
GROUP_W = DSWA_HEADS * HEAD_DIM
DSWA_WIDTH = N_GROUPS * GROUP_W
D_RNN = 1024
LRU_BLOCKS = 16
LRU_BW = D_RNN // LRU_BLOCKS
CONV_W = 4
LRU_C = 8.0
MEM_HEADS = 4
MEM_WIDTH = MEM_HEADS * HEAD_DIM
PEER_HEADS = 8
PEER_KEY_DIM = 256
N_KEYS = 128
N_EXPERTS = N_KEYS * N_KEYS
PEER_TOPK = 16
PEER_CHUNK_KEYS = (8,) * 2
BF16_ROWS = 16
PEER_MASK_LANES = 256
ALPHA = 2.0 ** 0.25
LN_EPS = 1e-5
NEG_INF = -1e30

LANES = 128
SUBLANES = 8
MXU_TILE = 256
VMEM_LIMIT = 56 * 1024 * 1024


def _cparams(sem):
    return pltpu.CompilerParams(dimension_semantics=sem, vmem_limit_bytes=VMEM_LIMIT)


def _dot(a, b):
    return jnp.dot(a, b, preferred_element_type=F32)


def _dot_nt(a, b):
    return lax.dot_general(a, b, (((1,), (1,)), ((), ())), preferred_element_type=F32)


def _gelu_erf(x):
    return 0.5 * x * (1.0 + lax.erf(x * (1.0 / math.sqrt(2.0))))


def _layernorm(h, g, b):
    mu = jnp.mean(h, axis=-1, keepdims=True)
    c = h - mu
    var = jnp.mean(c * c, axis=-1, keepdims=True)
    return c * lax.rsqrt(var + LN_EPS) * g + b


def _matmul_kernel(x_ref, w_ref, o_ref):
    o_ref[...] = _dot(x_ref[...], w_ref[...]).astype(o_ref.dtype)


def _matmul(x, w, col_off, n_cols, out_dtype, tm=1024, tn=1024):
    m, k = x.shape
    tn = min(tn, n_cols)
    tm = min(tm, m)
    assert m % tm == 0 and n_cols % tn == 0 and col_off % tn == 0
    off = col_off // tn
    return pl.pallas_call(
        _matmul_kernel,
        grid=(m // tm, n_cols // tn),
        in_specs=[pl.BlockSpec((tm, k), lambda i, j: (i, 0)),
                  pl.BlockSpec((k, tn), lambda i, j: (0, off + j))],
        out_specs=pl.BlockSpec((tm, tn), lambda i, j: (i, j)),
        out_shape=jax.ShapeDtypeStruct((m, n_cols), out_dtype),
        compiler_params=_cparams(("parallel", "arbitrary")),
        name="matmul",
    )(x, w)


def _permute_kernel(x_ref, *o_refs, dilations):
    for o_ref, dilation in zip(o_refs, dilations):
        rows = x_ref.shape[1] // dilation
        for bi in range(x_ref.shape[0]):
            for r in range(dilation):
                o_ref[bi, r] = x_ref[bi, pl.ds(r, rows, stride=dilation), :].astype(o_ref.dtype)


def _permute_rows(x, dilations):
    b, s, d = x.shape
    return pl.pallas_call(
        functools.partial(_permute_kernel, dilations=dilations),
        grid=(s // DSWA_TILE, d // LANES),
        in_specs=[pl.BlockSpec((b, DSWA_TILE, LANES), lambda n, c: (0, n, c))],
        out_specs=[pl.BlockSpec((b, dil, DSWA_TILE // dil, LANES), lambda n, c: (0, 0, n, c))
                   for dil in dilations],
        out_shape=[jax.ShapeDtypeStruct((b, dil, s // dil, d), BF16) for dil in dilations],
        compiler_params=_cparams(("parallel", "parallel")),
        name="permute",
    )(x)


def _dswa_kernel(q_ref, kc_ref, kp_ref, vc_ref, vp_ref, o_ref, lse_ref, o_scr, lse_scr, *, dilation):
    nq = q_ref.shape[2] // DSWA_BLK
    row = lax.broadcasted_iota(jnp.int32, (DSWA_BLK, DSWA_BLK), 0)
    col = lax.broadcasted_iota(jnp.int32, (DSWA_BLK, DSWA_BLK), 1)
    mask_cur = col <= row
    mask_band = col >= row
    mask_first = col >= row + jnp.where(pl.program_id(1) == 0, DSWA_BLK, 0)
    scale = 1.0 / math.sqrt(HEAD_DIM)
    bdot_nt = lambda a, b: jnp.einsum('hqd,hkd->hqk', a, b, preferred_element_type=F32)
    bdot = lambda a, b: jnp.einsum('hqk,hkd->hqd', a, b, preferred_element_type=F32)

    def heads(blocks):
        return jnp.stack([a[:, h * HEAD_DIM:(h + 1) * HEAD_DIM]
                          for a in blocks for h in range(DSWA_HEADS)], axis=0)

    def attend(blocks):
        qs, kps, kcs, vps, vcs = [], [], [], [], []
        for r, j in blocks:
            rows = slice(j * DSWA_BLK, (j + 1) * DSWA_BLK)
            prows = slice((j - 1) * DSWA_BLK, j * DSWA_BLK)
            qs.append(q_ref[0, r, rows, :])
            kcs.append(kc_ref[0, r, rows, :])
            vcs.append(vc_ref[0, r, rows, :])
            kps.append(kp_ref[0, r] if j == 0 else kc_ref[0, r, prows, :])
            vps.append(vp_ref[0, r] if j == 0 else vc_ref[0, r, prows, :])
        mask_prev = mask_first if blocks[0][1] == 0 else mask_band
        q4 = heads(qs)
        s_p = jnp.where(mask_prev[None], bdot_nt(q4, heads(kps)) * scale, NEG_INF)
        s_c = jnp.where(mask_cur[None], bdot_nt(q4, heads(kcs)) * scale, NEG_INF)
        m = jnp.maximum(jnp.max(s_p, axis=-1, keepdims=True), jnp.max(s_c, axis=-1, keepdims=True))
        p_p = jnp.exp(s_p - m)
        p_c = jnp.exp(s_c - m)
        l = jnp.sum(p_p, axis=-1, keepdims=True) + jnp.sum(p_c, axis=-1, keepdims=True)
        o = (bdot(p_p.astype(BF16), heads(vps)) + bdot(p_c.astype(BF16), heads(vcs))) / l
        lse = m + jnp.log(l)
        for bi, (r, j) in enumerate(blocks):
            dst = pl.ds(j * DSWA_BLK * dilation + r, DSWA_BLK, stride=dilation)
            lse_tile = jnp.zeros((DSWA_BLK, HEAD_DIM), F32)
            for h in range(DSWA_HEADS):
                o_scr[h, dst, :] = o[bi * DSWA_HEADS + h]
                lse_tile = jnp.where(col == h, lse[bi * DSWA_HEADS + h], lse_tile)
            lse_scr[dst, :] = lse_tile

    for first in (True, False):
        group = [(r, j) for j in range(nq) for r in range(dilation) if (j == 0) == first]
        for i in range(0, len(group), DSWA_BATCH_BLOCKS):
            attend(group[i:i + DSWA_BATCH_BLOCKS])
    for h in range(DSWA_HEADS):
        o_ref[0, :, h * HEAD_DIM:(h + 1) * HEAD_DIM] = o_scr[h].astype(o_ref.dtype)
    lse_ref[0] = lse_scr[...]


def _dswa_group(qkv, group):
    b, dilation, m_len, _ = qkv.shape
    s = dilation * m_len
    assert s % DSWA_TILE == 0 and DSWA_TILE % (dilation * DSWA_BLK) == 0
    rows = DSWA_TILE // dilation
    nq = rows // DSWA_BLK

    def cur(c):
        return pl.BlockSpec((1, dilation, rows, GROUP_W), lambda bi, n: (bi, 0, n, c))

    def prev(c):
        return pl.BlockSpec((1, dilation, DSWA_BLK, GROUP_W),
                            lambda bi, n: (bi, 0, jnp.maximum(n * nq - 1, 0), c))

    return pl.pallas_call(
        functools.partial(_dswa_kernel, dilation=dilation),
        grid=(b, s // DSWA_TILE),
        in_specs=[cur(0), cur(1), prev(1), cur(2), prev(2)],
        out_specs=[pl.BlockSpec((1, DSWA_TILE, GROUP_W), lambda bi, n: (bi, n, 0)),
                   pl.BlockSpec((1, DSWA_TILE, HEAD_DIM), lambda bi, n: (bi, n, 0))],
        out_shape=[jax.ShapeDtypeStruct((b, s, GROUP_W), BF16),
                   jax.ShapeDtypeStruct((b, s, HEAD_DIM), F32)],
        scratch_shapes=[pltpu.VMEM((DSWA_HEADS, DSWA_TILE, HEAD_DIM), F32),
                        pltpu.VMEM((DSWA_TILE, HEAD_DIM), F32)],
        compiler_params=_cparams(("parallel", "arbitrary")),
        name=f"dswa_g{group}",
    )(qkv, qkv, qkv, qkv, qkv)


def _linear_scan(a, b, h0):
    ts, d = a.shape
    a = a.reshape(ts // SUBLANES, SUBLANES, d)
    b = b.reshape(ts // SUBLANES, SUBLANES, d)
    row = lax.broadcasted_iota(jnp.int32, a.shape, 1)
    step = 1
    while step < SUBLANES:
        keep = row >= step
        a_s = jnp.where(keep, pltpu.roll(a, step, 1), 1.0)
        b_s = jnp.where(keep, pltpu.roll(b, step, 1), 0.0)
        b = a * b_s + b
        a = a * a_s
        step *= 2
    hs, carry = [], h0
    for g in range(ts // SUBLANES):
        hg = a[g] * carry + b[g]
        hs.append(hg)
        carry = hg[SUBLANES - 1:SUBLANES, :]
    return jnp.concatenate(hs, axis=0), carry


def _rglru_kernel(xb_ref, halo_ref, w_ref, cw_ref, cb_ref, wa_ref, ba_ref, wx_ref, bx_ref,
                  lam_ref, o_ref, h_ref, *, ts):
    si = pl.program_id(1)

    @pl.when(si == 0)
    def _():
        h_ref[...] = jnp.zeros_like(h_ref)

    z = _dot(xb_ref[0], w_ref[...])
    x = z[:, :D_RNN]
    yg = z[:, D_RNN:]
    halo = _dot(halo_ref[0], w_ref[:, :D_RNN])[BF16_ROWS - SUBLANES:, :]
    halo = jnp.where(si > 0, halo, 0.0)
    xfull = jnp.concatenate([halo, x], axis=0)
    xc = cb_ref[...] + cw_ref[CONV_W - 1:CONV_W, :] * x
    for k in range(1, CONV_W):
        xc = xc + cw_ref[CONV_W - 1 - k:CONV_W - k, :] * pltpu.roll(xfull, k, 0)[SUBLANES:]

    xb = xc.astype(BF16)
    nblk = D_RNN // MXU_TILE
    ra = jnp.concatenate(
        [_dot(xb[:, c * MXU_TILE:(c + 1) * MXU_TILE], wa_ref[c]) for c in range(nblk)], axis=1)
    rx = jnp.concatenate(
        [_dot(xb[:, c * MXU_TILE:(c + 1) * MXU_TILE], wx_ref[c]) for c in range(nblk)], axis=1)
    r = jax.nn.sigmoid(ra + ba_ref[...])
    gate_i = jax.nn.sigmoid(rx + bx_ref[...])
    neg_lam = -lam_ref[...]
    softplus = jnp.maximum(neg_lam, 0.0) + jnp.log1p(jnp.exp(-jnp.abs(neg_lam)))
    log_a = (-LRU_C) * r * softplus
    a = jnp.exp(log_a)
    bterm = jnp.sqrt(1.0 - a * a) * gate_i * xc

    h, carry = _linear_scan(a, bterm, h_ref[0:1, :])
    h_ref[0:1, :] = carry
    o_ref[0] = (h * jax.nn.gelu(yg, approximate=True)).astype(o_ref.dtype)


def _rglru(xb, w_rnn, conv_w, conv_b, wa_bd, ba, wx_bd, bx, lam, ts=1024):
    b, s, d = xb.shape
    assert s % ts == 0
    full = lambda shape: pl.BlockSpec(shape, lambda bi, si: (0,) * len(shape))
    return pl.pallas_call(
        functools.partial(_rglru_kernel, ts=ts),
        grid=(b, s // ts),
        in_specs=[pl.BlockSpec((1, ts, d), lambda bi, si: (bi, si, 0)),
                  pl.BlockSpec((1, BF16_ROWS, d),
                               lambda bi, si: (bi, jnp.maximum(si * (ts // BF16_ROWS) - 1, 0), 0)),
                  full(w_rnn.shape),
                  full((CONV_W, D_RNN)), full((1, D_RNN)),
                  full(wa_bd.shape), full((1, D_RNN)),
                  full(wx_bd.shape), full((1, D_RNN)), full((1, D_RNN))],
        out_specs=pl.BlockSpec((1, ts, D_RNN), lambda bi, si: (bi, si, 0)),
        out_shape=jax.ShapeDtypeStruct((b, s, D_RNN), BF16),
        scratch_shapes=[pltpu.VMEM((SUBLANES, D_RNN), F32)],
        compiler_params=_cparams(("parallel", "arbitrary")),
        name="rglru",
    )(xb, xb, w_rnn, conv_w, conv_b, wa_bd, ba, wx_bd, bx, lam)


def _memattn_kernel(xb_ref, wq_ref, kv_ref, o_ref):
    scale = 1.0 / math.sqrt(HEAD_DIM)
    q = _dot(xb_ref[0], wq_ref[...]).astype(BF16)
    for h in range(MEM_HEADS):
        hs = slice(h * HEAD_DIM, (h + 1) * HEAD_DIM)
        vs = slice(MEM_WIDTH + h * HEAD_DIM, MEM_WIDTH + (h + 1) * HEAD_DIM)
        s = _dot_nt(q[:, hs], kv_ref[0, :, hs]) * scale
        m = jnp.max(s, axis=-1, keepdims=True)
        p = jnp.exp(s - m)
        l = jnp.sum(p, axis=-1, keepdims=True)
        o_ref[0, :, hs] = (_dot(p.astype(BF16), kv_ref[0, :, vs]) / l).astype(o_ref.dtype)


def _memattn(xb, w_mq, kv, ts=1024):
    b, s, d = xb.shape
    mem_len = kv.shape[1]
    return pl.pallas_call(
        _memattn_kernel,
        grid=(b, s // ts),
        in_specs=[pl.BlockSpec((1, ts, d), lambda bi, si: (bi, si, 0)),
                  pl.BlockSpec(w_mq.shape, lambda bi, si: (0, 0), pipeline_mode=pl.Buffered(1)),
                  pl.BlockSpec((1, mem_len, 2 * MEM_WIDTH), lambda bi, si: (bi, 0, 0))],
        out_specs=pl.BlockSpec((1, ts, MEM_WIDTH), lambda bi, si: (bi, si, 0)),
        out_shape=jax.ShapeDtypeStruct((b, s, MEM_WIDTH), BF16),
        compiler_params=_cparams(("parallel", "arbitrary")),
        name="memattn",
    )(xb, w_mq, kv)


def _mix_kernel(o0_ref, o1_ref, o2_ref, l0_ref, l1_ref, l2_ref, rec_ref, memo_ref, x_ref,
                wg_ref, bg_ref, wa_ref, wl_ref, wm_ref, wo_ref, g_ref, b_ref, x1_ref, x1b_ref):
    l0, l1, l2 = l0_ref[...], l1_ref[...], l2_ref[...]
    mx = jnp.maximum(jnp.maximum(l0, l1), l2)
    e0, e1, e2 = jnp.exp(l0 - mx), jnp.exp(l1 - mx), jnp.exp(l2 - mx)
    inv = 1.0 / (e0 + e1 + e2)
    w0, w1, w2 = e0 * inv, e1 * inv, e2 * inv
    parts = []
    for h in range(DSWA_HEADS):
        hs = slice(h * HEAD_DIM, (h + 1) * HEAD_DIM)
        parts.append((w0[:, h:h + 1] * o0_ref[:, hs].astype(F32)
                      + w1[:, h:h + 1] * o1_ref[:, hs].astype(F32)
                      + w2[:, h:h + 1] * o2_ref[:, hs].astype(F32)).astype(BF16))
    attn = jnp.concatenate(parts, axis=1)
    d = D_MODEL
    gl = _dot(x_ref[...].astype(BF16), wg_ref[...])
    gate = lambda j: jax.nn.sigmoid(gl[:, j * d:(j + 1) * d] + bg_ref[j:j + 1, :])
    merged = (gate(0) * _dot(attn, wa_ref[...])
              + gate(1) * _dot(rec_ref[...], wl_ref[...])
              + gate(2) * _dot(memo_ref[...], wm_ref[...]))
    mix = _dot(merged.astype(BF16), wo_ref[...])
    x1 = _layernorm(ALPHA * x_ref[...] + mix, g_ref[...], b_ref[...])
    x1_ref[...] = x1
    x1b_ref[...] = x1.astype(BF16)


def _mix(os, lses, rec, memo, x, w_gate, b_gate, wa, wl, wm, wo, g, bta, tt=512):
    t = x.shape[0]
    rows = lambda w: pl.BlockSpec((tt, w), lambda i: (i, 0))
    full = lambda a: pl.BlockSpec(a.shape, lambda i: (0,) * a.ndim, pipeline_mode=pl.Buffered(1))
    return pl.pallas_call(
        _mix_kernel,
        grid=(t // tt,),
        in_specs=[rows(GROUP_W)] * 3 + [rows(HEAD_DIM)] * 3 + [
            rows(D_RNN), rows(MEM_WIDTH), rows(D_MODEL), full(w_gate),
            full(b_gate), full(wa), full(wl), full(wm), full(wo), full(g), full(bta)],
        out_specs=[rows(D_MODEL), rows(D_MODEL)],
        out_shape=[jax.ShapeDtypeStruct((t, D_MODEL), F32), jax.ShapeDtypeStruct((t, D_MODEL), BF16)],
        compiler_params=_cparams(("parallel",)),
        name="mix_ln1",
    )(*os, *lses, rec, memo, x, w_gate, b_gate, wa, wl, wm, wo, g, bta)


def _peer_scores_kernel(x_ref, wq_ref, kbd_ref, o_ref):
    q = _dot(x_ref[...], wq_ref[...]).astype(BF16)
    for h in range(PEER_HEADS):
        hs = slice(h * PEER_KEY_DIM, (h + 1) * PEER_KEY_DIM)
        o_ref[hs, :] = _dot_nt(kbd_ref[h], q[:, hs])


def _peer_scores(x1b, wq, kbd, tt=1024):
    t = x1b.shape[0]
    w = PEER_HEADS * PEER_KEY_DIM
    return pl.pallas_call(
        _peer_scores_kernel,
        grid=(t // tt,),
        in_specs=[pl.BlockSpec((tt, D_MODEL), lambda i: (i, 0)),
                  pl.BlockSpec(wq.shape, lambda i: (0, 0)),
                  pl.BlockSpec(kbd.shape, lambda i: (0, 0, 0))],
        out_specs=pl.BlockSpec((w, tt), lambda i: (0, i)),
        out_shape=jax.ShapeDtypeStruct((w, t), F32),
        compiler_params=_cparams(("parallel",)),
        name="peer_scores",
    )(x1b, wq, kbd)


_PEER_CANDS = tuple((p, q) for p in range(PEER_TOPK) for q in range(PEER_TOPK)
                    if (p + 1) * (q + 1) <= PEER_TOPK)


def _top_positions(s, exact_ties):
    key = lax.broadcasted_iota(jnp.int32, s.shape, 0).astype(F32)
    pos = jnp.full(s.shape, float(PEER_TOPK), F32)
    work = s
    tops = []
    for p in range(PEER_TOPK):
        mx = jnp.max(work, axis=0, keepdims=True)
        if exact_ties:
            first = jnp.min(jnp.where(work == mx, key, float(N_KEYS)), axis=0, keepdims=True)
            hit = key == first
        else:
            hit = work == mx
        pos = jnp.where(hit, float(p), pos)
        work = jnp.where(hit, -jnp.inf, work)
        tops.append(mx)
    return pos, tops


def _store_lane_tiles(ref, h, val):
    for c in range(ref.shape[0]):
        ref[c, h * N_KEYS:(h + 1) * N_KEYS, :] = val[:, c * LANES:(c + 1) * LANES]


def _peer_select_body(sc_ref, ra_ref, ea_ref, cb_ref, eb_ref, exact_ties):
    tt = sc_ref.shape[1]
    kg = N_KEYS // BF16_ROWS
    pos_b, tops_a, tops_b = [], [], []
    marked = jnp.zeros((1, tt), F32)
    for h in range(PEER_HEADS):
        base = h * PEER_KEY_DIM
        pa, ta = _top_positions(sc_ref[base:base + N_KEYS, :], exact_ties)
        pb, tb = _top_positions(sc_ref[base + N_KEYS:base + 2 * N_KEYS, :], exact_ties)
        _store_lane_tiles(ra_ref, h, pa)
        pos_b.append(pb)
        tops_a.append(ta)
        tops_b.append(tb)
        if not exact_ties:
            for pos in (pa, pb):
                n_marked = jnp.sum(jnp.where(pos < float(PEER_TOPK), 1.0, 0.0), axis=0, keepdims=True)
                marked = jnp.maximum(marked, n_marked)
    a_top = [jnp.concatenate([tops_a[h][p] for h in range(PEER_HEADS)], axis=0)
             for p in range(PEER_TOPK)]
    b_top = [jnp.concatenate([tops_b[h][q] for h in range(PEER_HEADS)], axis=0)
             for q in range(PEER_TOPK)]
    sums = [a_top[p] + b_top[q] for p, q in _PEER_CANDS]
    n = len(sums)
    rank = [jnp.full((PEER_HEADS, tt), float(n - 1 - c), F32) for c in range(n)]
    for c in range(n):
        for c2 in range(c):
            c2_wins = jnp.where(sums[c2] >= sums[c], 1.0, 0.0)
            rank[c] = rank[c] + c2_wins
            rank[c2] = rank[c2] - c2_wins
    cnt = [jnp.zeros((PEER_HEADS, tt), F32) for _ in range(PEER_TOPK)]
    zsum = jnp.zeros((PEER_HEADS, tt), F32)
    for c, (p, q) in enumerate(_PEER_CANDS):
        sel = rank[c] < float(PEER_TOPK)
        cnt[q] = cnt[q] + jnp.where(sel, 1.0, 0.0)
        zsum = zsum + jnp.where(sel, jnp.exp(sums[c] - sums[0]), 0.0)
    inv_z = 1.0 / zsum
    for h in range(PEER_HEADS):
        base = h * PEER_KEY_DIM
        rows = slice(h * N_KEYS, (h + 1) * N_KEYS)
        posb = pos_b[h].astype(BF16).reshape(kg, BF16_ROWS, tt)
        cb = jnp.zeros((kg, BF16_ROWS, tt), BF16)
        for q in range(PEER_TOPK):
            cnt_q = jnp.broadcast_to(cnt[q][h:h + 1, :], (BF16_ROWS, tt)).astype(BF16)
            cb = jnp.where(posb == float(q), cnt_q[None], cb)
        cb_ref[h * kg:(h + 1) * kg] = cb
        _store_lane_tiles(ea_ref, h, jnp.exp(sc_ref[base:base + N_KEYS, :] - a_top[0][h:h + 1, :])
                          * inv_z[h:h + 1, :])
        eb = jnp.exp(sc_ref[base + N_KEYS:base + 2 * N_KEYS, :] - b_top[0][h:h + 1, :])
        eb_ref[h * kg:(h + 1) * kg] = eb.astype(BF16).reshape(kg, BF16_ROWS, tt)
    return marked


def _peer_select_kernel(sc_ref, ra_ref, ea_ref, cb_ref, eb_ref):
    marked = _peer_select_body(sc_ref, ra_ref, ea_ref, cb_ref, eb_ref, exact_ties=False)

    @pl.when(jnp.max(marked) > float(PEER_TOPK))
    def _():
        _peer_select_body(sc_ref, ra_ref, ea_ref, cb_ref, eb_ref, exact_ties=True)


def _peer_select(sc, tt=256):
    w, t = sc.shape
    rows = PEER_HEADS * N_KEYS
    out_spec_a = pl.BlockSpec((tt // LANES, rows, LANES), lambda i: (i, 0, 0))
    shape_a = jax.ShapeDtypeStruct((t // LANES, rows, LANES), F32)
    out_spec_b = pl.BlockSpec((rows // BF16_ROWS, BF16_ROWS, tt), lambda i: (0, 0, i))
    shape_b = jax.ShapeDtypeStruct((rows // BF16_ROWS, BF16_ROWS, t), BF16)
    return pl.pallas_call(
        _peer_select_kernel,
        grid=(t // tt,),
        in_specs=[pl.BlockSpec((w, tt), lambda i: (0, i))],
        out_specs=[out_spec_a, out_spec_a, out_spec_b, out_spec_b],
        out_shape=[shape_a, shape_a, shape_b, shape_b],
        compiler_params=_cparams(("parallel",)),
        name="peer_select",
    )(sc)


def _row_on_sublanes(ref, h, i, lane0, lanes):
    tiles = range(lane0 // LANES, (lane0 + lanes) // LANES)
    return jnp.concatenate([ref[c, h, pl.ds(i, BF16_ROWS, stride=0), :] for c in tiles], axis=1)


def _masked_acts(hmat, ra_ref, ea_ref, cb_ref, eb_ref, key0):
    tt = hmat.shape[1]
    kg = N_KEYS // BF16_ROWS
    ys = []
    for k in range(hmat.shape[0] // N_KEYS):
        rows = slice(k * N_KEYS, (k + 1) * N_KEYS)
        ws = []
        for l0 in range(0, tt, PEER_MASK_LANES):
            ls = slice(l0, l0 + PEER_MASK_LANES)
            w = jnp.zeros((kg, BF16_ROWS, PEER_MASK_LANES), BF16)
            for h in range(PEER_HEADS):
                hs = slice(h * kg, (h + 1) * kg)
                ra = _row_on_sublanes(ra_ref, h, key0 + k, l0, PEER_MASK_LANES).astype(BF16)
                ea = _row_on_sublanes(ea_ref, h, key0 + k, l0, PEER_MASK_LANES).astype(BF16)
                w = w + eb_ref[hs, :, ls] * jnp.where(cb_ref[hs, :, ls] > ra[None], ea[None],
                                                      jnp.zeros_like(w))
            ws.append(w.reshape(N_KEYS, PEER_MASK_LANES))
        ys.append(jnp.concatenate(ws, axis=1) * _gelu_erf(hmat[rows, :]).astype(BF16))
    return jnp.concatenate(ys, axis=0)


def _peer_mix_kernel(x1b_ref, x1_ref, u_ref, vt_ref, ra_ref, ea_ref, cb_ref, eb_ref,
                     g_ref, b_ref, o_ref, acc_ref, *, ib):
    e = pl.program_id(1)

    @pl.when(e == 0)
    def _():
        acc_ref[...] = jnp.zeros_like(acc_ref)

    x1b = x1b_ref[...]
    acc = acc_ref[...]
    assert sum(PEER_CHUNK_KEYS) == ib
    starts = [sum(PEER_CHUNK_KEYS[:c]) for c in range(len(PEER_CHUNK_KEYS) + 1)]
    chunk_rows = lambda c: slice(starts[c] * N_KEYS, starts[c + 1] * N_KEYS)
    n_chunks = len(PEER_CHUNK_KEYS)
    h_next = _dot_nt(u_ref[chunk_rows(0), :], x1b)
    for c in range(n_chunks):
        hc = h_next
        if c + 1 < n_chunks:
            h_next = _dot_nt(u_ref[chunk_rows(c + 1), :], x1b)
        y = _masked_acts(hc, ra_ref, ea_ref, cb_ref, eb_ref, starts[c])
        acc = acc + _dot(vt_ref[:, chunk_rows(c)], y)
    acc_ref[...] = acc

    @pl.when(e == pl.num_programs(1) - 1)
    def _():
        o_ref[...] = _layernorm(ALPHA * x1_ref[...] + acc_ref[...].T, g_ref[...], b_ref[...])


def _peer_mix(x1b, x1, u, vt, ra, ea, cb, eb, g, bta, tt=512, te=sum(PEER_CHUNK_KEYS) * N_KEYS):
    t = x1.shape[0]
    ib = te // N_KEYS
    ne = N_EXPERTS // te
    rows = PEER_HEADS * N_KEYS
    return pl.pallas_call(
        functools.partial(_peer_mix_kernel, ib=ib),
        grid=(t // tt, ne),
        in_specs=[pl.BlockSpec((tt, D_MODEL), lambda i, e: (i, 0)),
                  pl.BlockSpec((tt, D_MODEL), lambda i, e: (i, 0)),
                  pl.BlockSpec((te, D_MODEL), lambda i, e: (e, 0)),
                  pl.BlockSpec((D_MODEL, te), lambda i, e: (0, e)),
                  pl.BlockSpec((tt // LANES, PEER_HEADS, ib, LANES), lambda i, e: (i, 0, e, 0)),
                  pl.BlockSpec((tt // LANES, PEER_HEADS, ib, LANES), lambda i, e: (i, 0, e, 0)),
                  pl.BlockSpec((rows // BF16_ROWS, BF16_ROWS, tt), lambda i, e: (0, 0, i)),
                  pl.BlockSpec((rows // BF16_ROWS, BF16_ROWS, tt), lambda i, e: (0, 0, i)),
                  pl.BlockSpec((1, D_MODEL), lambda i, e: (0, 0)),
                  pl.BlockSpec((1, D_MODEL), lambda i, e: (0, 0))],
        out_specs=pl.BlockSpec((tt, D_MODEL), lambda i, e: (i, 0)),
        out_shape=jax.ShapeDtypeStruct((t, D_MODEL), F32),
        scratch_shapes=[pltpu.VMEM((D_MODEL, tt), F32)],
        compiler_params=_cparams(("parallel", "arbitrary")),
        name="peer_mix_ln2",
    )(x1b, x1, u, vt, ra, ea, cb, eb, g, bta)


def _block_diag(w, per_tile):
    n, k, _ = w.shape
    w = w.reshape(n // per_tile, per_tile, k, k)
    eye = jnp.eye(per_tile, dtype=w.dtype)
    out = jnp.einsum('tpij,pq->tpiqj', w, eye)
    return out.reshape(n // per_tile, per_tile * k, per_tile * k)


def _layer(x, mem, w_in, b_gate, conv_w, conv_b, lru_wa, lru_ba, lru_wx, lru_bx, lru_lambda,
           w_mem_kv, w_br_attn, w_br_lru, w_br_mem, w_out, ln1_g, ln1_b,
           peer_wq, peer_keys, peer_u, peer_v, ln2_g, ln2_b):
    b, s, d = x.shape
    t = b * s
    xf = x.reshape(t, d)
    x_perm = [xp.reshape(t, d) for xp in _permute_rows(x, DSWA_DILATIONS)]
    xb = x_perm[DSWA_DILATIONS.index(1)]
    w_in_b = w_in.astype(BF16)
    row = lambda a: a.reshape(1, -1).astype(F32)

    off_rnn = 3 * DSWA_WIDTH
    off_mq = off_rnn + 2 * D_RNN
    off_gl = off_mq + MEM_WIDTH

    os, lses = [], []
    for gi, dil in enumerate(DSWA_DILATIONS):
        xp = x_perm[gi]
        w_g = jnp.concatenate([w_in_b[:, part * DSWA_WIDTH + gi * GROUP_W:
                                      part * DSWA_WIDTH + (gi + 1) * GROUP_W] for part in range(3)], axis=1)
        qkv = _matmul(xp, w_g, 0, 3 * GROUP_W, BF16, tn=3 * GROUP_W).reshape(b, dil, s // dil, 3 * GROUP_W)
        o_g, lse_g = _dswa_group(qkv, gi)
        os.append(o_g.reshape(t, GROUP_W))
        lses.append(lse_g.reshape(t, HEAD_DIM))

    per_tile = MXU_TILE // LRU_BW
    wa_bd = _block_diag(lru_wa, per_tile).astype(BF16)
    wx_bd = _block_diag(lru_wx, per_tile).astype(BF16)
    rec = _rglru(xb.reshape(b, s, d), w_in_b[:, off_rnn:off_mq], conv_w.astype(F32), row(conv_b),
                 wa_bd, row(lru_ba), wx_bd, row(lru_bx), row(lru_lambda)).reshape(t, D_RNN)

    mem_len = mem.shape[1]
    kv = _matmul(mem.reshape(b * mem_len, d).astype(BF16), w_mem_kv.astype(BF16), 0,
                 2 * MEM_WIDTH, BF16, tn=1024).reshape(b, mem_len, 2 * MEM_WIDTH)
    memo = _memattn(xb.reshape(b, s, d), w_in_b[:, off_mq:off_gl], kv).reshape(t, MEM_WIDTH)

    x1, x1b = _mix(os, lses, rec, memo, xf, w_in_b[:, off_gl:], b_gate.astype(F32), w_br_attn.astype(BF16),
                   w_br_lru.astype(BF16), w_br_mem.astype(BF16), w_out.astype(BF16),
                   row(ln1_g), row(ln1_b))

    kbd = _block_diag(peer_keys.reshape(PEER_HEADS * 2, N_KEYS, PEER_KEY_DIM // 2), 2).astype(BF16)
    scores = _peer_scores(x1b, peer_wq.astype(BF16), kbd)

    ra, ea, cb, eb = _peer_select(scores)
    ra = ra.reshape(t // LANES, PEER_HEADS, N_KEYS, LANES)
    ea = ea.reshape(t // LANES, PEER_HEADS, N_KEYS, LANES)

    out = _peer_mix(x1b, x1, peer_u.astype(BF16), peer_v.T.astype(BF16), ra, ea, cb, eb,
                    row(ln2_g), row(ln2_b))
    return out.reshape(b, s, d)


def kernel(x, mem, w_in, b_gate, conv_w, conv_b, lru_wa, lru_ba, lru_wx, lru_bx, lru_lambda, w_mem_kv, w_br_attn, w_br_lru, w_br_mem, w_out, ln1_g, ln1_b, peer_wq, peer_keys, peer_u, peer_v, ln2_g, ln2_b):
    h = x.astype(F32)
    depth = w_in.shape[0]
    for l in range(depth):
        h = _layer(h, mem, w_in[l], b_gate[l], conv_w[l], conv_b[l], lru_wa[l], lru_ba[l],
                   lru_wx[l], lru_bx[l], lru_lambda[l], w_mem_kv[l], w_br_attn[l], w_br_lru[l],
                   w_br_mem[l], w_out[l], ln1_g[l], ln1_b[l], peer_wq[l], peer_keys[l],
                   peer_u[l], peer_v[l], ln2_g[l], ln2_b[l])
    return h.astype(x.dtype)
```
